```python
import math
import jax, jax.numpy as jnp
from jax import lax
import numpy as np

D_MODEL = 1024
BATCH = 8
SEQ = 2048
DEPTH = 2
DEC_BATCH = 128
DEC_SEQ = 1
PAST_LEN = 16384
PAGE_SIZE = 128

GDN_HEAD_DIM = 128
GDN_WIDTH = D_MODEL // 2
GDN_HEADS = GDN_WIDTH // GDN_HEAD_DIM
GDN_CHUNK = 64
CONV_WIDTH = 4
S5_CH_PER_GROUP = 16
S5_WIDTH = D_MODEL - GDN_WIDTH
S5_GROUPS = S5_WIDTH // S5_CH_PER_GROUP
S5_STATE = 64
MIX_WIDTH = GDN_WIDTH + S5_WIDTH
QKV_WIDTH = 3 * GDN_WIDTH
IN_COLS = QKV_WIDTH + GDN_WIDTH + 2 * GDN_HEADS + S5_WIDTH
N_EXPERTS = 16
N_EXPERT_GROUPS = 4
EXPERTS_PER_GROUP = N_EXPERTS // N_EXPERT_GROUPS
TOP_K = 2
D_EXPERT = 256
NORM_EPS = 1e-6

kernel_name = 'hymba_gdn_s5_grouped_moe_step'


def rms_norm(x, g):
    xf = x.astype(jnp.float32)
    y = xf * lax.rsqrt(jnp.mean(xf * xf, axis=-1, keepdims=True) + NORM_EPS)
    return (y * g.astype(jnp.float32)).astype(x.dtype)


def l2_normalize(x):
    return x * lax.rsqrt(jnp.sum(x * x, axis=-1, keepdims=True) + NORM_EPS)


def causal_short_conv(x, buf, w):
    L = x.shape[1]
    xp = jnp.concatenate([buf.astype(x.dtype), x], axis=1)
    y = xp[:, 0:L] * w[0]
    for i in range(1, CONV_WIDTH):
        y = y + xp[:, i:i + L] * w[i]
    return jax.nn.silu(y), xp[:, -(CONV_WIDTH - 1):]


def gdn_chunked(q, k, v, g, beta, s0):
    Bn, L, H, DK = q.shape
    C = min(GDN_CHUNK, L)
    pad = (-L) % C
    if pad:
        pw = ((0, 0), (0, pad), (0, 0), (0, 0))
        q, k, v = jnp.pad(q, pw), jnp.pad(k, pw), jnp.pad(v, pw)
        g, beta = jnp.pad(g, pw[:3]), jnp.pad(beta, pw[:3])
    N = (L + pad) // C

    def blocks(t):
        return jnp.moveaxis(t.reshape((Bn, N, C) + t.shape[2:]), 3, 2)

    q, k, v, g, beta = blocks(q), blocks(k), blocks(v), blocks(g), blocks(beta)
    G = jnp.cumsum(g, axis=-1)
    pos = jnp.arange(C)
    incl = pos[:, None] >= pos[None, :]
    strict = pos[:, None] > pos[None, :]
    decay = jnp.exp(jnp.where(incl, G[..., :, None] - G[..., None, :], -jnp.inf))
    kk = jnp.einsum('bnhid,bnhjd->bnhij', k, k)
    lmat = jnp.where(strict, beta[..., :, None] * kk * decay, 0.0)
    eG = jnp.exp(G)
    rhs = jnp.concatenate([(beta * eG)[..., None] * k, beta[..., None] * v], axis=-1)
    sol = lax.linalg.triangular_solve(lmat + jnp.eye(C, dtype=lmat.dtype), rhs,
                                      left_side=True, lower=True, unit_diagonal=True)
    w_blk, u0_blk = sol[..., :DK], sol[..., DK:]
    attn = jnp.einsum('bnhid,bnhjd->bnhij', q, k) * decay
    qg = q * eG[..., None]
    kd = k * jnp.exp(G[..., -1:] - G)[..., None]
    g_last = eG[..., -1]

    def step(s, xs):
        w_c, u0_c, qg_c, attn_c, kd_c, gl_c = xs
        u = u0_c - jnp.einsum('bhcd,bhde->bhce', w_c, s)
        o = jnp.einsum('bhcd,bhde->bhce', qg_c, s) + jnp.einsum('bhij,bhje->bhie', attn_c, u)
        s = gl_c[..., None, None] * s + jnp.einsum('bhcd,bhce->bhde', kd_c, u)
        return s, o

    xs = (w_blk, u0_blk, qg, attn, kd, g_last)
    xs = tuple(jnp.moveaxis(t, 1, 0) for t in xs)
    s_final, o = lax.scan(step, s0, xs)
    o = jnp.moveaxis(jnp.moveaxis(o, 0, 1), 2, 3).reshape(Bn, N * C, H, -1)[:, :L]
    return o, s_final


def s5_scan(u, x0, lam_re, lam_im, log_dt, b_re, b_im, c_re, c_im, d):
    f32 = jnp.float32
    Bn, L, _ = u.shape
    uf = u.astype(f32).reshape(Bn, L, S5_GROUPS, S5_CH_PER_GROUP)
    lam = lax.complex(lam_re.astype(f32), lam_im.astype(f32))
    dt = jnp.exp(log_dt.astype(f32))[:, None]
    lam_bar = jnp.exp(lam * dt)
    b = lax.complex(b_re.astype(f32), b_im.astype(f32))
    b_bar = ((lam_bar - 1) / lam)[..., None] * b
    bu = jnp.einsum('blgc,gpc->blgp', uf.astype(jnp.complex64), b_bar)
    bu = bu.at[:, 0].add(lam_bar * x0)
    a = jnp.broadcast_to(lam_bar, bu.shape)

    def combine(e1, e2):
        a1, b1 = e1
        a2, b2 = e2
        return a2 * a1, a2 * b1 + b2

    _, xs = lax.associative_scan(combine, (a, bu), axis=1)
    c = lax.complex(c_re.astype(f32), c_im.astype(f32))
    y = jnp.real(jnp.einsum('blgp,gcp->blgc', xs, c))
    y = y + d.astype(f32).reshape(S5_GROUPS, S5_CH_PER_GROUP) * uf
    return y.reshape(Bn, L, S5_WIDTH), xs[:, -1]


def hybrid_mixer(h, conv_buf, s_gdn, x0_s5, w_in, conv_w, a_log, dt_bias, gdn_norm_g,
                 lam_re, lam_im, log_dt, b_re, b_im, c_re, c_im, s5_d, glu_w, glu_b,
                 s5_norm_g, w_out):
    f32 = jnp.float32
    Bn, L, _ = h.shape
    proj = h @ w_in
    cuts = [QKV_WIDTH, QKV_WIDTH + GDN_WIDTH, QKV_WIDTH + GDN_WIDTH + GDN_HEADS,
            QKV_WIDTH + GDN_WIDTH + 2 * GDN_HEADS]
    qkv, z, beta_raw, alpha_raw, u = jnp.split(proj, cuts, axis=-1)
    qkv, new_buf = causal_short_conv(qkv, conv_buf, conv_w)
    q, k, v = jnp.split(qkv.astype(f32), 3, axis=-1)
    hs = (Bn, L, GDN_HEADS, GDN_HEAD_DIM)
    q = l2_normalize(q.reshape(hs)) * (GDN_HEAD_DIM ** -0.5)
    k = l2_normalize(k.reshape(hs))
    v = v.reshape(hs)
    beta = jax.nn.sigmoid(beta_raw.astype(f32))
    g = -jnp.exp(a_log.astype(f32)) * jax.nn.softplus(alpha_raw.astype(f32) + dt_bias.astype(f32))
    o, s_new = gdn_chunked(q, k, v, g, beta, s_gdn.astype(f32))
    o = rms_norm(o, gdn_norm_g) * jax.nn.silu(z.astype(f32).reshape(hs))
    o_a = o.reshape(Bn, L, GDN_WIDTH).astype(h.dtype)
    y, x_new = s5_scan(u, x0_s5, lam_re, lam_im, log_dt, b_re, b_im, c_re, c_im, s5_d)
    y = jax.nn.gelu(y).astype(h.dtype)
    y = y * jax.nn.sigmoid(y @ glu_w + glu_b)
    o_b = rms_norm(y, s5_norm_g)
    out = jnp.concatenate([o_a, o_b], axis=-1) @ w_out
    return out, new_buf, s_new, x_new


def grouped_moe(h, w_router, b_router, w_gate, w_up, w_down):
    Bn, L, D = h.shape
    t = h.reshape(-1, D)
    T = t.shape[0]
    scores = jax.nn.sigmoid((t @ w_router).astype(jnp.float32))
    sel = scores + b_router.astype(jnp.float32)
    grp_top = lax.top_k(sel.reshape(T, N_EXPERT_GROUPS, EXPERTS_PER_GROUP), TOP_K)[0]
    best_group = jnp.argmax(jnp.sum(grp_top, axis=-1), axis=-1)
    in_group = (jnp.arange(N_EXPERTS) // EXPERTS_PER_GROUP)[None, :] == best_group[:, None]
    _, idx = lax.top_k(jnp.where(in_group, sel, -jnp.inf), TOP_K)
    wts = jnp.take_along_axis(scores, idx, axis=-1)
    wts = wts / jnp.sum(wts, axis=-1, keepdims=True)
    combine = jnp.sum(jax.nn.one_hot(idx, N_EXPERTS, dtype=jnp.float32) * wts[..., None], axis=1)
    hg = jnp.einsum('td,edf->tef', t, w_gate)
    hu = jnp.einsum('td,edf->tef', t, w_up)
    act = jax.nn.silu(hg) * hu * combine[..., None].astype(t.dtype)
    out = jnp.einsum('tef,efd->td', act, w_down)
    return out.reshape(Bn, L, D)


def trunk(x, c, st_conv, st_gdn, st_re, st_im, norm1_g, norm2_g, w_ada, b_ada, w_in, conv_w,
          a_log, dt_bias, gdn_norm_g, s5_lambda_re, s5_lambda_im, s5_log_dt, s5_b_re, s5_b_im,
          s5_c_re, s5_c_im, s5_d, s5_glu_w, s5_glu_b, s5_norm_g, w_out, w_router, b_router,
          w_gate, w_up, w_down, final_g):
    f32 = jnp.float32
    bufs, gdns, res, ims = [], [], [], []
    for l in range(DEPTH):
        mod = (jax.nn.silu(c) @ w_ada[l] + b_ada[l])[:, None, :]
        sh1, sc1, gt1, sh2, sc2, gt2 = jnp.split(mod, 6, axis=-1)
        hn = rms_norm(x, norm1_g[l]) * (1 + sc1) + sh1
        x0 = lax.complex(st_re[l].astype(f32), st_im[l].astype(f32))
        mix, nb, ns, nx = hybrid_mixer(
            hn, st_conv[l], st_gdn[l], x0, w_in[l], conv_w[l], a_log[l], dt_bias[l],
            gdn_norm_g[l], s5_lambda_re[l], s5_lambda_im[l], s5_log_dt[l], s5_b_re[l],
            s5_b_im[l], s5_c_re[l], s5_c_im[l], s5_d[l], s5_glu_w[l], s5_glu_b[l],
            s5_norm_g[l], w_out[l])
        x = x + gt1 * mix
        hn = rms_norm(x, norm2_g[l]) * (1 + sc2) + sh2
        x = x + gt2 * grouped_moe(hn, w_router, b_router, w_gate[l], w_up[l], w_down[l])
        bufs.append(nb)
        gdns.append(ns)
        res.append(jnp.real(nx))
        ims.append(jnp.imag(nx))
    y = rms_norm(x, final_g)
    return y, jnp.stack(bufs), jnp.stack(gdns), jnp.stack(res), jnp.stack(ims)


def setup_inputs(seed: int = 0) -> dict:
    key = jax.random.key(seed)
    ks = iter(jax.random.split(key, 48))
    f32 = jnp.float32

    def nrm(shape, scale):
        return scale * jax.random.normal(next(ks), shape, f32)

    def unif(shape, lo, hi):
        return jax.random.uniform(next(ks), shape, f32, lo, hi)

    x_prompt = nrm((BATCH, SEQ, D_MODEL), 1.0)
    x_sample = nrm((DEC_BATCH, DEC_SEQ, D_MODEL), 1.0)
    c_prompt = nrm((BATCH, D_MODEL), 1.0)
    c_sample = nrm((DEC_BATCH, D_MODEL), 1.0)
    state_conv = nrm((DEPTH, DEC_BATCH, CONV_WIDTH - 1, QKV_WIDTH), 1.0)
    state_gdn = nrm((DEPTH, DEC_BATCH, GDN_HEADS, GDN_HEAD_DIM, GDN_HEAD_DIM), 0.3)
    state_s5_re = nrm((DEPTH, DEC_BATCH, S5_GROUPS, S5_STATE), 0.1)
    state_s5_im = nrm((DEPTH, DEC_BATCH, S5_GROUPS, S5_STATE), 0.1)
    norm1_g = 1.0 + nrm((DEPTH, D_MODEL), 0.02)
    norm2_g = 1.0 + nrm((DEPTH, D_MODEL), 0.02)
    w_ada = nrm((DEPTH, D_MODEL, 6 * D_MODEL), 0.2 * D_MODEL ** -0.5)
    b_ada = nrm((DEPTH, 6 * D_MODEL), 0.02)
    w_in = nrm((DEPTH, D_MODEL, IN_COLS), D_MODEL ** -0.5)
    conv_w = nrm((DEPTH, CONV_WIDTH, QKV_WIDTH), CONV_WIDTH ** -0.5)
    a_log = jnp.log(unif((DEPTH, GDN_HEADS), 1.0, 16.0))
    dt = jnp.exp(unif((DEPTH, GDN_HEADS), math.log(1e-3), math.log(1e-1)))
    dt_bias = dt + jnp.log(-jnp.expm1(-dt))
    gdn_norm_g = 1.0 + nrm((DEPTH, GDN_HEAD_DIM), 0.02)
    n = jnp.arange(S5_STATE, dtype=f32)
    s5_lambda_re = -0.5 + nrm((DEPTH, S5_GROUPS, S5_STATE), 0.01)
    s5_lambda_im = math.pi * n + nrm((DEPTH, S5_GROUPS, S5_STATE), 0.01)
    s5_log_dt = unif((DEPTH, S5_GROUPS), math.log(1e-3), math.log(1e-1))
    b_scale = (2 * S5_CH_PER_GROUP) ** -0.5
    s5_b_re = nrm((DEPTH, S5_GROUPS, S5_STATE, S5_CH_PER_GROUP), b_scale)
    s5_b_im = nrm((DEPTH, S5_GROUPS, S5_STATE, S5_CH_PER_GROUP), b_scale)
    c_scale = (2 * S5_STATE) ** -0.5
    s5_c_re = nrm((DEPTH, S5_GROUPS, S5_CH_PER_GROUP, S5_STATE), c_scale)
    s5_c_im = nrm((DEPTH, S5_GROUPS, S5_CH_PER_GROUP, S5_STATE), c_scale)
    s5_d = nrm((DEPTH, S5_WIDTH), 1.0)
    s5_glu_w = nrm((DEPTH, S5_WIDTH, S5_WIDTH), S5_WIDTH ** -0.5)
    s5_glu_b = nrm((DEPTH, S5_WIDTH), 0.02)
    s5_norm_g = 1.0 + nrm((DEPTH, S5_WIDTH), 0.02)
    w_out = nrm((DEPTH, MIX_WIDTH, D_MODEL), MIX_WIDTH ** -0.5)
    w_router = nrm((D_MODEL, N_EXPERTS), D_MODEL ** -0.5)
    b_router = nrm((N_EXPERTS,), 0.01)
    w_gate = nrm((DEPTH, N_EXPERTS, D_MODEL, D_EXPERT), D_MODEL ** -0.5)
    w_up = nrm((DEPTH, N_EXPERTS, D_MODEL, D_EXPERT), D_MODEL ** -0.5)
    w_down = nrm((DEPTH, N_EXPERTS, D_EXPERT, D_MODEL), D_EXPERT ** -0.5)
    final_g = 1.0 + nrm((D_MODEL,), 0.02)
    return {
        'x_prompt': x_prompt, 'x_sample': x_sample, 'c_prompt': c_prompt, 'c_sample': c_sample,
        'state_conv': state_conv, 'state_gdn': state_gdn,
        'state_s5_re': state_s5_re, 'state_s5_im': state_s5_im,
        'norm1_g': norm1_g, 'norm2_g': norm2_g, 'w_ada': w_ada, 'b_ada': b_ada,
        'w_in': w_in, 'conv_w': conv_w, 'a_log': a_log, 'dt_bias': dt_bias,
        'gdn_norm_g': gdn_norm_g, 's5_lambda_re': s5_lambda_re, 's5_lambda_im': s5_lambda_im,
        's5_log_dt': s5_log_dt, 's5_b_re': s5_b_re, 's5_b_im': s5_b_im,
        's5_c_re': s5_c_re, 's5_c_im': s5_c_im, 's5_d': s5_d,
        's5_glu_w': s5_glu_w, 's5_glu_b': s5_glu_b, 's5_norm_g': s5_norm_g,
        'w_out': w_out, 'w_router': w_router, 'b_router': b_router,
        'w_gate': w_gate, 'w_up': w_up, 'w_down': w_down, 'final_g': final_g,
    }


def reference(x_prompt, x_sample, c_prompt, c_sample, state_conv, state_gdn, state_s5_re,
              state_s5_im, norm1_g, norm2_g, w_ada, b_ada, w_in, conv_w, a_log, dt_bias,
              gdn_norm_g, s5_lambda_re, s5_lambda_im, s5_log_dt, s5_b_re, s5_b_im, s5_c_re,
              s5_c_im, s5_d, s5_glu_w, s5_glu_b, s5_norm_g, w_out, w_router, b_router,
              w_gate, w_up, w_down, final_g):
    weights = (norm1_g, norm2_g, w_ada, b_ada, w_in, conv_w, a_log, dt_bias, gdn_norm_g,
               s5_lambda_re, s5_lambda_im, s5_log_dt, s5_b_re, s5_b_im, s5_c_re, s5_c_im,
               s5_d, s5_glu_w, s5_glu_b, s5_norm_g, w_out, w_router, b_router,
               w_gate, w_up, w_down, final_g)
    bp = x_prompt.shape[0]
    zero_conv = jnp.zeros((DEPTH, bp, CONV_WIDTH - 1, QKV_WIDTH), x_prompt.dtype)
    zero_gdn = jnp.zeros((DEPTH, bp, GDN_HEADS, GDN_HEAD_DIM, GDN_HEAD_DIM), jnp.float32)
    zero_s5 = jnp.zeros((DEPTH, bp, S5_GROUPS, S5_STATE), jnp.float32)
    y_prompt, conv_p, gdn_p, s5re_p, s5im_p = trunk(
        x_prompt, c_prompt, zero_conv, zero_gdn, zero_s5, zero_s5, *weights)
    y_sample, conv_s, gdn_s, s5re_s, s5im_s = trunk(
        x_sample, c_sample, state_conv, state_gdn, state_s5_re, state_s5_im, *weights)
    return (y_prompt, y_sample, conv_p, gdn_p, s5re_p, s5im_p, conv_s, gdn_s, s5re_s, s5im_s)
```

```python
import functools
import math

import jax
import jax.numpy as jnp
from jax import lax
from jax.experimental import pallas as pl
from jax.experimental.pallas import tpu as pltpu

F32 = jnp.float32
BF16 = jnp.bfloat16

D_MODEL = 1024
DEPTH = 2
GDN_HEAD_DIM = 128
GDN_WIDTH = 512
GDN_HEADS = 4
CONV_WIDTH = 4
S5_CH_PER_GROUP = 16
S5_WIDTH = 512
S5_GROUPS = 32
S5_STATE = 64
QKV_WIDTH = 3 * GDN_WIDTH
N_EXPERTS = 16
EXPERTS_PER_GROUP = 4
D_EXPERT = 256
NORM_EPS = 1e-6

LANES = 128
SUBLANES = 8
GDN_CHUNK = 128
S5_SLABS = 4
S5_SLAB_STATE = 512
VMEM_LIMIT = 56 * 1024 * 1024


def _cparams(sem):
    return pltpu.CompilerParams(dimension_semantics=sem, vmem_limit_bytes=VMEM_LIMIT)


def _sigmoid(x):
    return 1.0 / (1.0 + jnp.exp(-x))


def _silu(x):
    return x * _sigmoid(x)


def _softplus(x):
    return jnp.maximum(x, 0.0) + jnp.log1p(jnp.exp(-jnp.abs(x)))


def _mm(a, b):
    return jnp.dot(a.astype(BF16), b.astype(BF16), preferred_element_type=F32)


def _ada_kernel(c_ref, w_ref, b_ref, o_ref):
    c = c_ref[...]
    o_ref[0] = _mm(_silu(c), w_ref[0]) + b_ref[0]


def _ada(c_all, w_ada, b_ada):
    rows = c_all.shape[0]
    n_out = w_ada.shape[-1]
    tn = 1536
    return pl.pallas_call(
        _ada_kernel,
        grid=(DEPTH, n_out // tn),
        in_specs=[
            pl.BlockSpec((rows, D_MODEL), lambda l, j: (0, 0)),
            pl.BlockSpec((1, D_MODEL, tn), lambda l, j: (l, 0, j)),
            pl.BlockSpec((1, 1, tn), lambda l, j: (l, 0, j)),
        ],
        out_specs=pl.BlockSpec((1, rows, tn), lambda l, j: (l, 0, j)),
        out_shape=jax.ShapeDtypeStruct((DEPTH, rows, n_out), F32),
        compiler_params=_cparams(("arbitrary", "arbitrary")),
        name="ada_mod",
    )(c_all, w_ada, b_ada.reshape(DEPTH, 1, n_out))


def _inproj_kernel(x_ref, sc_ref, sh_ref, g_ref, wqkv_ref, wz_ref, wba_ref, wu_ref,
                   qkv_ref, z_ref, ba_ref, u_ref):
    x = x_ref[...]
    ms = jnp.mean(x * x, axis=-1, keepdims=True)
    hn = x * lax.rsqrt(ms + NORM_EPS) * g_ref[...]
    hn = hn * (1.0 + sc_ref[0]) + sh_ref[0]
    hb = hn.astype(BF16)
    qkv_ref[...] = jnp.dot(hb, wqkv_ref[...], preferred_element_type=F32)
    z_ref[...] = jnp.dot(hb, wz_ref[...], preferred_element_type=F32)
    ba_ref[...] = jnp.dot(hb, wba_ref[...], preferred_element_type=F32)
    u_ref[...] = jnp.dot(hb, wu_ref[...], preferred_element_type=F32)


def _mod_spec(mod, tm, rows_per_seq):
    if mod.shape[1] == 1:
        tiles_per_seq = rows_per_seq // tm
        return pl.BlockSpec((1, 1, D_MODEL), lambda i: (i // tiles_per_seq, 0, 0))
    return pl.BlockSpec((1, tm, D_MODEL), lambda i: (0, i, 0))


def _inproj(x, sc, sh, g, wqkv, wz, wba, wu, tm, rows_per_seq):
    t = x.shape[0]
    full = lambda a: pl.BlockSpec(a.shape, lambda i: (0,) * a.ndim)
    row = lambda n: pl.BlockSpec((tm, n), lambda i: (i, 0))
    return pl.pallas_call(
        _inproj_kernel,
        grid=(t // tm,),
        in_specs=[row(D_MODEL), _mod_spec(sc, tm, rows_per_seq), _mod_spec(sh, tm, rows_per_seq),
                  full(g), full(wqkv), full(wz), full(wba), full(wu)],
        out_specs=[row(QKV_WIDTH), row(GDN_WIDTH), row(LANES), row(S5_WIDTH)],
        out_shape=[jax.ShapeDtypeStruct((t, QKV_WIDTH), F32), jax.ShapeDtypeStruct((t, GDN_WIDTH), F32),
                   jax.ShapeDtypeStruct((t, LANES), F32), jax.ShapeDtypeStruct((t, S5_WIDTH), F32)],
        compiler_params=_cparams(("arbitrary",)),
        name="inproj",
    )(x, sc, sh, g, wqkv, wz, wba, wu)


def _gdn_gates(ba, alog, dtb):
    beta = _sigmoid(ba)
    g = -jnp.exp(alog) * _softplus(ba + dtb)
    return beta, g


def _l2n(x):
    return x * lax.rsqrt(jnp.sum(x * x, axis=-1, keepdims=True) + NORM_EPS)


def _gated_norm(o, gn, z):
    on = o * lax.rsqrt(jnp.mean(o * o, axis=-1, keepdims=True) + NORM_EPS) * gn
    return on * _silu(z)


def _gdn_prompt_kernel(qkv_ref, ba_ref, z_ref, cw_ref, alog_ref, dtb_ref, gn_ref, conv0_ref, s0_ref,
                       o_ref, sout_ref, convout_ref, xp_scr, y_scr, g_scr, b_scr, s_scr, *, tm):
    c_len = GDN_CHUNK
    i = pl.program_id(1)
    last = pl.num_programs(1) - 1

    @pl.when(i == 0)
    def _():
        xp_scr[0:SUBLANES, :] = conv0_ref[0]
        s_scr[...] = s0_ref[0]

    xp_scr[SUBLANES:SUBLANES + tm, :] = qkv_ref[...]
    cw = cw_ref[...]
    y = xp_scr[5:5 + tm, :] * cw[0:1, :]
    y = y + xp_scr[6:6 + tm, :] * cw[1:2, :]
    y = y + xp_scr[7:7 + tm, :] * cw[2:3, :]
    y = y + xp_scr[8:8 + tm, :] * cw[3:4, :]
    y_scr[...] = _silu(y)
    tail = xp_scr[tm:tm + SUBLANES, :]
    xp_scr[0:SUBLANES, :] = tail

    @pl.when(i == last)
    def _():
        convout_ref[0] = tail

    beta, g = _gdn_gates(ba_ref[...], alog_ref[...], dtb_ref[...])
    b_scr[...] = beta
    g_scr[...] = g

    r = lax.broadcasted_iota(jnp.int32, (c_len, c_len), 0)
    c = lax.broadcasted_iota(jnp.int32, (c_len, c_len), 1)
    ge = r >= c
    gt = r > c
    tri = jnp.where(ge, 1.0, 0.0).astype(BF16)
    eye = jnp.where(r == c, 1.0, 0.0).astype(F32)
    blk16 = (r // 16) == (c // 16)
    pair_masks = [((r // (2 * s)) == (c // (2 * s))) & ((r // s) != (c // s)) for s in (16, 32, 64)]
    gn = gn_ref[...]
    scale = GDN_HEAD_DIM ** -0.5

    def chunk(ci, carry):
        r0 = pl.multiple_of(ci * c_len, c_len)
        rows = pl.ds(r0, c_len)
        gch = g_scr[rows, :]
        bch = b_scr[rows, :]
        g1 = gch.astype(BF16)
        r1 = gch - g1.astype(F32)
        g2 = r1.astype(BF16)
        g3 = (r1 - g2.astype(F32)).astype(BF16)
        gcum = (jnp.dot(tri, g1, preferred_element_type=F32) + jnp.dot(tri, g2, preferred_element_type=F32)
                + jnp.dot(tri, g3, preferred_element_type=F32))
        gcum_t = gcum.T
        eg = jnp.exp(gcum)
        glast = gcum[c_len - 1:c_len, :]
        kdf = jnp.exp(glast - gcum)
        eglast = jnp.exp(glast)
        for h in range(GDN_HEADS):
            lo = h * GDN_HEAD_DIM
            q = _l2n(y_scr[rows, lo:lo + GDN_HEAD_DIM]) * scale
            k = _l2n(y_scr[rows, GDN_WIDTH + lo:GDN_WIDTH + lo + GDN_HEAD_DIM])
            v = y_scr[rows, 2 * GDN_WIDTH + lo:2 * GDN_WIDTH + lo + GDN_HEAD_DIM]
            gl = GDN_HEADS + h
            gc = gcum[:, gl:gl + 1]
            gr = gcum_t[gl:gl + 1, :]
            egc = eg[:, gl:gl + 1]
            bcol = bch[:, h:h + 1]
            decay = jnp.where(ge, jnp.exp(gc - gr), 0.0)
            kb = k.astype(BF16)
            kq = jnp.concatenate([kb, q.astype(BF16)], axis=0)
            kkqk = lax.dot_general(kq, kb, (((1,), (1,)), ((), ())), preferred_element_type=F32)
            lmat = jnp.where(gt, bcol * kkqk[:c_len] * decay, 0.0)
            n1 = jnp.where(blk16, -lmat, 0.0)
            n2 = _mm(n1, n1)
            n4 = _mm(n2, n2)
            n8 = _mm(n4, n4)
            tinv = eye + n1
            tinv = tinv + _mm(tinv, n2)
            tinv = tinv + _mm(tinv, n4)
            tinv = tinv + _mm(tinv, n8)
            for pm in pair_masks:
                bm = jnp.where(pm, lmat, 0.0)
                tinv = tinv - _mm(_mm(tinv, bm), tinv)
            rhs = jnp.concatenate([((bcol * egc) * k).astype(BF16), (bcol * v).astype(BF16)], axis=1)
            wu = jnp.dot(tinv.astype(BF16), rhs, preferred_element_type=F32)
            w = wu[:, :GDN_HEAD_DIM]
            u0 = wu[:, GDN_HEAD_DIM:]
            attn = kkqk[c_len:] * decay
            qg = q * egc
            kd = k * kdf[:, gl:gl + 1]
            s = s_scr[h]
            wsqs = jnp.dot(jnp.concatenate([w.astype(BF16), qg.astype(BF16)], axis=0), s.astype(BF16),
                           preferred_element_type=F32)
            u = u0 - wsqs[:c_len]
            auku = jnp.dot(jnp.concatenate([attn.astype(BF16), kd.T.astype(BF16)], axis=0), u.astype(BF16),
                           preferred_element_type=F32)
            o = wsqs[c_len:] + auku[:c_len]
            s_scr[h] = eglast[:, gl:gl + 1] * s + auku[c_len:]
            o_ref[rows, lo:lo + GDN_HEAD_DIM] = _gated_norm(o, gn, z_ref[rows, lo:lo + GDN_HEAD_DIM])
        return carry

    lax.fori_loop(0, tm // c_len, chunk, 0)

    @pl.when(i == last)
    def _():
        sout_ref[0] = s_scr[...]


def _gdn_prompt(qkv, ba, z, cw, alog, dtb, gn, conv0, s0, tm):
    n_seq = s0.shape[0]
    t = qkv.shape[0]
    nt = t // n_seq // tm
    row = lambda n: pl.BlockSpec((tm, n), lambda b, i: (b * nt + i, 0))
    full = lambda a: pl.BlockSpec(a.shape, lambda b, i: (0,) * a.ndim)
    hd = GDN_HEAD_DIM
    return pl.pallas_call(
        functools.partial(_gdn_prompt_kernel, tm=tm),
        grid=(n_seq, nt),
        in_specs=[row(QKV_WIDTH), row(LANES), row(GDN_WIDTH), full(cw), full(alog), full(dtb), full(gn),
                  pl.BlockSpec((1, SUBLANES, QKV_WIDTH), lambda b, i: (b, 0, 0)),
                  pl.BlockSpec((1, GDN_HEADS, hd, hd), lambda b, i: (b, 0, 0, 0))],
        out_specs=[row(GDN_WIDTH),
                   pl.BlockSpec((1, GDN_HEADS, hd, hd), lambda b, i: (b, 0, 0, 0)),
                   pl.BlockSpec((1, SUBLANES, QKV_WIDTH), lambda b, i: (b, 0, 0))],
        out_shape=[jax.ShapeDtypeStruct((t, GDN_WIDTH), F32),
                   jax.ShapeDtypeStruct((n_seq, GDN_HEADS, hd, hd), F32),
                   jax.ShapeDtypeStruct((n_seq, SUBLANES, QKV_WIDTH), F32)],
        scratch_shapes=[pltpu.VMEM((tm + SUBLANES, QKV_WIDTH), F32), pltpu.VMEM((tm, QKV_WIDTH), F32),
                        pltpu.VMEM((tm, LANES), F32), pltpu.VMEM((tm, LANES), F32),
                        pltpu.VMEM((GDN_HEADS, hd, hd), F32)],
        compiler_params=_cparams(("arbitrary", "arbitrary")),
        name="gdn_prompt",
    )(qkv, ba, z, cw, alog, dtb, gn, conv0, s0)


def _gdn_step_kernel(qkv_ref, ba_ref, z_ref, cw_ref, alog_ref, dtb_ref, gn_ref, conv_ref, s_ref,
                     o_ref, sout_ref, convout_ref):
    nb = SUBLANES
    hd = GDN_HEAD_DIM
    x = qkv_ref[...]
    cb = conv_ref[...]
    cw = cw_ref[...]
    b0 = cb[:, 0:QKV_WIDTH]
    b1 = cb[:, QKV_WIDTH:2 * QKV_WIDTH]
    b2 = cb[:, 2 * QKV_WIDTH:3 * QKV_WIDTH]
    y = b0 * cw[0:1, :]
    y = y + b1 * cw[1:2, :]
    y = y + b2 * cw[2:3, :]
    y = y + x * cw[3:4, :]
    y = _silu(y)
    convout_ref[...] = jnp.concatenate([b1, b2, x], axis=1)

    beta, g = _gdn_gates(ba_ref[...], alog_ref[...], dtb_ref[...])
    a = jnp.exp(g)
    gn = gn_ref[...]
    zpad = jnp.zeros((hd - nb, hd), F32)
    for h in range(GDN_HEADS):
        lo = h * hd
        q = _l2n(y[:, lo:lo + hd]) * (hd ** -0.5)
        k = _l2n(y[:, GDN_WIDTH + lo:GDN_WIDTH + lo + hd])
        v = y[:, 2 * GDN_WIDTH + lo:2 * GDN_WIDTH + lo + hd]
        kt = jnp.concatenate([k, zpad], axis=0).T
        qt = jnp.concatenate([q, zpad], axis=0).T
        kq = jnp.sum(k * q, axis=-1, keepdims=True)
        bh = beta[:, h:h + 1]
        ah = a[:, GDN_HEADS + h:GDN_HEADS + h + 1]
        o_rows = []
        for n in range(nb):
            s = s_ref[n, h]
            kc = kt[:, n:n + 1]
            qc = qt[:, n:n + 1]
            rk = jnp.sum(s * kc, axis=0, keepdims=True)
            rq = jnp.sum(s * qc, axis=0, keepdims=True)
            an = ah[n:n + 1, :]
            un = bh[n:n + 1, :] * (v[n:n + 1, :] - an * rk)
            sout_ref[n, h] = an * s + kc * un
            o_rows.append(an * rq + kq[n:n + 1, :] * un)
        o = jnp.concatenate(o_rows, axis=0)
        o_ref[:, lo:lo + hd] = _gated_norm(o, gn, z_ref[:, lo:lo + hd])


def _gdn_step(qkv, ba, z, cw, alog, dtb, gn, conv, s):
    n_seq = qkv.shape[0]
    nb = SUBLANES
    hd = GDN_HEAD_DIM
    row = lambda n: pl.BlockSpec((nb, n), lambda i: (i, 0))
    full = lambda a: pl.BlockSpec(a.shape, lambda i: (0,) * a.ndim)
    sspec = pl.BlockSpec((nb, GDN_HEADS, hd, hd), lambda i: (i, 0, 0, 0))
    return pl.pallas_call(
        _gdn_step_kernel,
        grid=(n_seq // nb,),
        in_specs=[row(QKV_WIDTH), row(LANES), row(GDN_WIDTH), full(cw), full(alog), full(dtb), full(gn),
                  row(3 * QKV_WIDTH), sspec],
        out_specs=[row(GDN_WIDTH), sspec, row(3 * QKV_WIDTH)],
        out_shape=[jax.ShapeDtypeStruct((n_seq, GDN_WIDTH), F32),
                   jax.ShapeDtypeStruct((n_seq, GDN_HEADS, hd, hd), F32),
                   jax.ShapeDtypeStruct((n_seq, 3 * QKV_WIDTH), F32)],
        compiler_params=_cparams(("arbitrary",)),
        name="gdn_step",
    )(qkv, ba, z, cw, alog, dtb, gn, conv, s)


def _s5_param_kernel(lre_ref, lim_ref, ldt_ref, bre_ref, bim_ref, are_ref, aim_ref, bbre_ref, bbim_ref):
    lre = lre_ref[...]
    lim = lim_ref[...]
    dt = jnp.exp(ldt_ref[...])
    mag = jnp.exp(lre * dt)
    are = mag * jnp.cos(lim * dt)
    aim = mag * jnp.sin(lim * dt)
    are_ref[...] = are
    aim_ref[...] = aim
    nre = are - 1.0
    den = lre * lre + lim * lim
    cre = (nre * lre + aim * lim) / den
    cim = (aim * lre - nre * lim) / den
    cre = cre[:, None, :]
    cim = cim[:, None, :]
    bre = bre_ref[...]
    bim = bim_ref[...]
    bbre_ref[...] = cre * bre - cim * bim
    bbim_ref[...] = cre * bim + cim * bre


def _s5_params(lam_re, lam_im, log_dt, b_re, b_im):
    g, p = lam_re.shape
    bt_re = jnp.swapaxes(b_re, 1, 2)
    bt_im = jnp.swapaxes(b_im, 1, 2)
    cg = bt_re.shape[1]
    return pl.pallas_call(
        _s5_param_kernel,
        out_shape=[jax.ShapeDtypeStruct((g, p), F32), jax.ShapeDtypeStruct((g, p), F32),
                   jax.ShapeDtypeStruct((g, cg, p), F32), jax.ShapeDtypeStruct((g, cg, p), F32)],
        name="s5_params",
    )(lam_re, lam_im, log_dt.reshape(g, 1), bt_re, bt_im)


def _slab_blockdiag(m):
    g, cg, p = m.shape
    gl = g // S5_SLABS
    m4 = m.reshape(S5_SLABS, gl, cg, p)
    eye = jnp.eye(gl, dtype=m.dtype)
    return jnp.einsum('igcp,gh->igchp', m4, eye).reshape(S5_SLABS, gl * cg, gl * p)


def _slab_vec(v):
    g, p = v.shape
    return v.reshape(S5_SLABS, (g // S5_SLABS) * p)


def _gelu_tanh(x):
    return 0.5 * x * (1.0 + jnp.tanh(math.sqrt(2.0 / math.pi) * (x + 0.044715 * (x * x * x))))


def _s5_kernel(u_ref, x0_ref, a_ref, bmat_ref, cre_ref, cim_ref, d_ref, gluw_ref, glub_ref, ng_ref,
               o_ref, xout_ref, utb_scr, xs_scr, x_scr, y_scr, *, nb, tt, interleave):
    rows = nb * tt
    ss = S5_SLAB_STATE
    i = pl.program_id(0)

    @pl.when(i == 0)
    def _():
        x_scr[...] = x0_ref[...]

    if interleave:
        for b in range(nb):
            for s in range(S5_SLABS):
                utb_scr[pl.ds(s * rows + b, tt, stride=nb), :] = u_ref[b, :, s * LANES:(s + 1) * LANES]
    else:
        for s in range(S5_SLABS):
            utb_scr[s * rows:(s + 1) * rows, :] = u_ref[:, s * LANES:(s + 1) * LANES]

    for s in range(S5_SLABS):
        xs_scr[:, s * 2 * ss:(s + 1) * 2 * ss] = jnp.dot(utb_scr[s * rows:(s + 1) * rows, :].astype(BF16),
                                                       bmat_ref[s], preferred_element_type=F32)

    def step(t, carry):
        r0 = pl.multiple_of(t * nb, nb)
        for s in range(S5_SLABS):
            lo = s * 2 * ss
            ar = a_ref[0:1, lo:lo + ss]
            ai = a_ref[0:1, lo + ss:lo + 2 * ss]
            xr = x_scr[:, lo:lo + ss]
            xi = x_scr[:, lo + ss:lo + 2 * ss]
            nr = (ar * xr - ai * xi) + xs_scr[pl.ds(r0, nb), lo:lo + ss]
            ni = (ar * xi + ai * xr) + xs_scr[pl.ds(r0, nb), lo + ss:lo + 2 * ss]
            x_scr[:, lo:lo + ss] = nr
            x_scr[:, lo + ss:lo + 2 * ss] = ni
            xs_scr[pl.ds(r0, nb), lo:lo + ss] = nr
            xs_scr[pl.ds(r0, nb), lo + ss:lo + 2 * ss] = ni
        return carry

    lax.fori_loop(0, tt, step, 0)

    @pl.when(i == pl.num_programs(0) - 1)
    def _():
        xout_ref[...] = x_scr[...]

    ys = []
    for s in range(S5_SLABS):
        lo = s * 2 * ss
        yr = jnp.dot(xs_scr[:, lo:lo + ss].astype(BF16), cre_ref[s], preferred_element_type=F32)
        yi = jnp.dot(xs_scr[:, lo + ss:lo + 2 * ss].astype(BF16), cim_ref[s], preferred_element_type=F32)
        ys.append((yr - yi) + d_ref[0:1, s * LANES:(s + 1) * LANES] * utb_scr[s * rows:(s + 1) * rows, :])
    y = _gelu_tanh(jnp.concatenate(ys, axis=1))
    y = y * _sigmoid(_mm(y, gluw_ref[...]) + glub_ref[...])
    y = y * lax.rsqrt(jnp.mean(y * y, axis=-1, keepdims=True) + NORM_EPS) * ng_ref[...]
    if interleave:
        for s in range(S5_SLABS):
            y_scr[s * rows:(s + 1) * rows, :] = y[:, s * LANES:(s + 1) * LANES]
        for b in range(nb):
            for s in range(S5_SLABS):
                o_ref[b, :, s * LANES:(s + 1) * LANES] = y_scr[pl.ds(s * rows + b, tt, stride=nb), :]
    else:
        o_ref[...] = y


def _s5(u, x0, a, bmat, cre, cim, d, gluw, glub, ng, nb, tt, interleave):
    rows = nb * tt
    nstate = x0.shape[1]
    full = lambda arr: pl.BlockSpec(arr.shape, lambda i: (0,) * arr.ndim)
    if interleave:
        steps = u.shape[1] // tt
        uspec = pl.BlockSpec((nb, tt, S5_WIDTH), lambda i: (0, i, 0))
        oshape = jax.ShapeDtypeStruct(u.shape, F32)
    else:
        steps = 1
        uspec = pl.BlockSpec((nb, S5_WIDTH), lambda i: (0, 0))
        oshape = jax.ShapeDtypeStruct(u.shape, F32)
    return pl.pallas_call(
        functools.partial(_s5_kernel, nb=nb, tt=tt, interleave=interleave),
        grid=(steps,),
        in_specs=[uspec, full(x0), full(a), full(bmat), full(cre), full(cim), full(d), full(gluw), full(glub),
                  full(ng)],
        out_specs=[uspec, full(x0)],
        out_shape=[oshape, jax.ShapeDtypeStruct(x0.shape, F32)],
        scratch_shapes=[pltpu.VMEM((S5_SLABS * rows, LANES), F32), pltpu.VMEM((rows, nstate), F32),
                        pltpu.VMEM((nb, nstate), F32), pltpu.VMEM((S5_SLABS * rows, LANES), F32)],
        compiler_params=_cparams(("arbitrary",)),
        name="s5_scan",
    )(u, x0, a, bmat, cre, cim, d, gluw, glub, ng)


def _router(logits_t, bias_col):
    scores = _sigmoid(logits_t)
    sel = scores + bias_col
    s = [sel[e:e + 1, :] for e in range(N_EXPERTS)]
    n_groups = N_EXPERTS // EXPERTS_PER_GROUP
    gs = []
    for gi in range(n_groups):
        m = s[gi * EXPERTS_PER_GROUP: (gi + 1) * EXPERTS_PER_GROUP]
        best = None
        for p in range(EXPERTS_PER_GROUP):
            for q in range(p + 1, EXPERTS_PER_GROUP):
                ps = m[p] + m[q]
                best = ps if best is None else jnp.maximum(best, ps)
        gs.append(best)
    gmax = functools.reduce(jnp.maximum, gs)
    taken = None
    in_best = []
    for gi in range(n_groups):
        hit = gs[gi] == gmax
        if taken is None:
            cur = hit
            taken = hit
        else:
            cur = jnp.logical_and(hit, jnp.logical_not(taken))
            taken = jnp.logical_or(taken, hit)
        in_best.append(cur)
    picked = []
    for e in range(N_EXPERTS):
        gi = e // EXPERTS_PER_GROUP
        cnt = jnp.zeros_like(s[e])
        for j in range(gi * EXPERTS_PER_GROUP, (gi + 1) * EXPERTS_PER_GROUP):
            if j == e:
                continue
            beats = (s[j] >= s[e]) if j < e else (s[j] > s[e])
            cnt = cnt + jnp.where(beats, 1.0, 0.0)
        sel_e = jnp.logical_and(in_best[gi], cnt < 1.5)
        picked.append(jnp.where(sel_e, scores[e:e + 1, :], 0.0))
    denom = functools.reduce(lambda x, y: x + y, picked)
    return jnp.concatenate([p / denom for p in picked], axis=0)


def _outproj_kernel(oa_ref, ob_ref, x_ref, gt_ref, sc_ref, sh_ref, g_ref, wout_ref, wrt_ref, br_ref,
                    xo_ref, hn_ref, comb_ref, *, tm):
    o = jnp.concatenate([oa_ref[...].astype(BF16), ob_ref[...].astype(BF16)], axis=1)
    mix = jnp.dot(o, wout_ref[...], preferred_element_type=F32)
    x = x_ref[...] + gt_ref[0] * mix
    xo_ref[...] = x
    ms = jnp.mean(x * x, axis=-1, keepdims=True)
    hn = x * lax.rsqrt(ms + NORM_EPS) * g_ref[...]
    hn = hn * (1.0 + sc_ref[0]) + sh_ref[0]
    hb = hn.astype(BF16)
    hn_ref[...] = hb
    logits_t = lax.dot_general(wrt_ref[...], hb, (((1,), (1,)), ((), ())), preferred_element_type=F32)
    comb_t = _router(logits_t, br_ref[...])
    pad = jnp.zeros((LANES - N_EXPERTS, tm), F32)
    comb_ref[...] = jnp.concatenate([comb_t, pad], axis=0).T


def _outproj(oa, ob, x, gt, sc, sh, g, wout, wrt, br, tm, rows_per_seq):
    t = x.shape[0]
    full = lambda a: pl.BlockSpec(a.shape, lambda i: (0,) * a.ndim)
    row = lambda n: pl.BlockSpec((tm, n), lambda i: (i, 0))
    ms = lambda m: _mod_spec(m, tm, rows_per_seq)
    return pl.pallas_call(
        functools.partial(_outproj_kernel, tm=tm),
        grid=(t // tm,),
        in_specs=[row(GDN_WIDTH), row(S5_WIDTH), row(D_MODEL), ms(gt), ms(sc), ms(sh), full(g), full(wout),
                  full(wrt), full(br)],
        out_specs=[row(D_MODEL), row(D_MODEL), row(LANES)],
        out_shape=[jax.ShapeDtypeStruct((t, D_MODEL), F32), jax.ShapeDtypeStruct((t, D_MODEL), BF16),
                   jax.ShapeDtypeStruct((t, LANES), F32)],
        compiler_params=_cparams(("arbitrary",)),
        name="outproj_router",
    )(oa, ob, x, gt, sc, sh, g, wout, wrt, br)


def _moe_kernel(hn_ref, comb_ref, x_ref, gt_ref, wgu_ref, wd_ref, fg_ref, o_ref, acc_scr, *, final_norm):
    e = pl.program_id(1)

    @pl.when(e == 0)
    def _():
        acc_scr[...] = jnp.zeros_like(acc_scr)

    h = jnp.dot(hn_ref[...], wgu_ref[0], preferred_element_type=F32)
    hg = h[:, :D_EXPERT]
    hu = h[:, D_EXPERT:]
    lane = lax.broadcasted_iota(jnp.int32, (1, LANES), 1)
    ce = jnp.sum(jnp.where(lane == e, comb_ref[...], 0.0), axis=-1, keepdims=True)
    act = _silu(hg) * hu * ce
    acc_scr[...] += jnp.dot(act.astype(BF16), wd_ref[0], preferred_element_type=F32)

    @pl.when(e == pl.num_programs(1) - 1)
    def _():
        x = x_ref[...] + gt_ref[0] * acc_scr[...]
        if final_norm:
            x = x * lax.rsqrt(jnp.mean(x * x, axis=-1, keepdims=True) + NORM_EPS) * fg_ref[...]
        o_ref[...] = x


def _moe(hn, comb, x, gt, wgu, wd, fg, tm, rows_per_seq, final_norm):
    t = x.shape[0]
    row = lambda n: pl.BlockSpec((tm, n), lambda i, e: (i, 0))
    if gt.shape[1] == 1:
        tiles_per_seq = rows_per_seq // tm
        gspec = pl.BlockSpec((1, 1, D_MODEL), lambda i, e: (i // tiles_per_seq, 0, 0))
    else:
        gspec = pl.BlockSpec((1, tm, D_MODEL), lambda i, e: (0, i, 0))
    return pl.pallas_call(
        functools.partial(_moe_kernel, final_norm=final_norm),
        grid=(t // tm, N_EXPERTS),
        in_specs=[row(D_MODEL), row(LANES), row(D_MODEL), gspec,
                  pl.BlockSpec((1, D_MODEL, 2 * D_EXPERT), lambda i, e: (e, 0, 0)),
                  pl.BlockSpec((1, D_EXPERT, D_MODEL), lambda i, e: (e, 0, 0)),
                  pl.BlockSpec(fg.shape, lambda i, e: (0, 0))],
        out_specs=row(D_MODEL),
        out_shape=jax.ShapeDtypeStruct((t, D_MODEL), F32),
        scratch_shapes=[pltpu.VMEM((tm, D_MODEL), F32)],
        compiler_params=_cparams(("arbitrary", "arbitrary")),
        name="moe",
    )(hn, comb, x, gt, wgu, wd, fg)


def _pad_lanes(v, offset):
    return jnp.zeros((1, LANES), F32).at[0, offset:offset + v.shape[0]].set(v)


def _state_to_slab(re, im):
    b = re.shape[0]
    r = re.reshape(b, S5_SLABS, S5_SLAB_STATE)
    i = im.reshape(b, S5_SLABS, S5_SLAB_STATE)
    return jnp.stack([r, i], axis=2).reshape(b, S5_SLABS * 2 * S5_SLAB_STATE)


def _slab_to_state(x):
    b = x.shape[0]
    x4 = x.reshape(b, S5_SLABS, 2, S5_SLAB_STATE)
    re = x4[:, :, 0].reshape(b, S5_GROUPS, S5_STATE)
    im = x4[:, :, 1].reshape(b, S5_GROUPS, S5_STATE)
    return re, im


def kernel(x_prompt, x_sample, c_prompt, c_sample, state_conv, state_gdn, state_s5_re, state_s5_im, norm1_g, norm2_g, w_ada, b_ada, w_in, conv_w, a_log, dt_bias, gdn_norm_g, s5_lambda_re, s5_lambda_im, s5_log_dt, s5_b_re, s5_b_im, s5_c_re, s5_c_im, s5_d, s5_glu_w, s5_glu_b, s5_norm_g, w_out, w_router, b_router, w_gate, w_up, w_down, final_g):
    bp, seq, _ = x_prompt.shape
    bs = x_sample.shape[0]
    tp = bp * seq

    mod = _ada(jnp.concatenate([c_prompt, c_sample], axis=0), w_ada, b_ada)

    def mod_slices(l):
        mp = [mod[l, :bp, j * D_MODEL:(j + 1) * D_MODEL].reshape(bp, 1, D_MODEL) for j in range(6)]
        msm = [mod[l, bp:, j * D_MODEL:(j + 1) * D_MODEL].reshape(1, bs, D_MODEL) for j in range(6)]
        return mp, msm

    wrt = jnp.transpose(w_router).astype(BF16)
    br = b_router.reshape(N_EXPERTS, 1)
    fg = final_g.reshape(1, D_MODEL)

    xp = x_prompt.reshape(tp, D_MODEL)
    xs = x_sample.reshape(bs, D_MODEL)
    tm_p = 512
    tm_moe = 1024 if seq % 1024 == 0 else tm_p
    tt = 64 if seq % 64 == 0 else seq

    outs_p = {k: [] for k in ("conv", "gdn", "re", "im")}
    outs_s = {k: [] for k in ("conv", "gdn", "re", "im")}
    zero_conv = jnp.zeros((bp, SUBLANES, QKV_WIDTH), F32)
    zero_gdn = jnp.zeros((bp, GDN_HEADS, GDN_HEAD_DIM, GDN_HEAD_DIM), F32)
    zero_s5 = jnp.zeros((bp, S5_SLABS * 2 * S5_SLAB_STATE), F32)

    for l in range(DEPTH):
        (sh1p, sc1p, gt1p, sh2p, sc2p, gt2p), (sh1s, sc1s, gt1s, sh2s, sc2s, gt2s) = mod_slices(l)
        wl = w_in[l]
        wqkv = wl[:, :QKV_WIDTH].astype(BF16)
        wz = wl[:, QKV_WIDTH:QKV_WIDTH + GDN_WIDTH].astype(BF16)
        nba = 2 * GDN_HEADS
        wba = jnp.zeros((D_MODEL, LANES), F32).at[:, :nba].set(
            wl[:, QKV_WIDTH + GDN_WIDTH:QKV_WIDTH + GDN_WIDTH + nba]).astype(BF16)
        wu = wl[:, QKV_WIDTH + GDN_WIDTH + nba:].astype(BF16)
        g1 = norm1_g[l].reshape(1, D_MODEL)
        g2 = norm2_g[l].reshape(1, D_MODEL)
        cw = conv_w[l]
        alog = _pad_lanes(a_log[l], GDN_HEADS)
        dtb = _pad_lanes(dt_bias[l], GDN_HEADS)
        gn = gdn_norm_g[l].reshape(1, GDN_HEAD_DIM)
        a_re, a_im, bb_re, bb_im = _s5_params(s5_lambda_re[l], s5_lambda_im[l], s5_log_dt[l], s5_b_re[l], s5_b_im[l])
        a_vec = jnp.concatenate([_slab_vec(a_re), _slab_vec(a_im)], axis=1).reshape(1, -1)
        bmat = jnp.concatenate([_slab_blockdiag(bb_re), _slab_blockdiag(bb_im)], axis=2).astype(BF16)
        cre = jnp.swapaxes(_slab_blockdiag(s5_c_re[l]), 1, 2).astype(BF16)
        cim = jnp.swapaxes(_slab_blockdiag(s5_c_im[l]), 1, 2).astype(BF16)
        dvec = s5_d[l].reshape(1, S5_WIDTH)
        gluw = s5_glu_w[l].astype(BF16)
        glub = s5_glu_b[l].reshape(1, S5_WIDTH)
        ng = s5_norm_g[l].reshape(1, S5_WIDTH)
        wout = w_out[l].astype(BF16)
        wgu = jnp.concatenate([w_gate[l], w_up[l]], axis=-1).astype(BF16)
        wd = w_down[l].astype(BF16)
        last = l == DEPTH - 1

        qkv, z, ba, u = _inproj(xp, sc1p, sh1p, g1, wqkv, wz, wba, wu, tm_p, seq)
        oa, sg, cv = _gdn_prompt(qkv, ba, z, cw, alog, dtb, gn, zero_conv, zero_gdn, tm_p)
        ob, xst = _s5(u.reshape(bp, seq, S5_WIDTH), zero_s5, a_vec, bmat, cre, cim, dvec, gluw, glub, ng,
                      bp, tt, True)
        xp, hn, comb = _outproj(oa, ob.reshape(tp, S5_WIDTH), xp, gt1p, sc2p, sh2p, g2, wout, wrt, br, tm_p, seq)
        xp = _moe(hn, comb, xp, gt2p, wgu, wd, fg, tm_moe, seq, last)
        outs_p["conv"].append(cv[:, SUBLANES - (CONV_WIDTH - 1):, :])
        outs_p["gdn"].append(sg)
        re, im = _slab_to_state(xst)
        outs_p["re"].append(re)
        outs_p["im"].append(im)

        qkv, z, ba, u = _inproj(xs, sc1s, sh1s, g1, wqkv, wz, wba, wu, bs, 1)
        oa, sg, cv = _gdn_step(qkv, ba, z, cw, alog, dtb, gn,
                               state_conv[l].reshape(bs, (CONV_WIDTH - 1) * QKV_WIDTH), state_gdn[l])
        ob, xst = _s5(u, _state_to_slab(state_s5_re[l], state_s5_im[l]), a_vec, bmat, cre, cim, dvec, gluw,
                      glub, ng, bs, 1, False)
        xs, hn, comb = _outproj(oa, ob, xs, gt1s, sc2s, sh2s, g2, wout, wrt, br, bs, 1)
        xs = _moe(hn, comb, xs, gt2s, wgu, wd, fg, bs, 1, last)
        outs_s["conv"].append(cv.reshape(bs, CONV_WIDTH - 1, QKV_WIDTH))
        outs_s["gdn"].append(sg)
        re, im = _slab_to_state(xst)
        outs_s["re"].append(re)
        outs_s["im"].append(im)

    st = lambda d, k: jnp.stack(d[k])
    return (xp.reshape(bp, seq, D_MODEL), xs.reshape(bs, 1, D_MODEL),
            st(outs_p, "conv"), st(outs_p, "gdn"), st(outs_p, "re"), st(outs_p, "im"),
            st(outs_s, "conv"), st(outs_s, "gdn"), st(outs_s, "re"), st(outs_s, "im"))
```

```python
import functools
import math

import jax
import jax.numpy as jnp
from jax import lax
from jax.experimental import pallas as pl
from jax.experimental.pallas import tpu as pltpu

F32 = jnp.float32
BF16 = jnp.bfloat16

D_MODEL = 1024
DEPTH = 2
GDN_HEAD_DIM = 128
GDN_WIDTH = 512
GDN_HEADS = 4
CONV_WIDTH = 4
S5_CH_PER_GROUP = 16
S5_WIDTH = 512
S5_GROUPS = 32
S5_STATE = 64
QKV_WIDTH = 3 * GDN_WIDTH
N_EXPERTS = 16
EXPERTS_PER_GROUP = 4
D_EXPERT = 256
NORM_EPS = 1e-6

LANES = 128
SUBLANES = 8
GDN_CHUNK = 128
GDN_CHUNKS_PER_ITER = 2
S5_SLABS = 4
S5_SLAB_STATE = 512
VMEM_LIMIT = 56 * 1024 * 1024


def _cparams(sem):
    return pltpu.CompilerParams(dimension_semantics=sem, vmem_limit_bytes=VMEM_LIMIT)


def _sigmoid(x):
    return 1.0 / (1.0 + jnp.exp(-x))


def _silu(x):
    return x * _sigmoid(x)


def _softplus(x):
    return jnp.maximum(x, 0.0) + jnp.log1p(jnp.exp(-jnp.abs(x)))


def _mm(a, b):
    return jnp.dot(a.astype(BF16), b.astype(BF16), preferred_element_type=F32)


def _ada_kernel(c_ref, w_ref, b_ref, o_ref):
    c = c_ref[...]
    o_ref[0] = _mm(_silu(c), w_ref[0]) + b_ref[0]


def _ada(c_all, w_ada, b_ada):
    rows = c_all.shape[0]
    n_out = w_ada.shape[-1]
    tn = 1536
    return pl.pallas_call(
        _ada_kernel,
        grid=(DEPTH, n_out // tn),
        in_specs=[
            pl.BlockSpec((rows, D_MODEL), lambda l, j: (0, 0)),
            pl.BlockSpec((1, D_MODEL, tn), lambda l, j: (l, 0, j)),
            pl.BlockSpec((1, 1, tn), lambda l, j: (l, 0, j)),
        ],
        out_specs=pl.BlockSpec((1, rows, tn), lambda l, j: (l, 0, j)),
        out_shape=jax.ShapeDtypeStruct((DEPTH, rows, n_out), F32),
        compiler_params=_cparams(("arbitrary", "arbitrary")),
        name="ada_mod",
    )(c_all, w_ada, b_ada.reshape(DEPTH, 1, n_out))


def _inproj_kernel(x_ref, sc_ref, sh_ref, g_ref, wqkv_ref, wz_ref, wba_ref, wu_ref,
                   qkv_ref, z_ref, ba_ref, u_ref):
    x = x_ref[...]
    ms = jnp.mean(x * x, axis=-1, keepdims=True)
    hn = x * lax.rsqrt(ms + NORM_EPS) * g_ref[...]
    hn = hn * (1.0 + sc_ref[0]) + sh_ref[0]
    hb = hn.astype(BF16)
    qkv_ref[...] = jnp.dot(hb, wqkv_ref[...], preferred_element_type=F32)
    z_ref[...] = jnp.dot(hb, wz_ref[...], preferred_element_type=F32)
    ba_ref[...] = jnp.dot(hb, wba_ref[...], preferred_element_type=F32)
    u_ref[...] = jnp.dot(hb, wu_ref[...], preferred_element_type=F32)


def _mod_spec(mod, tm, rows_per_seq):
    if mod.shape[1] == 1:
        tiles_per_seq = rows_per_seq // tm
        return pl.BlockSpec((1, 1, D_MODEL), lambda i: (i // tiles_per_seq, 0, 0))
    return pl.BlockSpec((1, tm, D_MODEL), lambda i: (0, i, 0))


def _inproj(x, sc, sh, g, wqkv, wz, wba, wu, tm, rows_per_seq):
    t = x.shape[0]
    full = lambda a: pl.BlockSpec(a.shape, lambda i: (0,) * a.ndim)
    row = lambda n: pl.BlockSpec((tm, n), lambda i: (i, 0))
    return pl.pallas_call(
        _inproj_kernel,
        grid=(t // tm,),
        in_specs=[row(D_MODEL), _mod_spec(sc, tm, rows_per_seq), _mod_spec(sh, tm, rows_per_seq),
                  full(g), full(wqkv), full(wz), full(wba), full(wu)],
        out_specs=[row(QKV_WIDTH), row(GDN_WIDTH), row(LANES), row(S5_WIDTH)],
        out_shape=[jax.ShapeDtypeStruct((t, QKV_WIDTH), F32), jax.ShapeDtypeStruct((t, GDN_WIDTH), F32),
                   jax.ShapeDtypeStruct((t, LANES), F32), jax.ShapeDtypeStruct((t, S5_WIDTH), F32)],
        compiler_params=_cparams(("arbitrary",)),
        name="inproj",
    )(x, sc, sh, g, wqkv, wz, wba, wu)


def _gdn_gates(ba, alog, dtb):
    beta = _sigmoid(ba)
    g = -jnp.exp(alog) * _softplus(ba + dtb)
    return beta, g


def _l2n(x):
    return x * lax.rsqrt(jnp.sum(x * x, axis=-1, keepdims=True) + NORM_EPS)


def _gated_norm(o, gn, z):
    on = o * lax.rsqrt(jnp.mean(o * o, axis=-1, keepdims=True) + NORM_EPS) * gn
    return on * _silu(z)


def _gdn_prompt_kernel(qkv_ref, ba_ref, z_ref, cw_ref, alog_ref, dtb_ref, gn_ref, conv0_ref, s0_ref,
                       o_ref, sout_ref, convout_ref, xp_scr, y_scr, g_scr, b_scr, s_scr, wq_scr, ak_scr, u0_scr,
                       egl_scr, *, tm):
    c_len = GDN_CHUNK
    i = pl.program_id(1)
    last = pl.num_programs(1) - 1

    @pl.when(i == 0)
    def _():
        xp_scr[0:SUBLANES, :] = conv0_ref[0]
        s_scr[...] = s0_ref[0]

    xp_scr[SUBLANES:SUBLANES + tm, :] = qkv_ref[...]
    cw = cw_ref[...]
    n_tiles = tm // SUBLANES
    x3 = xp_scr[...].reshape(n_tiles + 1, SUBLANES, QKV_WIDTH)
    sub = lax.broadcasted_iota(jnp.int32, (1, SUBLANES, QKV_WIDTH), 1)

    def delayed(s):
        rot = pltpu.roll(x3, s, axis=1)
        return jnp.where(sub >= s, rot[1:], rot[:-1])

    y = delayed(3) * cw[0:1, :].reshape(1, 1, QKV_WIDTH)
    y = y + delayed(2) * cw[1:2, :].reshape(1, 1, QKV_WIDTH)
    y = y + delayed(1) * cw[2:3, :].reshape(1, 1, QKV_WIDTH)
    y = y + x3[1:] * cw[3:4, :].reshape(1, 1, QKV_WIDTH)
    y_scr[...] = _silu(y).reshape(tm, QKV_WIDTH)
    tail = xp_scr[tm:tm + SUBLANES, :]
    xp_scr[0:SUBLANES, :] = tail

    @pl.when(i == last)
    def _():
        convout_ref[0] = tail

    beta, g = _gdn_gates(ba_ref[...], alog_ref[...], dtb_ref[...])
    b_scr[...] = beta
    g_scr[...] = g

    r = lax.broadcasted_iota(jnp.int32, (c_len, c_len), 0)
    c = lax.broadcasted_iota(jnp.int32, (c_len, c_len), 1)
    ge = r >= c
    gt = r > c
    tri = jnp.where(ge, 1.0, 0.0).astype(BF16)
    eye = jnp.where(r == c, 1.0, 0.0).astype(F32)
    blk16 = (r // 16) == (c // 16)
    pair_masks = [((r // (2 * s)) == (c // (2 * s))) & ((r // s) != (c // s)) for s in (16, 32, 64)]
    gn = gn_ref[...]
    scale = GDN_HEAD_DIM ** -0.5
    hd = GDN_HEAD_DIM

    def phase_a(p, carry):
        chains = []
        for ck in range(GDN_CHUNKS_PER_ITER):
            ci = p * GDN_CHUNKS_PER_ITER + ck
            rows = pl.ds(pl.multiple_of(ci * c_len, c_len), c_len)
            gch = g_scr[rows, :]
            bch = b_scr[rows, :]
            g1 = gch.astype(BF16)
            r1 = gch - g1.astype(F32)
            g2 = r1.astype(BF16)
            g3 = (r1 - g2.astype(F32)).astype(BF16)
            gcum = (jnp.dot(tri, g1, preferred_element_type=F32) + jnp.dot(tri, g2, preferred_element_type=F32)
                    + jnp.dot(tri, g3, preferred_element_type=F32))
            gcum_t = gcum.T
            glast = gcum[c_len - 1:c_len, :]
            egl_scr[ci] = jnp.broadcast_to(jnp.exp(glast), (SUBLANES, LANES))
            for h in range(GDN_HEADS):
                lo = h * hd
                q = _l2n(y_scr[rows, lo:lo + hd]) * scale
                k = _l2n(y_scr[rows, GDN_WIDTH + lo:GDN_WIDTH + lo + hd])
                v = y_scr[rows, 2 * GDN_WIDTH + lo:2 * GDN_WIDTH + lo + hd]
                gl = GDN_HEADS + h
                gcb = jnp.broadcast_to(gcum[:, gl:gl + 1], (c_len, c_len))
                bcol = jnp.broadcast_to(bch[:, h:h + 1], (c_len, c_len))
                egc = jnp.exp(gcb)
                kdf = jnp.exp(glast[:, gl:gl + 1] - gcb)
                decay = jnp.where(ge, jnp.exp(gcb - gcum_t[gl:gl + 1, :]), 0.0)
                kb = k.astype(BF16)
                chains.append(dict(
                    ci=ci, h=h, decay=decay, bcol=bcol, kb=kb,
                    kq=jnp.concatenate([kb, q.astype(BF16)], axis=0),
                    rhs=jnp.concatenate([((bcol * egc) * k).astype(BF16), (bcol * v).astype(BF16)], axis=1),
                    qg=(q * egc).astype(BF16),
                    kdt=(k * kdf).T.astype(BF16)))
        for ch in chains:
            ch["kkqk"] = lax.dot_general(ch["kq"], ch["kb"], (((1,), (1,)), ((), ())), preferred_element_type=F32)
        for ch in chains:
            ch["lmat"] = jnp.where(gt, ch["bcol"] * ch["kkqk"][:c_len] * ch["decay"], 0.0)
            ch["n1"] = jnp.where(blk16, -ch["lmat"], 0.0)
            ch["t"] = eye + ch["n1"]
        for ch in chains:
            ch["n2"] = _mm(ch["n1"], ch["n1"])
        for ch in chains:
            ch["n4"] = _mm(ch["n2"], ch["n2"])
            ch["t"] = ch["t"] + _mm(ch["t"], ch["n2"])
        for ch in chains:
            ch["n8"] = _mm(ch["n4"], ch["n4"])
            ch["t"] = ch["t"] + _mm(ch["t"], ch["n4"])
        for ch in chains:
            ch["t"] = ch["t"] + _mm(ch["t"], ch["n8"])
        for pm in pair_masks:
            for ch in chains:
                ch["x"] = _mm(ch["t"], jnp.where(pm, ch["lmat"], 0.0))
            for ch in chains:
                ch["t"] = ch["t"] - _mm(ch["x"], ch["t"])
        for ch in chains:
            wu = jnp.dot(ch["t"].astype(BF16), ch["rhs"], preferred_element_type=F32)
            ci, h = ch["ci"], ch["h"]
            wq_scr[ci, h] = jnp.concatenate([wu[:, :hd].astype(BF16), ch["qg"]], axis=0)
            u0_scr[ci, h] = wu[:, hd:]
            ak_scr[ci, h] = jnp.concatenate([(ch["kkqk"][c_len:] * ch["decay"]).astype(BF16), ch["kdt"]], axis=0)
        return carry

    lax.fori_loop(0, tm // c_len // GDN_CHUNKS_PER_ITER, phase_a, 0)

    def phase_b(ci, carry):
        rows = pl.ds(pl.multiple_of(ci * c_len, c_len), c_len)
        egl = egl_scr[ci]
        heads = range(GDN_HEADS)
        ss = [s_scr[h] for h in heads]
        wsqs = [jnp.dot(wq_scr[ci, h], ss[h].astype(BF16), preferred_element_type=F32) for h in heads]
        us = [(u0_scr[ci, h] - wsqs[h][:c_len]).astype(BF16) for h in heads]
        auku = [jnp.dot(ak_scr[ci, h], us[h], preferred_element_type=F32) for h in heads]
        for h in heads:
            lo = h * hd
            gl = GDN_HEADS + h
            o = wsqs[h][c_len:] + auku[h][:c_len]
            s_scr[h] = egl[0:1, gl:gl + 1] * ss[h] + auku[h][c_len:]
            o_ref[rows, lo:lo + hd] = _gated_norm(o, gn, z_ref[rows, lo:lo + hd])
        return carry

    lax.fori_loop(0, tm // c_len, phase_b, 0)

    @pl.when(i == last)
    def _():
        sout_ref[0] = s_scr[...]


def _gdn_prompt(qkv, ba, z, cw, alog, dtb, gn, conv0, s0, tm):
    n_seq = s0.shape[0]
    t = qkv.shape[0]
    nt = t // n_seq // tm
    row = lambda n: pl.BlockSpec((tm, n), lambda b, i: (b * nt + i, 0))
    full = lambda a: pl.BlockSpec(a.shape, lambda b, i: (0,) * a.ndim)
    hd = GDN_HEAD_DIM
    return pl.pallas_call(
        functools.partial(_gdn_prompt_kernel, tm=tm),
        grid=(n_seq, nt),
        in_specs=[row(QKV_WIDTH), row(LANES), row(GDN_WIDTH), full(cw), full(alog), full(dtb), full(gn),
                  pl.BlockSpec((1, SUBLANES, QKV_WIDTH), lambda b, i: (b, 0, 0)),
                  pl.BlockSpec((1, GDN_HEADS, hd, hd), lambda b, i: (b, 0, 0, 0))],
        out_specs=[row(GDN_WIDTH),
                   pl.BlockSpec((1, GDN_HEADS, hd, hd), lambda b, i: (b, 0, 0, 0)),
                   pl.BlockSpec((1, SUBLANES, QKV_WIDTH), lambda b, i: (b, 0, 0))],
        out_shape=[jax.ShapeDtypeStruct((t, GDN_WIDTH), F32),
                   jax.ShapeDtypeStruct((n_seq, GDN_HEADS, hd, hd), F32),
                   jax.ShapeDtypeStruct((n_seq, SUBLANES, QKV_WIDTH), F32)],
        scratch_shapes=[pltpu.VMEM((tm + SUBLANES, QKV_WIDTH), F32), pltpu.VMEM((tm, QKV_WIDTH), F32),
                        pltpu.VMEM((tm, LANES), F32), pltpu.VMEM((tm, LANES), F32),
                        pltpu.VMEM((GDN_HEADS, hd, hd), F32),
                        pltpu.VMEM((tm // GDN_CHUNK, GDN_HEADS, 2 * GDN_CHUNK, hd), BF16),
                        pltpu.VMEM((tm // GDN_CHUNK, GDN_HEADS, 2 * GDN_CHUNK, hd), BF16),
                        pltpu.VMEM((tm // GDN_CHUNK, GDN_HEADS, GDN_CHUNK, hd), F32),
                        pltpu.VMEM((tm // GDN_CHUNK, SUBLANES, LANES), F32)],
        compiler_params=_cparams(("arbitrary", "arbitrary")),
        name="gdn_prompt",
    )(qkv, ba, z, cw, alog, dtb, gn, conv0, s0)


def _gdn_step_kernel(qkv_ref, ba_ref, z_ref, cw_ref, alog_ref, dtb_ref, gn_ref, conv_ref, s_ref,
                     o_ref, sout_ref, convout_ref):
    nb = SUBLANES
    hd = GDN_HEAD_DIM
    x = qkv_ref[...]
    cb = conv_ref[...]
    cw = cw_ref[...]
    b0 = cb[:, 0:QKV_WIDTH]
    b1 = cb[:, QKV_WIDTH:2 * QKV_WIDTH]
    b2 = cb[:, 2 * QKV_WIDTH:3 * QKV_WIDTH]
    y = b0 * cw[0:1, :]
    y = y + b1 * cw[1:2, :]
    y = y + b2 * cw[2:3, :]
    y = y + x * cw[3:4, :]
    y = _silu(y)
    convout_ref[...] = jnp.concatenate([b1, b2, x], axis=1)

    beta, g = _gdn_gates(ba_ref[...], alog_ref[...], dtb_ref[...])
    a = jnp.exp(g)
    gn = gn_ref[...]
    zpad = jnp.zeros((hd - nb, hd), F32)
    for h in range(GDN_HEADS):
        lo = h * hd
        q = _l2n(y[:, lo:lo + hd]) * (hd ** -0.5)
        k = _l2n(y[:, GDN_WIDTH + lo:GDN_WIDTH + lo + hd])
        v = y[:, 2 * GDN_WIDTH + lo:2 * GDN_WIDTH + lo + hd]
        kt = jnp.concatenate([k, zpad], axis=0).T
        qt = jnp.concatenate([q, zpad], axis=0).T
        kq = jnp.sum(k * q, axis=-1, keepdims=True)
        bh = beta[:, h:h + 1]
        ah = a[:, GDN_HEADS + h:GDN_HEADS + h + 1]
        o_rows = []
        for n in range(nb):
            s = s_ref[n, h]
            kc = kt[:, n:n + 1]
            qc = qt[:, n:n + 1]
            rk = jnp.sum(s * kc, axis=0, keepdims=True)
            rq = jnp.sum(s * qc, axis=0, keepdims=True)
            an = ah[n:n + 1, :]
            un = bh[n:n + 1, :] * (v[n:n + 1, :] - an * rk)
            sout_ref[n, h] = an * s + kc * un
            o_rows.append(an * rq + kq[n:n + 1, :] * un)
        o = jnp.concatenate(o_rows, axis=0)
        o_ref[:, lo:lo + hd] = _gated_norm(o, gn, z_ref[:, lo:lo + hd])


def _gdn_step(qkv, ba, z, cw, alog, dtb, gn, conv, s):
    n_seq = qkv.shape[0]
    nb = SUBLANES
    hd = GDN_HEAD_DIM
    row = lambda n: pl.BlockSpec((nb, n), lambda i: (i, 0))
    full = lambda a: pl.BlockSpec(a.shape, lambda i: (0,) * a.ndim)
    sspec = pl.BlockSpec((nb, GDN_HEADS, hd, hd), lambda i: (i, 0, 0, 0))
    return pl.pallas_call(
        _gdn_step_kernel,
        grid=(n_seq // nb,),
        in_specs=[row(QKV_WIDTH), row(LANES), row(GDN_WIDTH), full(cw), full(alog), full(dtb), full(gn),
                  row(3 * QKV_WIDTH), sspec],
        out_specs=[row(GDN_WIDTH), sspec, row(3 * QKV_WIDTH)],
        out_shape=[jax.ShapeDtypeStruct((n_seq, GDN_WIDTH), F32),
                   jax.ShapeDtypeStruct((n_seq, GDN_HEADS, hd, hd), F32),
                   jax.ShapeDtypeStruct((n_seq, 3 * QKV_WIDTH), F32)],
        compiler_params=_cparams(("arbitrary",)),
        name="gdn_step",
    )(qkv, ba, z, cw, alog, dtb, gn, conv, s)


def _s5_param_kernel(lre_ref, lim_ref, ldt_ref, bre_ref, bim_ref, are_ref, aim_ref, bbre_ref, bbim_ref):
    lre = lre_ref[...]
    lim = lim_ref[...]
    dt = jnp.exp(ldt_ref[...])
    mag = jnp.exp(lre * dt)
    are = mag * jnp.cos(lim * dt)
    aim = mag * jnp.sin(lim * dt)
    are_ref[...] = are
    aim_ref[...] = aim
    nre = are - 1.0
    den = lre * lre + lim * lim
    cre = (nre * lre + aim * lim) / den
    cim = (aim * lre - nre * lim) / den
    cre = cre[:, None, :]
    cim = cim[:, None, :]
    bre = bre_ref[...]
    bim = bim_ref[...]
    bbre_ref[...] = cre * bre - cim * bim
    bbim_ref[...] = cre * bim + cim * bre


def _s5_params(lam_re, lam_im, log_dt, b_re, b_im):
    g, p = lam_re.shape
    bt_re = jnp.swapaxes(b_re, 1, 2)
    bt_im = jnp.swapaxes(b_im, 1, 2)
    cg = bt_re.shape[1]
    return pl.pallas_call(
        _s5_param_kernel,
        out_shape=[jax.ShapeDtypeStruct((g, p), F32), jax.ShapeDtypeStruct((g, p), F32),
                   jax.ShapeDtypeStruct((g, cg, p), F32), jax.ShapeDtypeStruct((g, cg, p), F32)],
        name="s5_params",
    )(lam_re, lam_im, log_dt.reshape(g, 1), bt_re, bt_im)


def _slab_blockdiag(m):
    g, cg, p = m.shape
    gl = g // S5_SLABS
    m4 = m.reshape(S5_SLABS, gl, cg, p)
    eye = jnp.eye(gl, dtype=m.dtype)
    return jnp.einsum('igcp,gh->igchp', m4, eye).reshape(S5_SLABS, gl * cg, gl * p)


def _slab_vec(v):
    g, p = v.shape
    return v.reshape(S5_SLABS, (g // S5_SLABS) * p)


def _gelu_tanh(x):
    return 0.5 * x * (1.0 + jnp.tanh(math.sqrt(2.0 / math.pi) * (x + 0.044715 * (x * x * x))))


def _s5_kernel(u_ref, x0_ref, a_ref, bmat_ref, cre_ref, cim_ref, d_ref, gluw_ref, glub_ref, ng_ref,
               o_ref, xout_ref, utb_scr, xs_scr, x_scr, y_scr, ab_scr, *, nb, tt, interleave):
    rows = nb * tt
    ss = S5_SLAB_STATE
    i = pl.program_id(0)

    @pl.when(i == 0)
    def _():
        x_scr[...] = x0_ref[...]
        ab_scr[...] = jnp.broadcast_to(a_ref[...], ab_scr.shape)

    if interleave:
        for b in range(nb):
            for s in range(S5_SLABS):
                utb_scr[pl.ds(s * rows + b, tt, stride=nb), :] = u_ref[b, :, s * LANES:(s + 1) * LANES]
    else:
        for s in range(S5_SLABS):
            utb_scr[s * rows:(s + 1) * rows, :] = u_ref[:, s * LANES:(s + 1) * LANES]

    for s in range(S5_SLABS):
        xs_scr[:, s * 2 * ss:(s + 1) * 2 * ss] = jnp.dot(utb_scr[s * rows:(s + 1) * rows, :].astype(BF16),
                                                       bmat_ref[s], preferred_element_type=F32)

    def step(t, carry):
        r0 = pl.multiple_of(t * nb, nb)
        for s in range(S5_SLABS):
            lo = s * 2 * ss
            ar = ab_scr[:, lo:lo + ss]
            ai = ab_scr[:, lo + ss:lo + 2 * ss]
            xr = x_scr[:, lo:lo + ss]
            xi = x_scr[:, lo + ss:lo + 2 * ss]
            nr = (ar * xr - ai * xi) + xs_scr[pl.ds(r0, nb), lo:lo + ss]
            ni = (ar * xi + ai * xr) + xs_scr[pl.ds(r0, nb), lo + ss:lo + 2 * ss]
            x_scr[:, lo:lo + ss] = nr
            x_scr[:, lo + ss:lo + 2 * ss] = ni
            xs_scr[pl.ds(r0, nb), lo:lo + ss] = nr
            xs_scr[pl.ds(r0, nb), lo + ss:lo + 2 * ss] = ni
        return carry

    lax.fori_loop(0, tt, step, 0, unroll=min(tt, 4))

    @pl.when(i == pl.num_programs(0) - 1)
    def _():
        xout_ref[...] = x_scr[...]

    ys = []
    for s in range(S5_SLABS):
        lo = s * 2 * ss
        yr = jnp.dot(xs_scr[:, lo:lo + ss].astype(BF16), cre_ref[s], preferred_element_type=F32)
        yi = jnp.dot(xs_scr[:, lo + ss:lo + 2 * ss].astype(BF16), cim_ref[s], preferred_element_type=F32)
        ys.append((yr - yi) + d_ref[0:1, s * LANES:(s + 1) * LANES] * utb_scr[s * rows:(s + 1) * rows, :])
    y = _gelu_tanh(jnp.concatenate(ys, axis=1))
    y = y * _sigmoid(_mm(y, gluw_ref[...]) + glub_ref[...])
    y = y * lax.rsqrt(jnp.mean(y * y, axis=-1, keepdims=True) + NORM_EPS) * ng_ref[...]
    if interleave:
        for s in range(S5_SLABS):
            y_scr[s * rows:(s + 1) * rows, :] = y[:, s * LANES:(s + 1) * LANES]
        for b in range(nb):
            for s in range(S5_SLABS):
                o_ref[b, :, s * LANES:(s + 1) * LANES] = y_scr[pl.ds(s * rows + b, tt, stride=nb), :]
    else:
        o_ref[...] = y


def _s5(u, x0, a, bmat, cre, cim, d, gluw, glub, ng, nb, tt, interleave):
    rows = nb * tt
    nstate = x0.shape[1]
    full = lambda arr: pl.BlockSpec(arr.shape, lambda i: (0,) * arr.ndim)
    if interleave:
        steps = u.shape[1] // tt
        uspec = pl.BlockSpec((nb, tt, S5_WIDTH), lambda i: (0, i, 0))
        oshape = jax.ShapeDtypeStruct(u.shape, F32)
    else:
        steps = 1
        uspec = pl.BlockSpec((nb, S5_WIDTH), lambda i: (0, 0))
        oshape = jax.ShapeDtypeStruct(u.shape, F32)
    return pl.pallas_call(
        functools.partial(_s5_kernel, nb=nb, tt=tt, interleave=interleave),
        grid=(steps,),
        in_specs=[uspec, full(x0), full(a), full(bmat), full(cre), full(cim), full(d), full(gluw), full(glub),
                  full(ng)],
        out_specs=[uspec, full(x0)],
        out_shape=[oshape, jax.ShapeDtypeStruct(x0.shape, F32)],
        scratch_shapes=[pltpu.VMEM((S5_SLABS * rows, LANES), F32), pltpu.VMEM((rows, nstate), F32),
                        pltpu.VMEM((nb, nstate), F32), pltpu.VMEM((S5_SLABS * rows, LANES), F32),
                        pltpu.VMEM((nb, nstate), F32)],
        compiler_params=_cparams(("arbitrary",)),
        name="s5_scan",
    )(u, x0, a, bmat, cre, cim, d, gluw, glub, ng)


def _router(logits_t, bias_col):
    scores = _sigmoid(logits_t)
    sel = scores + bias_col
    s = [sel[e:e + 1, :] for e in range(N_EXPERTS)]
    n_groups = N_EXPERTS // EXPERTS_PER_GROUP
    gs = []
    for gi in range(n_groups):
        m = s[gi * EXPERTS_PER_GROUP: (gi + 1) * EXPERTS_PER_GROUP]
        best = None
        for p in range(EXPERTS_PER_GROUP):
            for q in range(p + 1, EXPERTS_PER_GROUP):
                ps = m[p] + m[q]
                best = ps if best is None else jnp.maximum(best, ps)
        gs.append(best)
    gmax = functools.reduce(jnp.maximum, gs)
    taken = None
    in_best = []
    for gi in range(n_groups):
        hit = gs[gi] == gmax
        if taken is None:
            cur = hit
            taken = hit
        else:
            cur = jnp.logical_and(hit, jnp.logical_not(taken))
            taken = jnp.logical_or(taken, hit)
        in_best.append(cur)
    picked = []
    for e in range(N_EXPERTS):
        gi = e // EXPERTS_PER_GROUP
        cnt = jnp.zeros_like(s[e])
        for j in range(gi * EXPERTS_PER_GROUP, (gi + 1) * EXPERTS_PER_GROUP):
            if j == e:
                continue
            beats = (s[j] >= s[e]) if j < e else (s[j] > s[e])
            cnt = cnt + jnp.where(beats, 1.0, 0.0)
        sel_e = jnp.logical_and(in_best[gi], cnt < 1.5)
        picked.append(jnp.where(sel_e, scores[e:e + 1, :], 0.0))
    denom = functools.reduce(lambda x, y: x + y, picked)
    return jnp.concatenate([p / denom for p in picked], axis=0)


def _outproj_kernel(oa_ref, ob_ref, x_ref, gt_ref, sc_ref, sh_ref, g_ref, wout_ref, wrt_ref, br_ref,
                    xo_ref, hn_ref, comb_ref, *, tm):
    o = jnp.concatenate([oa_ref[...].astype(BF16), ob_ref[...].astype(BF16)], axis=1)
    mix = jnp.dot(o, wout_ref[...], preferred_element_type=F32)
    x = x_ref[...] + gt_ref[0] * mix
    xo_ref[...] = x
    ms = jnp.mean(x * x, axis=-1, keepdims=True)
    hn = x * lax.rsqrt(ms + NORM_EPS) * g_ref[...]
    hn = hn * (1.0 + sc_ref[0]) + sh_ref[0]
    hb = hn.astype(BF16)
    hn_ref[...] = hb
    logits_t = lax.dot_general(wrt_ref[...], hb, (((1,), (1,)), ((), ())), preferred_element_type=F32)
    comb_t = _router(logits_t, br_ref[...])
    pad = jnp.zeros((LANES - N_EXPERTS, tm), F32)
    comb_ref[...] = jnp.concatenate([comb_t, pad], axis=0).T


def _outproj(oa, ob, x, gt, sc, sh, g, wout, wrt, br, tm, rows_per_seq):
    t = x.shape[0]
    full = lambda a: pl.BlockSpec(a.shape, lambda i: (0,) * a.ndim)
    row = lambda n: pl.BlockSpec((tm, n), lambda i: (i, 0))
    ms = lambda m: _mod_spec(m, tm, rows_per_seq)
    return pl.pallas_call(
        functools.partial(_outproj_kernel, tm=tm),
        grid=(t // tm,),
        in_specs=[row(GDN_WIDTH), row(S5_WIDTH), row(D_MODEL), ms(gt), ms(sc), ms(sh), full(g), full(wout),
                  full(wrt), full(br)],
        out_specs=[row(D_MODEL), row(D_MODEL), row(LANES)],
        out_shape=[jax.ShapeDtypeStruct((t, D_MODEL), F32), jax.ShapeDtypeStruct((t, D_MODEL), BF16),
                   jax.ShapeDtypeStruct((t, LANES), F32)],
        compiler_params=_cparams(("arbitrary",)),
        name="outproj_router",
    )(oa, ob, x, gt, sc, sh, g, wout, wrt, br)


def _moe_kernel(hn_ref, comb_ref, x_ref, gt_ref, wgu_ref, wd_ref, fg_ref, o_ref, acc_scr, *, final_norm):
    e = pl.program_id(1)

    @pl.when(e == 0)
    def _():
        acc_scr[...] = jnp.zeros_like(acc_scr)

    h = jnp.dot(hn_ref[...], wgu_ref[0], preferred_element_type=F32)
    hg = h[:, :D_EXPERT]
    hu = h[:, D_EXPERT:]
    lane = lax.broadcasted_iota(jnp.int32, (1, LANES), 1)
    ce = jnp.sum(jnp.where(lane == e, comb_ref[...], 0.0), axis=-1, keepdims=True)
    act = _silu(hg) * hu * ce
    acc_scr[...] += jnp.dot(act.astype(BF16), wd_ref[0], preferred_element_type=F32)

    @pl.when(e == pl.num_programs(1) - 1)
    def _():
        x = x_ref[...] + gt_ref[0] * acc_scr[...]
        if final_norm:
            x = x * lax.rsqrt(jnp.mean(x * x, axis=-1, keepdims=True) + NORM_EPS) * fg_ref[...]
        o_ref[...] = x


def _moe(hn, comb, x, gt, wgu, wd, fg, tm, rows_per_seq, final_norm):
    t = x.shape[0]
    row = lambda n: pl.BlockSpec((tm, n), lambda i, e: (i, 0))
    if gt.shape[1] == 1:
        tiles_per_seq = rows_per_seq // tm
        gspec = pl.BlockSpec((1, 1, D_MODEL), lambda i, e: (i // tiles_per_seq, 0, 0))
    else:
        gspec = pl.BlockSpec((1, tm, D_MODEL), lambda i, e: (0, i, 0))
    return pl.pallas_call(
        functools.partial(_moe_kernel, final_norm=final_norm),
        grid=(t // tm, N_EXPERTS),
        in_specs=[row(D_MODEL), row(LANES), row(D_MODEL), gspec,
                  pl.BlockSpec((1, D_MODEL, 2 * D_EXPERT), lambda i, e: (e, 0, 0)),
                  pl.BlockSpec((1, D_EXPERT, D_MODEL), lambda i, e: (e, 0, 0)),
                  pl.BlockSpec(fg.shape, lambda i, e: (0, 0))],
        out_specs=row(D_MODEL),
        out_shape=jax.ShapeDtypeStruct((t, D_MODEL), F32),
        scratch_shapes=[pltpu.VMEM((tm, D_MODEL), F32)],
        compiler_params=_cparams(("arbitrary", "arbitrary")),
        name="moe",
    )(hn, comb, x, gt, wgu, wd, fg)


def _pad_lanes(v, offset):
    return jnp.zeros((1, LANES), F32).at[0, offset:offset + v.shape[0]].set(v)


def _state_to_slab(re, im):
    b = re.shape[0]
    r = re.reshape(b, S5_SLABS, S5_SLAB_STATE)
    i = im.reshape(b, S5_SLABS, S5_SLAB_STATE)
    return jnp.stack([r, i], axis=2).reshape(b, S5_SLABS * 2 * S5_SLAB_STATE)


def _slab_to_state(x):
    b = x.shape[0]
    x4 = x.reshape(b, S5_SLABS, 2, S5_SLAB_STATE)
    re = x4[:, :, 0].reshape(b, S5_GROUPS, S5_STATE)
    im = x4[:, :, 1].reshape(b, S5_GROUPS, S5_STATE)
    return re, im


def kernel(x_prompt, x_sample, c_prompt, c_sample, state_conv, state_gdn, state_s5_re, state_s5_im, norm1_g, norm2_g, w_ada, b_ada, w_in, conv_w, a_log, dt_bias, gdn_norm_g, s5_lambda_re, s5_lambda_im, s5_log_dt, s5_b_re, s5_b_im, s5_c_re, s5_c_im, s5_d, s5_glu_w, s5_glu_b, s5_norm_g, w_out, w_router, b_router, w_gate, w_up, w_down, final_g):
    bp, seq, _ = x_prompt.shape
    bs = x_sample.shape[0]
    tp = bp * seq

    mod = _ada(jnp.concatenate([c_prompt, c_sample], axis=0), w_ada, b_ada)

    def mod_slices(l):
        mp = [mod[l, :bp, j * D_MODEL:(j + 1) * D_MODEL].reshape(bp, 1, D_MODEL) for j in range(6)]
        msm = [mod[l, bp:, j * D_MODEL:(j + 1) * D_MODEL].reshape(1, bs, D_MODEL) for j in range(6)]
        return mp, msm

    wrt = jnp.transpose(w_router).astype(BF16)
    br = b_router.reshape(N_EXPERTS, 1)
    fg = final_g.reshape(1, D_MODEL)

    xp = x_prompt.reshape(tp, D_MODEL)
    xs = x_sample.reshape(bs, D_MODEL)
    tm_p = 512
    tm_moe = 1024 if seq % 1024 == 0 else tm_p
    tt = 64 if seq % 64 == 0 else seq

    outs_p = {k: [] for k in ("conv", "gdn", "re", "im")}
    outs_s = {k: [] for k in ("conv", "gdn", "re", "im")}
    zero_conv = jnp.zeros((bp, SUBLANES, QKV_WIDTH), F32)
    zero_gdn = jnp.zeros((bp, GDN_HEADS, GDN_HEAD_DIM, GDN_HEAD_DIM), F32)
    zero_s5 = jnp.zeros((bp, S5_SLABS * 2 * S5_SLAB_STATE), F32)

    for l in range(DEPTH):
        (sh1p, sc1p, gt1p, sh2p, sc2p, gt2p), (sh1s, sc1s, gt1s, sh2s, sc2s, gt2s) = mod_slices(l)
        wl = w_in[l]
        wqkv = wl[:, :QKV_WIDTH].astype(BF16)
        wz = wl[:, QKV_WIDTH:QKV_WIDTH + GDN_WIDTH].astype(BF16)
        nba = 2 * GDN_HEADS
        wba = jnp.zeros((D_MODEL, LANES), F32).at[:, :nba].set(
            wl[:, QKV_WIDTH + GDN_WIDTH:QKV_WIDTH + GDN_WIDTH + nba]).astype(BF16)
        wu = wl[:, QKV_WIDTH + GDN_WIDTH + nba:].astype(BF16)
        g1 = norm1_g[l].reshape(1, D_MODEL)
        g2 = norm2_g[l].reshape(1, D_MODEL)
        cw = conv_w[l]
        alog = _pad_lanes(a_log[l], GDN_HEADS)
        dtb = _pad_lanes(dt_bias[l], GDN_HEADS)
        gn = gdn_norm_g[l].reshape(1, GDN_HEAD_DIM)
        a_re, a_im, bb_re, bb_im = _s5_params(s5_lambda_re[l], s5_lambda_im[l], s5_log_dt[l], s5_b_re[l], s5_b_im[l])
        a_vec = jnp.concatenate([_slab_vec(a_re), _slab_vec(a_im)], axis=1).reshape(1, -1)
        bmat = jnp.concatenate([_slab_blockdiag(bb_re), _slab_blockdiag(bb_im)], axis=2).astype(BF16)
        cre = jnp.swapaxes(_slab_blockdiag(s5_c_re[l]), 1, 2).astype(BF16)
        cim = jnp.swapaxes(_slab_blockdiag(s5_c_im[l]), 1, 2).astype(BF16)
        dvec = s5_d[l].reshape(1, S5_WIDTH)
        gluw = s5_glu_w[l].astype(BF16)
        glub = s5_glu_b[l].reshape(1, S5_WIDTH)
        ng = s5_norm_g[l].reshape(1, S5_WIDTH)
        wout = w_out[l].astype(BF16)
        wgu = jnp.concatenate([w_gate[l], w_up[l]], axis=-1).astype(BF16)
        wd = w_down[l].astype(BF16)
        last = l == DEPTH - 1

        qkv, z, ba, u = _inproj(xp, sc1p, sh1p, g1, wqkv, wz, wba, wu, tm_p, seq)
        oa, sg, cv = _gdn_prompt(qkv, ba, z, cw, alog, dtb, gn, zero_conv, zero_gdn, tm_p)
        ob, xst = _s5(u.reshape(bp, seq, S5_WIDTH), zero_s5, a_vec, bmat, cre, cim, dvec, gluw, glub, ng,
                      bp, tt, True)
        xp, hn, comb = _outproj(oa, ob.reshape(tp, S5_WIDTH), xp, gt1p, sc2p, sh2p, g2, wout, wrt, br, tm_p, seq)
        xp = _moe(hn, comb, xp, gt2p, wgu, wd, fg, tm_moe, seq, last)
        outs_p["conv"].append(cv[:, SUBLANES - (CONV_WIDTH - 1):, :])
        outs_p["gdn"].append(sg)
        re, im = _slab_to_state(xst)
        outs_p["re"].append(re)
        outs_p["im"].append(im)

        qkv, z, ba, u = _inproj(xs, sc1s, sh1s, g1, wqkv, wz, wba, wu, bs, 1)
        oa, sg, cv = _gdn_step(qkv, ba, z, cw, alog, dtb, gn,
                               state_conv[l].reshape(bs, (CONV_WIDTH - 1) * QKV_WIDTH), state_gdn[l])
        ob, xst = _s5(u, _state_to_slab(state_s5_re[l], state_s5_im[l]), a_vec, bmat, cre, cim, dvec, gluw,
                      glub, ng, bs, 1, False)
        xs, hn, comb = _outproj(oa, ob, xs, gt1s, sc2s, sh2s, g2, wout, wrt, br, bs, 1)
        xs = _moe(hn, comb, xs, gt2s, wgu, wd, fg, bs, 1, last)
        outs_s["conv"].append(cv.reshape(bs, CONV_WIDTH - 1, QKV_WIDTH))
        outs_s["gdn"].append(sg)
        re, im = _slab_to_state(xst)
        outs_s["re"].append(re)
        outs_s["im"].append(im)

    st = lambda d, k: jnp.stack(d[k])
    return (xp.reshape(bp, seq, D_MODEL), xs.reshape(bs, 1, D_MODEL),
            st(outs_p, "conv"), st(outs_p, "gdn"), st(outs_p, "re"), st(outs_p, "im"),
            st(outs_s, "conv"), st(outs_s, "gdn"), st(outs_s, "re"), st(outs_s, "im"))
```

```python
import functools
import math

import jax
import jax.numpy as jnp
from jax import lax
from jax.experimental import pallas as pl
from jax.experimental.pallas import tpu as pltpu

F32 = jnp.float32
BF16 = jnp.bfloat16

D_MODEL = 1024
DEPTH = 2
GDN_HEAD_DIM = 128
GDN_WIDTH = 512
GDN_HEADS = 4
CONV_WIDTH = 4
S5_CH_PER_GROUP = 16
S5_WIDTH = 512
S5_GROUPS = 32
S5_STATE = 64
QKV_WIDTH = 3 * GDN_WIDTH
N_EXPERTS = 16
EXPERTS_PER_GROUP = 4
D_EXPERT = 256
NORM_EPS = 1e-6

LANES = 128
SUBLANES = 8
GDN_CHUNK = 128
GDN_CHUNKS_PER_ITER = 2
S5_SLABS = 4
S5_SLAB_STATE = 512
MOE_TILE = 256
PAIR_VALUE = (0.0, 1.0, 2.0, 4.0)
PAIR_CODES = ((0, 1), (0, 2), (1, 2), (0, 3), (1, 3), (2, 3))
N_CLASSES = (N_EXPERTS // EXPERTS_PER_GROUP) * len(PAIR_CODES)
CLASS_ROWS = 32
HX_WIDTH = D_MODEL + LANES
VMEM_LIMIT = 56 * 1024 * 1024


def _cparams(sem):
    return pltpu.CompilerParams(dimension_semantics=sem, vmem_limit_bytes=VMEM_LIMIT)


def _sigmoid(x):
    return 1.0 / (1.0 + jnp.exp(-x))


def _silu(x):
    return x * _sigmoid(x)


def _softplus(x):
    return jnp.maximum(x, 0.0) + jnp.log1p(jnp.exp(-jnp.abs(x)))


def _mm(a, b):
    return jnp.dot(a.astype(BF16), b.astype(BF16), preferred_element_type=F32)


def _ada_kernel(c_ref, w_ref, b_ref, o_ref):
    c = c_ref[...]
    o_ref[0] = _mm(_silu(c), w_ref[0]) + b_ref[0]


def _ada(c_all, w_ada, b_ada):
    rows = c_all.shape[0]
    n_out = w_ada.shape[-1]
    tn = 1536
    return pl.pallas_call(
        _ada_kernel,
        grid=(DEPTH, n_out // tn),
        in_specs=[
            pl.BlockSpec((rows, D_MODEL), lambda l, j: (0, 0)),
            pl.BlockSpec((1, D_MODEL, tn), lambda l, j: (l, 0, j)),
            pl.BlockSpec((1, 1, tn), lambda l, j: (l, 0, j)),
        ],
        out_specs=pl.BlockSpec((1, rows, tn), lambda l, j: (l, 0, j)),
        out_shape=jax.ShapeDtypeStruct((DEPTH, rows, n_out), F32),
        compiler_params=_cparams(("arbitrary", "arbitrary")),
        name="ada_mod",
    )(c_all, w_ada, b_ada.reshape(DEPTH, 1, n_out))


def _inproj_kernel(x_ref, sc_ref, sh_ref, g_ref, wqkv_ref, wz_ref, wba_ref, wu_ref,
                   qkv_ref, z_ref, ba_ref, u_ref):
    x = x_ref[...]
    ms = jnp.mean(x * x, axis=-1, keepdims=True)
    hn = x * lax.rsqrt(ms + NORM_EPS) * g_ref[...]
    hn = hn * (1.0 + sc_ref[0]) + sh_ref[0]
    hb = hn.astype(BF16)
    qkv_ref[...] = jnp.dot(hb, wqkv_ref[...], preferred_element_type=F32)
    z_ref[...] = jnp.dot(hb, wz_ref[...], preferred_element_type=F32)
    ba_ref[...] = jnp.dot(hb, wba_ref[...], preferred_element_type=F32)
    u_ref[...] = jnp.dot(hb, wu_ref[...], preferred_element_type=F32)


def _mod_spec(mod, tm, rows_per_seq):
    if mod.shape[1] == 1:
        tiles_per_seq = rows_per_seq // tm
        return pl.BlockSpec((1, 1, D_MODEL), lambda i: (i // tiles_per_seq, 0, 0))
    return pl.BlockSpec((1, tm, D_MODEL), lambda i: (0, i, 0))


def _inproj(x, sc, sh, g, wqkv, wz, wba, wu, tm, rows_per_seq):
    t = x.shape[0]
    full = lambda a: pl.BlockSpec(a.shape, lambda i: (0,) * a.ndim)
    row = lambda n: pl.BlockSpec((tm, n), lambda i: (i, 0))
    return pl.pallas_call(
        _inproj_kernel,
        grid=(t // tm,),
        in_specs=[row(D_MODEL), _mod_spec(sc, tm, rows_per_seq), _mod_spec(sh, tm, rows_per_seq),
                  full(g), full(wqkv), full(wz), full(wba), full(wu)],
        out_specs=[row(QKV_WIDTH), row(GDN_WIDTH), row(LANES), row(S5_WIDTH)],
        out_shape=[jax.ShapeDtypeStruct((t, QKV_WIDTH), F32), jax.ShapeDtypeStruct((t, GDN_WIDTH), F32),
                   jax.ShapeDtypeStruct((t, LANES), F32), jax.ShapeDtypeStruct((t, S5_WIDTH), F32)],
        compiler_params=_cparams(("arbitrary",)),
        name="inproj",
    )(x, sc, sh, g, wqkv, wz, wba, wu)


def _gdn_gates(ba, alog, dtb):
    beta = _sigmoid(ba)
    g = -jnp.exp(alog) * _softplus(ba + dtb)
    return beta, g


def _l2n(x):
    return x * lax.rsqrt(jnp.sum(x * x, axis=-1, keepdims=True) + NORM_EPS)


def _gated_norm(o, gn, z):
    on = o * lax.rsqrt(jnp.mean(o * o, axis=-1, keepdims=True) + NORM_EPS) * gn
    return on * _silu(z)


def _gdn_prompt_kernel(qkv_ref, ba_ref, z_ref, cw_ref, alog_ref, dtb_ref, gn_ref, conv0_ref, s0_ref,
                       o_ref, sout_ref, convout_ref, xp_scr, y_scr, g_scr, b_scr, s_scr, wq_scr, ak_scr, u0_scr,
                       egl_scr, *, tm):
    c_len = GDN_CHUNK
    i = pl.program_id(1)
    last = pl.num_programs(1) - 1

    @pl.when(i == 0)
    def _():
        xp_scr[0:SUBLANES, :] = conv0_ref[0]
        s_scr[...] = s0_ref[0]

    xp_scr[SUBLANES:SUBLANES + tm, :] = qkv_ref[...]
    cw = cw_ref[...]
    n_tiles = tm // SUBLANES
    x3 = xp_scr[...].reshape(n_tiles + 1, SUBLANES, QKV_WIDTH)
    sub = lax.broadcasted_iota(jnp.int32, (1, SUBLANES, QKV_WIDTH), 1)

    def delayed(s):
        rot = pltpu.roll(x3, s, axis=1)
        return jnp.where(sub >= s, rot[1:], rot[:-1])

    y = delayed(3) * cw[0:1, :].reshape(1, 1, QKV_WIDTH)
    y = y + delayed(2) * cw[1:2, :].reshape(1, 1, QKV_WIDTH)
    y = y + delayed(1) * cw[2:3, :].reshape(1, 1, QKV_WIDTH)
    y = y + x3[1:] * cw[3:4, :].reshape(1, 1, QKV_WIDTH)
    y_scr[...] = _silu(y).reshape(tm, QKV_WIDTH)
    tail = xp_scr[tm:tm + SUBLANES, :]
    xp_scr[0:SUBLANES, :] = tail

    @pl.when(i == last)
    def _():
        convout_ref[0] = tail

    beta, g = _gdn_gates(ba_ref[...], alog_ref[...], dtb_ref[...])
    b_scr[...] = beta
    g_scr[...] = g

    r = lax.broadcasted_iota(jnp.int32, (c_len, c_len), 0)
    c = lax.broadcasted_iota(jnp.int32, (c_len, c_len), 1)
    ge = r >= c
    gt = r > c
    tri = jnp.where(ge, 1.0, 0.0).astype(BF16)
    eye = jnp.where(r == c, 1.0, 0.0).astype(F32)
    blk16 = (r // 16) == (c // 16)
    pair_masks = [((r // (2 * s)) == (c // (2 * s))) & ((r // s) != (c // s)) for s in (16, 32, 64)]
    gn = gn_ref[...]
    scale = GDN_HEAD_DIM ** -0.5
    hd = GDN_HEAD_DIM

    def phase_a(p, carry):
        chains = []
        for ck in range(GDN_CHUNKS_PER_ITER):
            ci = p * GDN_CHUNKS_PER_ITER + ck
            rows = pl.ds(pl.multiple_of(ci * c_len, c_len), c_len)
            gch = g_scr[rows, :]
            bch = b_scr[rows, :]
            g1 = gch.astype(BF16)
            r1 = gch - g1.astype(F32)
            g2 = r1.astype(BF16)
            g3 = (r1 - g2.astype(F32)).astype(BF16)
            gcum = (jnp.dot(tri, g1, preferred_element_type=F32) + jnp.dot(tri, g2, preferred_element_type=F32)
                    + jnp.dot(tri, g3, preferred_element_type=F32))
            gcum_t = gcum.T
            glast = gcum[c_len - 1:c_len, :]
            egl_scr[ci] = jnp.broadcast_to(jnp.exp(glast), (SUBLANES, LANES))
            for h in range(GDN_HEADS):
                lo = h * hd
                q = _l2n(y_scr[rows, lo:lo + hd]) * scale
                k = _l2n(y_scr[rows, GDN_WIDTH + lo:GDN_WIDTH + lo + hd])
                v = y_scr[rows, 2 * GDN_WIDTH + lo:2 * GDN_WIDTH + lo + hd]
                gl = GDN_HEADS + h
                gcb = jnp.broadcast_to(gcum[:, gl:gl + 1], (c_len, c_len))
                bcol = jnp.broadcast_to(bch[:, h:h + 1], (c_len, c_len))
                egc = jnp.exp(gcb)
                kdf = jnp.exp(glast[:, gl:gl + 1] - gcb)
                decay = jnp.where(ge, jnp.exp(gcb - gcum_t[gl:gl + 1, :]), 0.0)
                kb = k.astype(BF16)
                chains.append(dict(
                    ci=ci, h=h, decay=decay, bcol=bcol, kb=kb,
                    kq=jnp.concatenate([kb, q.astype(BF16)], axis=0),
                    rhs=jnp.concatenate([((bcol * egc) * k).astype(BF16), (bcol * v).astype(BF16)], axis=1),
                    qg=(q * egc).astype(BF16),
                    kdt=(k * kdf).T.astype(BF16)))
        for ch in chains:
            ch["kkqk"] = lax.dot_general(ch["kq"], ch["kb"], (((1,), (1,)), ((), ())), preferred_element_type=F32)
        for ch in chains:
            ch["lmat"] = jnp.where(gt, ch["bcol"] * ch["kkqk"][:c_len] * ch["decay"], 0.0)
            ch["n1"] = jnp.where(blk16, -ch["lmat"], 0.0)
            ch["t"] = eye + ch["n1"]
        for ch in chains:
            ch["n2"] = _mm(ch["n1"], ch["n1"])
        for ch in chains:
            ch["n4"] = _mm(ch["n2"], ch["n2"])
            ch["t"] = ch["t"] + _mm(ch["t"], ch["n2"])
        for ch in chains:
            ch["n8"] = _mm(ch["n4"], ch["n4"])
            ch["t"] = ch["t"] + _mm(ch["t"], ch["n4"])
        for ch in chains:
            ch["t"] = ch["t"] + _mm(ch["t"], ch["n8"])
        for pm in pair_masks:
            for ch in chains:
                ch["x"] = _mm(ch["t"], jnp.where(pm, ch["lmat"], 0.0))
            for ch in chains:
                ch["t"] = ch["t"] - _mm(ch["x"], ch["t"])
        for ch in chains:
            wu = jnp.dot(ch["t"].astype(BF16), ch["rhs"], preferred_element_type=F32)
            ci, h = ch["ci"], ch["h"]
            wq_scr[ci, h] = jnp.concatenate([wu[:, :hd].astype(BF16), ch["qg"]], axis=0)
            u0_scr[ci, h] = wu[:, hd:]
            ak_scr[ci, h] = jnp.concatenate([(ch["kkqk"][c_len:] * ch["decay"]).astype(BF16), ch["kdt"]], axis=0)
        return carry

    lax.fori_loop(0, tm // c_len // GDN_CHUNKS_PER_ITER, phase_a, 0)

    def phase_b(ci, carry):
        rows = pl.ds(pl.multiple_of(ci * c_len, c_len), c_len)
        egl = egl_scr[ci]
        heads = range(GDN_HEADS)
        ss = [s_scr[h] for h in heads]
        wsqs = [jnp.dot(wq_scr[ci, h], ss[h].astype(BF16), preferred_element_type=F32) for h in heads]
        us = [(u0_scr[ci, h] - wsqs[h][:c_len]).astype(BF16) for h in heads]
        auku = [jnp.dot(ak_scr[ci, h], us[h], preferred_element_type=F32) for h in heads]
        for h in heads:
            lo = h * hd
            gl = GDN_HEADS + h
            o = wsqs[h][c_len:] + auku[h][:c_len]
            s_scr[h] = egl[0:1, gl:gl + 1] * ss[h] + auku[h][c_len:]
            o_ref[rows, lo:lo + hd] = _gated_norm(o, gn, z_ref[rows, lo:lo + hd])
        return carry

    lax.fori_loop(0, tm // c_len, phase_b, 0)

    @pl.when(i == last)
    def _():
        sout_ref[0] = s_scr[...]


def _gdn_prompt(qkv, ba, z, cw, alog, dtb, gn, conv0, s0, tm):
    n_seq = s0.shape[0]
    t = qkv.shape[0]
    nt = t // n_seq // tm
    row = lambda n: pl.BlockSpec((tm, n), lambda b, i: (b * nt + i, 0))
    full = lambda a: pl.BlockSpec(a.shape, lambda b, i: (0,) * a.ndim)
    hd = GDN_HEAD_DIM
    return pl.pallas_call(
        functools.partial(_gdn_prompt_kernel, tm=tm),
        grid=(n_seq, nt),
        in_specs=[row(QKV_WIDTH), row(LANES), row(GDN_WIDTH), full(cw), full(alog), full(dtb), full(gn),
                  pl.BlockSpec((1, SUBLANES, QKV_WIDTH), lambda b, i: (b, 0, 0)),
                  pl.BlockSpec((1, GDN_HEADS, hd, hd), lambda b, i: (b, 0, 0, 0))],
        out_specs=[row(GDN_WIDTH),
                   pl.BlockSpec((1, GDN_HEADS, hd, hd), lambda b, i: (b, 0, 0, 0)),
                   pl.BlockSpec((1, SUBLANES, QKV_WIDTH), lambda b, i: (b, 0, 0))],
        out_shape=[jax.ShapeDtypeStruct((t, GDN_WIDTH), F32),
                   jax.ShapeDtypeStruct((n_seq, GDN_HEADS, hd, hd), F32),
                   jax.ShapeDtypeStruct((n_seq, SUBLANES, QKV_WIDTH), F32)],
        scratch_shapes=[pltpu.VMEM((tm + SUBLANES, QKV_WIDTH), F32), pltpu.VMEM((tm, QKV_WIDTH), F32),
                        pltpu.VMEM((tm, LANES), F32), pltpu.VMEM((tm, LANES), F32),
                        pltpu.VMEM((GDN_HEADS, hd, hd), F32),
                        pltpu.VMEM((tm // GDN_CHUNK, GDN_HEADS, 2 * GDN_CHUNK, hd), BF16),
                        pltpu.VMEM((tm // GDN_CHUNK, GDN_HEADS, 2 * GDN_CHUNK, hd), BF16),
                        pltpu.VMEM((tm // GDN_CHUNK, GDN_HEADS, GDN_CHUNK, hd), F32),
                        pltpu.VMEM((tm // GDN_CHUNK, SUBLANES, LANES), F32)],
        compiler_params=_cparams(("arbitrary", "arbitrary")),
        name="gdn_prompt",
    )(qkv, ba, z, cw, alog, dtb, gn, conv0, s0)


def _gdn_step_kernel(qkv_ref, ba_ref, z_ref, cw_ref, alog_ref, dtb_ref, gn_ref, conv_ref, s_ref,
                     o_ref, sout_ref, convout_ref):
    nb = SUBLANES
    hd = GDN_HEAD_DIM
    x = qkv_ref[...]
    cb = conv_ref[...]
    cw = cw_ref[...]
    b0 = cb[:, 0:QKV_WIDTH]
    b1 = cb[:, QKV_WIDTH:2 * QKV_WIDTH]
    b2 = cb[:, 2 * QKV_WIDTH:3 * QKV_WIDTH]
    y = b0 * cw[0:1, :]
    y = y + b1 * cw[1:2, :]
    y = y + b2 * cw[2:3, :]
    y = y + x * cw[3:4, :]
    y = _silu(y)
    convout_ref[...] = jnp.concatenate([b1, b2, x], axis=1)

    beta, g = _gdn_gates(ba_ref[...], alog_ref[...], dtb_ref[...])
    a = jnp.exp(g)
    gn = gn_ref[...]
    zpad = jnp.zeros((hd - nb, hd), F32)
    for h in range(GDN_HEADS):
        lo = h * hd
        q = _l2n(y[:, lo:lo + hd]) * (hd ** -0.5)
        k = _l2n(y[:, GDN_WIDTH + lo:GDN_WIDTH + lo + hd])
        v = y[:, 2 * GDN_WIDTH + lo:2 * GDN_WIDTH + lo + hd]
        kt = jnp.concatenate([k, zpad], axis=0).T
        qt = jnp.concatenate([q, zpad], axis=0).T
        kq = jnp.sum(k * q, axis=-1, keepdims=True)
        bh = beta[:, h:h + 1]
        ah = a[:, GDN_HEADS + h:GDN_HEADS + h + 1]
        o_rows = []
        for n in range(nb):
            s = s_ref[n, h]
            kc = kt[:, n:n + 1]
            qc = qt[:, n:n + 1]
            rk = jnp.sum(s * kc, axis=0, keepdims=True)
            rq = jnp.sum(s * qc, axis=0, keepdims=True)
            an = ah[n:n + 1, :]
            un = bh[n:n + 1, :] * (v[n:n + 1, :] - an * rk)
            sout_ref[n, h] = an * s + kc * un
            o_rows.append(an * rq + kq[n:n + 1, :] * un)
        o = jnp.concatenate(o_rows, axis=0)
        o_ref[:, lo:lo + hd] = _gated_norm(o, gn, z_ref[:, lo:lo + hd])


def _gdn_step(qkv, ba, z, cw, alog, dtb, gn, conv, s):
    n_seq = qkv.shape[0]
    nb = SUBLANES
    hd = GDN_HEAD_DIM
    row = lambda n: pl.BlockSpec((nb, n), lambda i: (i, 0))
    full = lambda a: pl.BlockSpec(a.shape, lambda i: (0,) * a.ndim)
    sspec = pl.BlockSpec((nb, GDN_HEADS, hd, hd), lambda i: (i, 0, 0, 0))
    return pl.pallas_call(
        _gdn_step_kernel,
        grid=(n_seq // nb,),
        in_specs=[row(QKV_WIDTH), row(LANES), row(GDN_WIDTH), full(cw), full(alog), full(dtb), full(gn),
                  row(3 * QKV_WIDTH), sspec],
        out_specs=[row(GDN_WIDTH), sspec, row(3 * QKV_WIDTH)],
        out_shape=[jax.ShapeDtypeStruct((n_seq, GDN_WIDTH), F32),
                   jax.ShapeDtypeStruct((n_seq, GDN_HEADS, hd, hd), F32),
                   jax.ShapeDtypeStruct((n_seq, 3 * QKV_WIDTH), F32)],
        compiler_params=_cparams(("arbitrary",)),
        name="gdn_step",
    )(qkv, ba, z, cw, alog, dtb, gn, conv, s)


def _s5_param_kernel(lre_ref, lim_ref, ldt_ref, bre_ref, bim_ref, are_ref, aim_ref, bbre_ref, bbim_ref):
    lre = lre_ref[...]
    lim = lim_ref[...]
    dt = jnp.exp(ldt_ref[...])
    mag = jnp.exp(lre * dt)
    are = mag * jnp.cos(lim * dt)
    aim = mag * jnp.sin(lim * dt)
    are_ref[...] = are
    aim_ref[...] = aim
    nre = are - 1.0
    den = lre * lre + lim * lim
    cre = (nre * lre + aim * lim) / den
    cim = (aim * lre - nre * lim) / den
    cre = cre[:, None, :]
    cim = cim[:, None, :]
    bre = bre_ref[...]
    bim = bim_ref[...]
    bbre_ref[...] = cre * bre - cim * bim
    bbim_ref[...] = cre * bim + cim * bre


def _s5_params(lam_re, lam_im, log_dt, b_re, b_im):
    g, p = lam_re.shape
    bt_re = jnp.swapaxes(b_re, 1, 2)
    bt_im = jnp.swapaxes(b_im, 1, 2)
    cg = bt_re.shape[1]
    return pl.pallas_call(
        _s5_param_kernel,
        out_shape=[jax.ShapeDtypeStruct((g, p), F32), jax.ShapeDtypeStruct((g, p), F32),
                   jax.ShapeDtypeStruct((g, cg, p), F32), jax.ShapeDtypeStruct((g, cg, p), F32)],
        name="s5_params",
    )(lam_re, lam_im, log_dt.reshape(g, 1), bt_re, bt_im)


def _slab_blockdiag(m):
    g, cg, p = m.shape
    gl = g // S5_SLABS
    m4 = m.reshape(S5_SLABS, gl, cg, p)
    eye = jnp.eye(gl, dtype=m.dtype)
    return jnp.einsum('igcp,gh->igchp', m4, eye).reshape(S5_SLABS, gl * cg, gl * p)


def _slab_vec(v):
    g, p = v.shape
    return v.reshape(S5_SLABS, (g // S5_SLABS) * p)


def _gelu_tanh(x):
    return 0.5 * x * (1.0 + jnp.tanh(math.sqrt(2.0 / math.pi) * (x + 0.044715 * (x * x * x))))


def _s5_kernel(u_ref, x0_ref, a_ref, bmat_ref, cre_ref, cim_ref, d_ref, gluw_ref, glub_ref, ng_ref,
               o_ref, xout_ref, utb_scr, xs_scr, x_scr, y_scr, ab_scr, *, nb, tt, interleave):
    rows = nb * tt
    ss = S5_SLAB_STATE
    i = pl.program_id(0)

    @pl.when(i == 0)
    def _():
        x_scr[...] = x0_ref[...]
        ab_scr[...] = jnp.broadcast_to(a_ref[...], ab_scr.shape)

    if interleave:
        for b in range(nb):
            for s in range(S5_SLABS):
                utb_scr[pl.ds(s * rows + b, tt, stride=nb), :] = u_ref[b, :, s * LANES:(s + 1) * LANES]
    else:
        for s in range(S5_SLABS):
            utb_scr[s * rows:(s + 1) * rows, :] = u_ref[:, s * LANES:(s + 1) * LANES]

    for s in range(S5_SLABS):
        xs_scr[:, s * 2 * ss:(s + 1) * 2 * ss] = jnp.dot(utb_scr[s * rows:(s + 1) * rows, :].astype(BF16),
                                                       bmat_ref[s], preferred_element_type=F32)

    def step(t, carry):
        r0 = pl.multiple_of(t * nb, nb)
        for s in range(S5_SLABS):
            lo = s * 2 * ss
            ar = ab_scr[:, lo:lo + ss]
            ai = ab_scr[:, lo + ss:lo + 2 * ss]
            xr = x_scr[:, lo:lo + ss]
            xi = x_scr[:, lo + ss:lo + 2 * ss]
            nr = (ar * xr - ai * xi) + xs_scr[pl.ds(r0, nb), lo:lo + ss]
            ni = (ar * xi + ai * xr) + xs_scr[pl.ds(r0, nb), lo + ss:lo + 2 * ss]
            x_scr[:, lo:lo + ss] = nr
            x_scr[:, lo + ss:lo + 2 * ss] = ni
            xs_scr[pl.ds(r0, nb), lo:lo + ss] = nr
            xs_scr[pl.ds(r0, nb), lo + ss:lo + 2 * ss] = ni
        return carry

    lax.fori_loop(0, tt, step, 0, unroll=min(tt, 4))

    @pl.when(i == pl.num_programs(0) - 1)
    def _():
        xout_ref[...] = x_scr[...]

    ys = []
    for s in range(S5_SLABS):
        lo = s * 2 * ss
        yr = jnp.dot(xs_scr[:, lo:lo + ss].astype(BF16), cre_ref[s], preferred_element_type=F32)
        yi = jnp.dot(xs_scr[:, lo + ss:lo + 2 * ss].astype(BF16), cim_ref[s], preferred_element_type=F32)
        ys.append((yr - yi) + d_ref[0:1, s * LANES:(s + 1) * LANES] * utb_scr[s * rows:(s + 1) * rows, :])
    y = _gelu_tanh(jnp.concatenate(ys, axis=1))
    y = y * _sigmoid(_mm(y, gluw_ref[...]) + glub_ref[...])
    y = y * lax.rsqrt(jnp.mean(y * y, axis=-1, keepdims=True) + NORM_EPS) * ng_ref[...]
    if interleave:
        for s in range(S5_SLABS):
            y_scr[s * rows:(s + 1) * rows, :] = y[:, s * LANES:(s + 1) * LANES]
        for b in range(nb):
            for s in range(S5_SLABS):
                o_ref[b, :, s * LANES:(s + 1) * LANES] = y_scr[pl.ds(s * rows + b, tt, stride=nb), :]
    else:
        o_ref[...] = y


def _s5(u, x0, a, bmat, cre, cim, d, gluw, glub, ng, nb, tt, interleave):
    rows = nb * tt
    nstate = x0.shape[1]
    full = lambda arr: pl.BlockSpec(arr.shape, lambda i: (0,) * arr.ndim)
    if interleave:
        steps = u.shape[1] // tt
        uspec = pl.BlockSpec((nb, tt, S5_WIDTH), lambda i: (0, i, 0))
        oshape = jax.ShapeDtypeStruct(u.shape, F32)
    else:
        steps = 1
        uspec = pl.BlockSpec((nb, S5_WIDTH), lambda i: (0, 0))
        oshape = jax.ShapeDtypeStruct(u.shape, F32)
    return pl.pallas_call(
        functools.partial(_s5_kernel, nb=nb, tt=tt, interleave=interleave),
        grid=(steps,),
        in_specs=[uspec, full(x0), full(a), full(bmat), full(cre), full(cim), full(d), full(gluw), full(glub),
                  full(ng)],
        out_specs=[uspec, full(x0)],
        out_shape=[oshape, jax.ShapeDtypeStruct(x0.shape, F32)],
        scratch_shapes=[pltpu.VMEM((S5_SLABS * rows, LANES), F32), pltpu.VMEM((rows, nstate), F32),
                        pltpu.VMEM((nb, nstate), F32), pltpu.VMEM((S5_SLABS * rows, LANES), F32),
                        pltpu.VMEM((nb, nstate), F32)],
        compiler_params=_cparams(("arbitrary",)),
        name="s5_scan",
    )(u, x0, a, bmat, cre, cim, d, gluw, glub, ng)


def _router(logits_t, bias_col):
    scores = _sigmoid(logits_t)
    sel = scores + bias_col
    s = [sel[e:e + 1, :] for e in range(N_EXPERTS)]
    n_groups = N_EXPERTS // EXPERTS_PER_GROUP
    gs = []
    for gi in range(n_groups):
        m = s[gi * EXPERTS_PER_GROUP: (gi + 1) * EXPERTS_PER_GROUP]
        best = None
        for p in range(EXPERTS_PER_GROUP):
            for q in range(p + 1, EXPERTS_PER_GROUP):
                ps = m[p] + m[q]
                best = ps if best is None else jnp.maximum(best, ps)
        gs.append(best)
    gmax = functools.reduce(jnp.maximum, gs)
    taken = None
    in_best = []
    for gi in range(n_groups):
        hit = gs[gi] == gmax
        if taken is None:
            cur = hit
            taken = hit
        else:
            cur = jnp.logical_and(hit, jnp.logical_not(taken))
            taken = jnp.logical_or(taken, hit)
        in_best.append(cur)
    selm = []
    picked = []
    for e in range(N_EXPERTS):
        gi = e // EXPERTS_PER_GROUP
        cnt = jnp.zeros_like(s[e])
        for j in range(gi * EXPERTS_PER_GROUP, (gi + 1) * EXPERTS_PER_GROUP):
            if j == e:
                continue
            beats = (s[j] >= s[e]) if j < e else (s[j] > s[e])
            cnt = cnt + jnp.where(beats, 1.0, 0.0)
        sel_e = jnp.logical_and(in_best[gi], cnt < 1.5)
        selm.append(jnp.where(sel_e, 1.0, 0.0))
        picked.append(jnp.where(sel_e, scores[e:e + 1, :], 0.0))
    denom = functools.reduce(lambda x, y: x + y, picked)
    comb = [p / denom for p in picked]
    return comb, selm, in_best


def _router_dense(logits_t, bias_col):
    comb, _, _ = _router(logits_t, bias_col)
    return jnp.concatenate(comb, axis=0)


def _router_sparse(logits_t, bias_col):
    comb, selm, in_best = _router(logits_t, bias_col)
    cls = None
    wlo = None
    whi = None
    for gi in range(N_EXPERTS // EXPERTS_PER_GROUP):
        term = jnp.where(in_best[gi], float(len(PAIR_CODES) * gi) - 1.0, 0.0)
        cls = term if cls is None else cls + term
        seen = None
        for j in range(EXPERTS_PER_GROUP):
            e = gi * EXPERTS_PER_GROUP + j
            cls = cls + selm[e] * PAIR_VALUE[j]
            first = selm[e] if seen is None else selm[e] * (1.0 - seen)
            seen = selm[e] if seen is None else jnp.maximum(seen, selm[e])
            lo_term = first * comb[e]
            hi_term = (selm[e] - first) * comb[e]
            wlo = lo_term if wlo is None else wlo + lo_term
            whi = hi_term if whi is None else whi + hi_term
    return cls, wlo, whi


def _outproj_kernel(oa_ref, ob_ref, x_ref, gt_ref, sc_ref, sh_ref, g_ref, wout_ref, wrt_ref, br_ref,
                    xo_ref, hn_ref, comb_ref, *, tm):
    o = jnp.concatenate([oa_ref[...].astype(BF16), ob_ref[...].astype(BF16)], axis=1)
    mix = jnp.dot(o, wout_ref[...], preferred_element_type=F32)
    x = x_ref[...] + gt_ref[0] * mix
    xo_ref[...] = x
    ms = jnp.mean(x * x, axis=-1, keepdims=True)
    hn = x * lax.rsqrt(ms + NORM_EPS) * g_ref[...]
    hn = hn * (1.0 + sc_ref[0]) + sh_ref[0]
    hb = hn.astype(BF16)
    hn_ref[...] = hb
    logits_t = lax.dot_general(wrt_ref[...], hb, (((1,), (1,)), ((), ())), preferred_element_type=F32)
    comb_t = _router_dense(logits_t, br_ref[...])
    pad = jnp.zeros((LANES - N_EXPERTS, tm), F32)
    comb_ref[...] = jnp.concatenate([comb_t, pad], axis=0).T


def _outproj(oa, ob, x, gt, sc, sh, g, wout, wrt, br, tm, rows_per_seq):
    t = x.shape[0]
    full = lambda a: pl.BlockSpec(a.shape, lambda i: (0,) * a.ndim)
    row = lambda n: pl.BlockSpec((tm, n), lambda i: (i, 0))
    ms = lambda m: _mod_spec(m, tm, rows_per_seq)
    return pl.pallas_call(
        functools.partial(_outproj_kernel, tm=tm),
        grid=(t // tm,),
        in_specs=[row(GDN_WIDTH), row(S5_WIDTH), row(D_MODEL), ms(gt), ms(sc), ms(sh), full(g), full(wout),
                  full(wrt), full(br)],
        out_specs=[row(D_MODEL), row(D_MODEL), row(LANES)],
        out_shape=[jax.ShapeDtypeStruct((t, D_MODEL), F32), jax.ShapeDtypeStruct((t, D_MODEL), BF16),
                   jax.ShapeDtypeStruct((t, LANES), F32)],
        compiler_params=_cparams(("arbitrary",)),
        name="outproj_router",
    )(oa, ob, x, gt, sc, sh, g, wout, wrt, br)


def _outproj_sparse_kernel(oa_ref, ob_ref, x_ref, gt_ref, sc_ref, sh_ref, g_ref, wout_ref, wrt_ref, br_ref,
                           xo_ref, hx_ref, route_ref, cnt_ref, run_scr, *, tm):
    i = pl.program_id(0)

    @pl.when(i == 0)
    def _():
        run_scr[...] = jnp.zeros_like(run_scr)

    o = jnp.concatenate([oa_ref[...].astype(BF16), ob_ref[...].astype(BF16)], axis=1)
    mix = jnp.dot(o, wout_ref[...], preferred_element_type=F32)
    x = x_ref[...] + gt_ref[0] * mix
    xo_ref[...] = x
    ms = jnp.mean(x * x, axis=-1, keepdims=True)
    hn = x * lax.rsqrt(ms + NORM_EPS) * g_ref[...]
    hn = hn * (1.0 + sc_ref[0]) + sh_ref[0]
    hb = hn.astype(BF16)
    logits_t = lax.dot_general(wrt_ref[...], hb, (((1,), (1,)), ((), ())), preferred_element_type=F32)
    cls, wlo, whi = _router_sparse(logits_t, br_ref[...])
    sub = lax.broadcasted_iota(jnp.int32, (CLASS_ROWS, tm), 0).astype(F32)
    onehot = sub == cls
    r = lax.broadcasted_iota(jnp.int32, (tm, tm), 0)
    c = lax.broadcasted_iota(jnp.int32, (tm, tm), 1)
    before = jnp.where(r < c, 1.0, 0.0).astype(BF16)
    prefix = jnp.dot(jnp.where(onehot, 1.0, 0.0).astype(BF16), before, preferred_element_type=F32)
    run = run_scr[...]
    rank = jnp.sum(jnp.where(onehot, prefix + run[:, 0:1], 0.0), axis=0, keepdims=True)
    run_scr[...] = run + jnp.sum(jnp.where(onehot, 1.0, 0.0), axis=1, keepdims=True)
    cnt_ref[...] = run_scr[...]
    route = jnp.concatenate([cls, rank, wlo, whi, jnp.zeros((SUBLANES - 4, tm), F32)], axis=0)
    route_ref[...] = route
    hx_ref[:, :D_MODEL] = hn
    hx_ref[:, D_MODEL:] = jnp.concatenate([route, jnp.zeros((LANES - SUBLANES, tm), F32)], axis=0).T


def _outproj_sparse(oa, ob, x, gt, sc, sh, g, wout, wrt, br, tm, rows_per_seq):
    t = x.shape[0]
    full = lambda a: pl.BlockSpec(a.shape, lambda i: (0,) * a.ndim)
    row = lambda n: pl.BlockSpec((tm, n), lambda i: (i, 0))
    ms = lambda m: _mod_spec(m, tm, rows_per_seq)
    return pl.pallas_call(
        functools.partial(_outproj_sparse_kernel, tm=tm),
        grid=(t // tm,),
        in_specs=[row(GDN_WIDTH), row(S5_WIDTH), row(D_MODEL), ms(gt), ms(sc), ms(sh), full(g), full(wout),
                  full(wrt), full(br)],
        out_specs=[row(D_MODEL), row(HX_WIDTH), pl.BlockSpec((SUBLANES, tm), lambda i: (0, i)),
                   pl.BlockSpec((CLASS_ROWS, LANES), lambda i: (0, 0))],
        out_shape=[jax.ShapeDtypeStruct((t, D_MODEL), F32), jax.ShapeDtypeStruct((t, HX_WIDTH), F32),
                   jax.ShapeDtypeStruct((SUBLANES, t), F32), jax.ShapeDtypeStruct((CLASS_ROWS, LANES), F32)],
        scratch_shapes=[pltpu.VMEM((CLASS_ROWS, LANES), F32)],
        compiler_params=_cparams(("arbitrary",)),
        name="outproj_route",
    )(oa, ob, x, gt, sc, sh, g, wout, wrt, br)


def _dispatch_kernel(pos_ref, tail_ref, hx_ref, sorted_ref, stage, zero_scr, sems, zsem, *, tm, n_steps):
    i = pl.program_id(0)
    slot = i % 2

    def tail_copy(cidx):
        start = pl.multiple_of(jnp.maximum(tail_ref[cidx], 0), MOE_TILE)
        return pltpu.make_async_copy(zero_scr, sorted_ref.at[pl.ds(start, MOE_TILE), :], zsem)

    @pl.when(i == 0)
    def _():
        zero_scr[...] = jnp.zeros_like(zero_scr)
        for cidx in range(2 * N_CLASSES):
            @pl.when(tail_ref[cidx] >= 0)
            def _():
                tail_copy(cidx).start()
        for cidx in range(2 * N_CLASSES):
            @pl.when(tail_ref[cidx] >= 0)
            def _():
                tail_copy(cidx).wait()

    def row_copy(s, r, p):
        return pltpu.make_async_copy(stage.at[s, pl.ds(r, 1), :], sorted_ref.at[pl.ds(p, 1), :], sems.at[s])

    def wait_slot(s):
        pltpu.make_async_copy(stage.at[s], sorted_ref.at[pl.ds(0, tm), :], sems.at[s]).wait()

    @pl.when(i >= 2)
    def _():
        wait_slot(slot)

    stage[slot] = hx_ref[...]
    base = i * tm

    def issue(r, carry):
        row_copy(slot, r, pos_ref[base + r]).start()
        return carry

    lax.fori_loop(0, tm, issue, 0, unroll=8)

    @pl.when(i == n_steps - 1)
    def _():
        wait_slot(slot)
        if n_steps > 1:
            wait_slot(1 - slot)


def _dispatch(pos, tail, hx, n_rows, tm):
    t = hx.shape[0]
    n_steps = t // tm
    return pl.pallas_call(
        functools.partial(_dispatch_kernel, tm=tm, n_steps=n_steps),
        grid_spec=pltpu.PrefetchScalarGridSpec(
            num_scalar_prefetch=2,
            grid=(n_steps,),
            in_specs=[pl.BlockSpec((tm, HX_WIDTH), lambda i, pos, tail: (i, 0))],
            out_specs=pl.BlockSpec(memory_space=pl.ANY),
            scratch_shapes=[pltpu.VMEM((2, tm, HX_WIDTH), F32), pltpu.VMEM((MOE_TILE, HX_WIDTH), F32),
                            pltpu.SemaphoreType.DMA((2,)), pltpu.SemaphoreType.DMA(())]),
        out_shape=jax.ShapeDtypeStruct((n_rows, HX_WIDTH), F32),
        compiler_params=_cparams(("arbitrary",)),
        name="moe_dispatch",
    )(pos, tail, hx)


def _moe_sorted_kernel(lo_ref, hi_ref, src_ref, nv_ref, xs_ref, wgu_lo_ref, wgu_hi_ref, wd_lo_ref, wd_hi_ref,
                       o_ref):
    j = pl.program_id(0)

    @pl.when(j < nv_ref[0])
    def _():
        xs = xs_ref[...]
        x = xs[:, :D_MODEL].astype(BF16)
        wlo = xs[:, D_MODEL + 2:D_MODEL + 3]
        whi = xs[:, D_MODEL + 3:D_MODEL + 4]
        h_lo = jnp.dot(x, wgu_lo_ref[0], preferred_element_type=F32)
        a_lo = _silu(h_lo[:, :D_EXPERT]) * h_lo[:, D_EXPERT:] * wlo
        h_hi = jnp.dot(x, wgu_hi_ref[0], preferred_element_type=F32)
        a_hi = _silu(h_hi[:, :D_EXPERT]) * h_hi[:, D_EXPERT:] * whi
        o_ref[...] = (jnp.dot(a_lo.astype(BF16), wd_lo_ref[0], preferred_element_type=F32)
                      + jnp.dot(a_hi.astype(BF16), wd_hi_ref[0], preferred_element_type=F32))

    @pl.when(j >= nv_ref[0])
    def _():
        o_ref[...] = jnp.zeros_like(o_ref)


def _moe_sorted(tile_lo, tile_hi, tile_src, n_valid, xs, wgu, wd):
    n_rows = xs.shape[0]
    n_tiles = n_rows // MOE_TILE
    return pl.pallas_call(
        _moe_sorted_kernel,
        grid_spec=pltpu.PrefetchScalarGridSpec(
            num_scalar_prefetch=4,
            grid=(n_tiles,),
            in_specs=[pl.BlockSpec((MOE_TILE, HX_WIDTH), lambda j, lo, hi, src, nv: (src[j], 0)),
                      pl.BlockSpec((1, D_MODEL, 2 * D_EXPERT), lambda j, lo, hi, src, nv: (lo[j], 0, 0)),
                      pl.BlockSpec((1, D_MODEL, 2 * D_EXPERT), lambda j, lo, hi, src, nv: (hi[j], 0, 0)),
                      pl.BlockSpec((1, D_EXPERT, D_MODEL), lambda j, lo, hi, src, nv: (lo[j], 0, 0)),
                      pl.BlockSpec((1, D_EXPERT, D_MODEL), lambda j, lo, hi, src, nv: (hi[j], 0, 0))],
            out_specs=pl.BlockSpec((MOE_TILE, D_MODEL), lambda j, lo, hi, src, nv: (j, 0))),
        out_shape=jax.ShapeDtypeStruct((n_rows, D_MODEL), F32),
        compiler_params=_cparams(("arbitrary",)),
        name="moe_sorted",
    )(tile_lo, tile_hi, tile_src, n_valid, xs, wgu, wgu, wd, wd)


def _combine_kernel(pos_ref, ys_ref, x_ref, gt_ref, fg_ref, o_ref, buf, sems, *, tm, n_steps, final_norm):
    i = pl.program_id(0)
    slot = i % 2

    def row_copy(s, r, p):
        return pltpu.make_async_copy(ys_ref.at[pl.ds(p, 1), :], buf.at[s, pl.ds(r, 1), :], sems.at[s])

    def issue(step, s):
        base = step * tm

        def body(r, carry):
            row_copy(s, r, pos_ref[base + r]).start()
            return carry

        lax.fori_loop(0, tm, body, 0, unroll=8)

    @pl.when(i == 0)
    def _():
        issue(0, 0)

    @pl.when(i + 1 < n_steps)
    def _():
        issue(i + 1, 1 - slot)

    pltpu.make_async_copy(ys_ref.at[pl.ds(0, tm), :], buf.at[slot], sems.at[slot]).wait()
    x = x_ref[...] + gt_ref[0] * buf[slot]
    if final_norm:
        x = x * lax.rsqrt(jnp.mean(x * x, axis=-1, keepdims=True) + NORM_EPS) * fg_ref[...]
    o_ref[...] = x


def _combine(pos, ys, x, gt, fg, tm, rows_per_seq, final_norm):
    t = x.shape[0]
    n_steps = t // tm
    tiles_per_seq = rows_per_seq // tm
    return pl.pallas_call(
        functools.partial(_combine_kernel, tm=tm, n_steps=n_steps, final_norm=final_norm),
        grid_spec=pltpu.PrefetchScalarGridSpec(
            num_scalar_prefetch=1,
            grid=(n_steps,),
            in_specs=[pl.BlockSpec(memory_space=pl.ANY),
                      pl.BlockSpec((tm, D_MODEL), lambda i, pos: (i, 0)),
                      pl.BlockSpec((1, 1, D_MODEL), lambda i, pos: (i // tiles_per_seq, 0, 0)),
                      pl.BlockSpec(fg.shape, lambda i, pos: (0, 0))],
            out_specs=pl.BlockSpec((tm, D_MODEL), lambda i, pos: (i, 0)),
            scratch_shapes=[pltpu.VMEM((2, tm, D_MODEL), F32), pltpu.SemaphoreType.DMA((2,))]),
        out_shape=jax.ShapeDtypeStruct((t, D_MODEL), F32),
        compiler_params=_cparams(("arbitrary",)),
        name="moe_combine",
    )(pos, ys, x, gt, fg)


def _route_plan(route, counts, n_tokens):
    cnt = counts[:N_CLASSES, 0].astype(jnp.int32)
    padded = ((cnt + MOE_TILE - 1) // MOE_TILE) * MOE_TILE
    ends = jnp.cumsum(padded)
    offsets = ends - padded
    cls = route[0].astype(jnp.int32)
    pos = offsets[cls] + route[1].astype(jnp.int32)
    n_tiles = n_tokens // MOE_TILE + N_CLASSES
    n_valid = ends[-1] // MOE_TILE
    src = jnp.minimum(jnp.arange(n_tiles, dtype=jnp.int32), jnp.maximum(n_valid - 1, 0))
    tile_cls = jnp.minimum(jnp.sum((src[:, None] * MOE_TILE >= ends[None, :]).astype(jnp.int32), axis=1),
                           N_CLASSES - 1)
    group = tile_cls // len(PAIR_CODES)
    pair = jnp.asarray(PAIR_CODES, dtype=jnp.int32)[tile_cls % len(PAIR_CODES)]
    tile_lo = group * EXPERTS_PER_GROUP + pair[:, 0]
    tile_hi = group * EXPERTS_PER_GROUP + pair[:, 1]
    unused = (n_valid + jnp.arange(N_CLASSES, dtype=jnp.int32)) * MOE_TILE
    tail = jnp.concatenate([jnp.where(cnt > 0, ends - MOE_TILE, -1),
                            jnp.where(unused < n_tiles * MOE_TILE, unused, -1)]).astype(jnp.int32)
    return pos, tail, tile_lo, tile_hi, src, n_valid.reshape(1).astype(jnp.int32), n_tiles * MOE_TILE


def _moe_kernel(hn_ref, comb_ref, x_ref, gt_ref, wgu_ref, wd_ref, fg_ref, o_ref, acc_scr, *, final_norm):
    e = pl.program_id(1)

    @pl.when(e == 0)
    def _():
        acc_scr[...] = jnp.zeros_like(acc_scr)

    h = jnp.dot(hn_ref[...], wgu_ref[0], preferred_element_type=F32)
    hg = h[:, :D_EXPERT]
    hu = h[:, D_EXPERT:]
    lane = lax.broadcasted_iota(jnp.int32, (1, LANES), 1)
    ce = jnp.sum(jnp.where(lane == e, comb_ref[...], 0.0), axis=-1, keepdims=True)
    act = _silu(hg) * hu * ce
    acc_scr[...] += jnp.dot(act.astype(BF16), wd_ref[0], preferred_element_type=F32)

    @pl.when(e == pl.num_programs(1) - 1)
    def _():
        x = x_ref[...] + gt_ref[0] * acc_scr[...]
        if final_norm:
            x = x * lax.rsqrt(jnp.mean(x * x, axis=-1, keepdims=True) + NORM_EPS) * fg_ref[...]
        o_ref[...] = x


def _moe(hn, comb, x, gt, wgu, wd, fg, tm, rows_per_seq, final_norm):
    t = x.shape[0]
    row = lambda n: pl.BlockSpec((tm, n), lambda i, e: (i, 0))
    if gt.shape[1] == 1:
        tiles_per_seq = rows_per_seq // tm
        gspec = pl.BlockSpec((1, 1, D_MODEL), lambda i, e: (i // tiles_per_seq, 0, 0))
    else:
        gspec = pl.BlockSpec((1, tm, D_MODEL), lambda i, e: (0, i, 0))
    return pl.pallas_call(
        functools.partial(_moe_kernel, final_norm=final_norm),
        grid=(t // tm, N_EXPERTS),
        in_specs=[row(D_MODEL), row(LANES), row(D_MODEL), gspec,
                  pl.BlockSpec((1, D_MODEL, 2 * D_EXPERT), lambda i, e: (e, 0, 0)),
                  pl.BlockSpec((1, D_EXPERT, D_MODEL), lambda i, e: (e, 0, 0)),
                  pl.BlockSpec(fg.shape, lambda i, e: (0, 0))],
        out_specs=row(D_MODEL),
        out_shape=jax.ShapeDtypeStruct((t, D_MODEL), F32),
        scratch_shapes=[pltpu.VMEM((tm, D_MODEL), F32)],
        compiler_params=_cparams(("arbitrary", "arbitrary")),
        name="moe",
    )(hn, comb, x, gt, wgu, wd, fg)


def _pad_lanes(v, offset):
    return jnp.zeros((1, LANES), F32).at[0, offset:offset + v.shape[0]].set(v)


def _state_to_slab(re, im):
    b = re.shape[0]
    r = re.reshape(b, S5_SLABS, S5_SLAB_STATE)
    i = im.reshape(b, S5_SLABS, S5_SLAB_STATE)
    return jnp.stack([r, i], axis=2).reshape(b, S5_SLABS * 2 * S5_SLAB_STATE)


def _slab_to_state(x):
    b = x.shape[0]
    x4 = x.reshape(b, S5_SLABS, 2, S5_SLAB_STATE)
    re = x4[:, :, 0].reshape(b, S5_GROUPS, S5_STATE)
    im = x4[:, :, 1].reshape(b, S5_GROUPS, S5_STATE)
    return re, im


def kernel(x_prompt, x_sample, c_prompt, c_sample, state_conv, state_gdn, state_s5_re, state_s5_im, norm1_g, norm2_g, w_ada, b_ada, w_in, conv_w, a_log, dt_bias, gdn_norm_g, s5_lambda_re, s5_lambda_im, s5_log_dt, s5_b_re, s5_b_im, s5_c_re, s5_c_im, s5_d, s5_glu_w, s5_glu_b, s5_norm_g, w_out, w_router, b_router, w_gate, w_up, w_down, final_g):
    bp, seq, _ = x_prompt.shape
    bs = x_sample.shape[0]
    tp = bp * seq

    mod = _ada(jnp.concatenate([c_prompt, c_sample], axis=0), w_ada, b_ada)

    def mod_slices(l):
        mp = [mod[l, :bp, j * D_MODEL:(j + 1) * D_MODEL].reshape(bp, 1, D_MODEL) for j in range(6)]
        msm = [mod[l, bp:, j * D_MODEL:(j + 1) * D_MODEL].reshape(1, bs, D_MODEL) for j in range(6)]
        return mp, msm

    wrt = jnp.transpose(w_router).astype(BF16)
    br = b_router.reshape(N_EXPERTS, 1)
    fg = final_g.reshape(1, D_MODEL)

    xp = x_prompt.reshape(tp, D_MODEL)
    xs = x_sample.reshape(bs, D_MODEL)
    tm_p = 512
    tt = 64 if seq % 64 == 0 else seq

    outs_p = {k: [] for k in ("conv", "gdn", "re", "im")}
    outs_s = {k: [] for k in ("conv", "gdn", "re", "im")}
    zero_conv = jnp.zeros((bp, SUBLANES, QKV_WIDTH), F32)
    zero_gdn = jnp.zeros((bp, GDN_HEADS, GDN_HEAD_DIM, GDN_HEAD_DIM), F32)
    zero_s5 = jnp.zeros((bp, S5_SLABS * 2 * S5_SLAB_STATE), F32)

    for l in range(DEPTH):
        (sh1p, sc1p, gt1p, sh2p, sc2p, gt2p), (sh1s, sc1s, gt1s, sh2s, sc2s, gt2s) = mod_slices(l)
        wl = w_in[l]
        wqkv = wl[:, :QKV_WIDTH].astype(BF16)
        wz = wl[:, QKV_WIDTH:QKV_WIDTH + GDN_WIDTH].astype(BF16)
        nba = 2 * GDN_HEADS
        wba = jnp.zeros((D_MODEL, LANES), F32).at[:, :nba].set(
            wl[:, QKV_WIDTH + GDN_WIDTH:QKV_WIDTH + GDN_WIDTH + nba]).astype(BF16)
        wu = wl[:, QKV_WIDTH + GDN_WIDTH + nba:].astype(BF16)
        g1 = norm1_g[l].reshape(1, D_MODEL)
        g2 = norm2_g[l].reshape(1, D_MODEL)
        cw = conv_w[l]
        alog = _pad_lanes(a_log[l], GDN_HEADS)
        dtb = _pad_lanes(dt_bias[l], GDN_HEADS)
        gn = gdn_norm_g[l].reshape(1, GDN_HEAD_DIM)
        a_re, a_im, bb_re, bb_im = _s5_params(s5_lambda_re[l], s5_lambda_im[l], s5_log_dt[l], s5_b_re[l], s5_b_im[l])
        a_vec = jnp.concatenate([_slab_vec(a_re), _slab_vec(a_im)], axis=1).reshape(1, -1)
        bmat = jnp.concatenate([_slab_blockdiag(bb_re), _slab_blockdiag(bb_im)], axis=2).astype(BF16)
        cre = jnp.swapaxes(_slab_blockdiag(s5_c_re[l]), 1, 2).astype(BF16)
        cim = jnp.swapaxes(_slab_blockdiag(s5_c_im[l]), 1, 2).astype(BF16)
        dvec = s5_d[l].reshape(1, S5_WIDTH)
        gluw = s5_glu_w[l].astype(BF16)
        glub = s5_glu_b[l].reshape(1, S5_WIDTH)
        ng = s5_norm_g[l].reshape(1, S5_WIDTH)
        wout = w_out[l].astype(BF16)
        wgu = jnp.concatenate([w_gate[l], w_up[l]], axis=-1).astype(BF16)
        wd = w_down[l].astype(BF16)
        last = l == DEPTH - 1

        qkv, z, ba, u = _inproj(xp, sc1p, sh1p, g1, wqkv, wz, wba, wu, tm_p, seq)
        oa, sg, cv = _gdn_prompt(qkv, ba, z, cw, alog, dtb, gn, zero_conv, zero_gdn, tm_p)
        ob, xst = _s5(u.reshape(bp, seq, S5_WIDTH), zero_s5, a_vec, bmat, cre, cim, dvec, gluw, glub, ng,
                      bp, tt, True)
        xp, hx, route, counts = _outproj_sparse(oa, ob.reshape(tp, S5_WIDTH), xp, gt1p, sc2p, sh2p, g2, wout, wrt,
                                                br, tm_p, seq)
        pos, tail, tile_lo, tile_hi, tile_src, n_valid, n_rows = _route_plan(route, counts, tp)
        xsorted = _dispatch(pos, tail, hx, n_rows, tm_p)
        ys = _moe_sorted(tile_lo, tile_hi, tile_src, n_valid, xsorted, wgu, wd)
        xp = _combine(pos, ys, xp, gt2p, fg, tm_p, seq, last)
        outs_p["conv"].append(cv[:, SUBLANES - (CONV_WIDTH - 1):, :])
        outs_p["gdn"].append(sg)
        re, im = _slab_to_state(xst)
        outs_p["re"].append(re)
        outs_p["im"].append(im)

        qkv, z, ba, u = _inproj(xs, sc1s, sh1s, g1, wqkv, wz, wba, wu, bs, 1)
        oa, sg, cv = _gdn_step(qkv, ba, z, cw, alog, dtb, gn,
                               state_conv[l].reshape(bs, (CONV_WIDTH - 1) * QKV_WIDTH), state_gdn[l])
        ob, xst = _s5(u, _state_to_slab(state_s5_re[l], state_s5_im[l]), a_vec, bmat, cre, cim, dvec, gluw,
                      glub, ng, bs, 1, False)
        xs, hn, comb = _outproj(oa, ob, xs, gt1s, sc2s, sh2s, g2, wout, wrt, br, bs, 1)
        xs = _moe(hn, comb, xs, gt2s, wgu, wd, fg, bs, 1, last)
        outs_s["conv"].append(cv.reshape(bs, CONV_WIDTH - 1, QKV_WIDTH))
        outs_s["gdn"].append(sg)
        re, im = _slab_to_state(xst)
        outs_s["re"].append(re)
        outs_s["im"].append(im)

    st = lambda d, k: jnp.stack(d[k])
    return (xp.reshape(bp, seq, D_MODEL), xs.reshape(bs, 1, D_MODEL),
            st(outs_p, "conv"), st(outs_p, "gdn"), st(outs_p, "re"), st(outs_p, "im"),
            st(outs_s, "conv"), st(outs_s, "gdn"), st(outs_s, "re"), st(outs_s, "im"))
```

```python
import functools
import math

import jax
import jax.numpy as jnp
from jax import lax
from jax.experimental import pallas as pl
from jax.experimental.pallas import tpu as pltpu

F32 = jnp.float32
BF16 = jnp.bfloat16

D_MODEL = 1024
DEPTH = 2
GDN_HEAD_DIM = 128
GDN_WIDTH = 512
GDN_HEADS = 4
CONV_WIDTH = 4
S5_CH_PER_GROUP = 16
S5_WIDTH = 512
S5_GROUPS = 32
S5_STATE = 64
QKV_WIDTH = 3 * GDN_WIDTH
N_EXPERTS = 16
EXPERTS_PER_GROUP = 4
D_EXPERT = 256
NORM_EPS = 1e-6

LANES = 128
SUBLANES = 8
GDN_CHUNK = 128
GDN_CHUNKS_PER_ITER = 2
S5_SLABS = 4
S5_SLAB_STATE = 512
MOE_TILE = 256
PAIR_VALUE = (0.0, 1.0, 2.0, 4.0)
PAIR_CODES = ((0, 1), (0, 2), (1, 2), (0, 3), (1, 3), (2, 3))
N_CLASSES = (N_EXPERTS // EXPERTS_PER_GROUP) * len(PAIR_CODES)
CLASS_ROWS = 32
HX_WIDTH = D_MODEL + LANES
VMEM_LIMIT = 56 * 1024 * 1024


def _cparams(sem):
    return pltpu.CompilerParams(dimension_semantics=sem, vmem_limit_bytes=VMEM_LIMIT)


def _sigmoid(x):
    return 1.0 / (1.0 + jnp.exp(-x))


def _silu(x):
    return x * _sigmoid(x)


def _softplus(x):
    return jnp.maximum(x, 0.0) + jnp.log1p(jnp.exp(-jnp.abs(x)))


def _mm(a, b):
    return jnp.dot(a.astype(BF16), b.astype(BF16), preferred_element_type=F32)


def _ada_kernel(c_ref, w_ref, b_ref, o_ref):
    c = c_ref[...]
    o_ref[0] = _mm(_silu(c), w_ref[0]) + b_ref[0]


def _ada(c_all, w_ada, b_ada):
    rows = c_all.shape[0]
    n_out = w_ada.shape[-1]
    tn = 1536
    return pl.pallas_call(
        _ada_kernel,
        grid=(DEPTH, n_out // tn),
        in_specs=[
            pl.BlockSpec((rows, D_MODEL), lambda l, j: (0, 0)),
            pl.BlockSpec((1, D_MODEL, tn), lambda l, j: (l, 0, j)),
            pl.BlockSpec((1, 1, tn), lambda l, j: (l, 0, j)),
        ],
        out_specs=pl.BlockSpec((1, rows, tn), lambda l, j: (l, 0, j)),
        out_shape=jax.ShapeDtypeStruct((DEPTH, rows, n_out), F32),
        compiler_params=_cparams(("arbitrary", "arbitrary")),
        name="ada_mod",
    )(c_all, w_ada, b_ada.reshape(DEPTH, 1, n_out))


def _inproj_kernel(x_ref, sc_ref, sh_ref, g_ref, wqkv_ref, wz_ref, wba_ref, wu_ref,
                   qkv_ref, z_ref, ba_ref, u_ref):
    x = x_ref[...]
    ms = jnp.mean(x * x, axis=-1, keepdims=True)
    hn = x * lax.rsqrt(ms + NORM_EPS) * g_ref[...]
    hn = hn * (1.0 + sc_ref[...]) + sh_ref[...]
    hb = hn.astype(BF16)
    qkv_ref[...] = jnp.dot(hb, wqkv_ref[...], preferred_element_type=F32)
    z_ref[...] = jnp.dot(hb, wz_ref[...], preferred_element_type=F32)
    ba_ref[...] = jnp.dot(hb, wba_ref[...], preferred_element_type=F32)
    u_ref[...] = jnp.dot(hb, wu_ref[...], preferred_element_type=F32)


def _mod_spec(mod, l, j, tm, rows_per_seq):
    if mod.ndim == 4:
        tiles_per_seq = rows_per_seq // tm
        return pl.BlockSpec((None, None, 1, D_MODEL), lambda i, *_: (l, i // tiles_per_seq, 0, j))
    return pl.BlockSpec((None, tm, D_MODEL), lambda i, *_: (l, i, j))


def _layer_spec(arr, l):
    return pl.BlockSpec((None,) + arr.shape[1:], lambda *_: (l,) + (0,) * (arr.ndim - 1))


def _inproj(x, mod, l, g, w_in_packed, tm, rows_per_seq):
    t = x.shape[0]
    row = lambda n: pl.BlockSpec((tm, n), lambda i: (i, 0))
    wcol = lambda width, start: pl.BlockSpec((None, D_MODEL, width), lambda i: (l, 0, start // width))
    u_start = QKV_WIDTH + GDN_WIDTH
    return pl.pallas_call(
        _inproj_kernel,
        grid=(t // tm,),
        in_specs=[row(D_MODEL), _mod_spec(mod, l, 1, tm, rows_per_seq), _mod_spec(mod, l, 0, tm, rows_per_seq),
                  _layer_spec(g, l), wcol(QKV_WIDTH, 0), wcol(GDN_WIDTH, QKV_WIDTH),
                  wcol(LANES, u_start + S5_WIDTH), wcol(S5_WIDTH, u_start)],
        out_specs=[row(QKV_WIDTH), row(GDN_WIDTH), row(LANES), row(S5_WIDTH)],
        out_shape=[jax.ShapeDtypeStruct((t, QKV_WIDTH), F32), jax.ShapeDtypeStruct((t, GDN_WIDTH), F32),
                   jax.ShapeDtypeStruct((t, LANES), F32), jax.ShapeDtypeStruct((t, S5_WIDTH), F32)],
        compiler_params=_cparams(("arbitrary",)),
        name="inproj",
    )(x, mod, mod, g, w_in_packed, w_in_packed, w_in_packed, w_in_packed)


def _gdn_gates(ba, alog, dtb):
    beta = _sigmoid(ba)
    g = -jnp.exp(alog) * _softplus(ba + dtb)
    return beta, g


def _l2n(x):
    return x * lax.rsqrt(jnp.sum(x * x, axis=-1, keepdims=True) + NORM_EPS)


def _gated_norm(o, gn, z):
    on = o * lax.rsqrt(jnp.mean(o * o, axis=-1, keepdims=True) + NORM_EPS) * gn
    return on * _silu(z)


def _gdn_prompt_kernel(qkv_ref, ba_ref, z_ref, cw_ref, alog_ref, dtb_ref, gn_ref, conv0_ref, s0_ref,
                       o_ref, sout_ref, convout_ref, xp_scr, y_scr, g_scr, b_scr, s_scr, wq_scr, ak_scr, u0_scr,
                       egl_scr, *, tm):
    c_len = GDN_CHUNK
    i = pl.program_id(1)
    last = pl.num_programs(1) - 1

    @pl.when(i == 0)
    def _():
        xp_scr[0:SUBLANES, :] = conv0_ref[0]
        s_scr[...] = s0_ref[0]

    xp_scr[SUBLANES:SUBLANES + tm, :] = qkv_ref[...]
    cw = cw_ref[...]
    n_tiles = tm // SUBLANES
    x3 = xp_scr[...].reshape(n_tiles + 1, SUBLANES, QKV_WIDTH)
    sub = lax.broadcasted_iota(jnp.int32, (1, SUBLANES, QKV_WIDTH), 1)

    def delayed(s):
        rot = pltpu.roll(x3, s, axis=1)
        return jnp.where(sub >= s, rot[1:], rot[:-1])

    y = delayed(3) * cw[0:1, :].reshape(1, 1, QKV_WIDTH)
    y = y + delayed(2) * cw[1:2, :].reshape(1, 1, QKV_WIDTH)
    y = y + delayed(1) * cw[2:3, :].reshape(1, 1, QKV_WIDTH)
    y = y + x3[1:] * cw[3:4, :].reshape(1, 1, QKV_WIDTH)
    y_scr[...] = _silu(y).reshape(tm, QKV_WIDTH)
    tail = xp_scr[tm:tm + SUBLANES, :]
    xp_scr[0:SUBLANES, :] = tail

    @pl.when(i == last)
    def _():
        convout_ref[0] = tail

    beta, g = _gdn_gates(ba_ref[...], alog_ref[...], dtb_ref[...])
    b_scr[...] = beta
    g_scr[...] = g

    r = lax.broadcasted_iota(jnp.int32, (c_len, c_len), 0)
    c = lax.broadcasted_iota(jnp.int32, (c_len, c_len), 1)
    ge = r >= c
    gt = r > c
    tri = jnp.where(ge, 1.0, 0.0).astype(BF16)
    eye = jnp.where(r == c, 1.0, 0.0).astype(F32)
    blk16 = (r // 16) == (c // 16)
    pair_masks = [((r // (2 * s)) == (c // (2 * s))) & ((r // s) != (c // s)) for s in (16, 32, 64)]
    gn = gn_ref[...]
    scale = GDN_HEAD_DIM ** -0.5
    hd = GDN_HEAD_DIM

    def phase_a(p, carry):
        chains = []
        for ck in range(GDN_CHUNKS_PER_ITER):
            ci = p * GDN_CHUNKS_PER_ITER + ck
            rows = pl.ds(pl.multiple_of(ci * c_len, c_len), c_len)
            gch = g_scr[rows, :]
            bch = b_scr[rows, :]
            g1 = gch.astype(BF16)
            r1 = gch - g1.astype(F32)
            g2 = r1.astype(BF16)
            g3 = (r1 - g2.astype(F32)).astype(BF16)
            gcum = (jnp.dot(tri, g1, preferred_element_type=F32) + jnp.dot(tri, g2, preferred_element_type=F32)
                    + jnp.dot(tri, g3, preferred_element_type=F32))
            gcum_t = gcum.T
            glast = gcum[c_len - 1:c_len, :]
            egl_scr[ci] = jnp.broadcast_to(jnp.exp(glast), (SUBLANES, LANES))
            for h in range(GDN_HEADS):
                lo = h * hd
                q = _l2n(y_scr[rows, lo:lo + hd]) * scale
                k = _l2n(y_scr[rows, GDN_WIDTH + lo:GDN_WIDTH + lo + hd])
                v = y_scr[rows, 2 * GDN_WIDTH + lo:2 * GDN_WIDTH + lo + hd]
                gl = GDN_HEADS + h
                gcb = jnp.broadcast_to(gcum[:, gl:gl + 1], (c_len, c_len))
                bcol = jnp.broadcast_to(bch[:, h:h + 1], (c_len, c_len))
                egc = jnp.exp(gcb)
                kdf = jnp.exp(glast[:, gl:gl + 1] - gcb)
                decay = jnp.where(ge, jnp.exp(gcb - gcum_t[gl:gl + 1, :]), 0.0)
                kb = k.astype(BF16)
                chains.append(dict(
                    ci=ci, h=h, decay=decay, bcol=bcol, kb=kb,
                    kq=jnp.concatenate([kb, q.astype(BF16)], axis=0),
                    rhs=jnp.concatenate([((bcol * egc) * k).astype(BF16), (bcol * v).astype(BF16)], axis=1),
                    qg=(q * egc).astype(BF16),
                    kdt=(k * kdf).T.astype(BF16)))
        for ch in chains:
            ch["kkqk"] = lax.dot_general(ch["kq"], ch["kb"], (((1,), (1,)), ((), ())), preferred_element_type=F32)
        for ch in chains:
            ch["lmat"] = jnp.where(gt, ch["bcol"] * ch["kkqk"][:c_len] * ch["decay"], 0.0)
            ch["n1"] = jnp.where(blk16, -ch["lmat"], 0.0)
            ch["t"] = eye + ch["n1"]
        for ch in chains:
            ch["n2"] = _mm(ch["n1"], ch["n1"])
        for ch in chains:
            ch["n4"] = _mm(ch["n2"], ch["n2"])
            ch["t"] = ch["t"] + _mm(ch["t"], ch["n2"])
        for ch in chains:
            ch["n8"] = _mm(ch["n4"], ch["n4"])
            ch["t"] = ch["t"] + _mm(ch["t"], ch["n4"])
        for ch in chains:
            ch["t"] = ch["t"] + _mm(ch["t"], ch["n8"])
        for pm in pair_masks:
            for ch in chains:
                ch["x"] = _mm(ch["t"], jnp.where(pm, ch["lmat"], 0.0))
            for ch in chains:
                ch["t"] = ch["t"] - _mm(ch["x"], ch["t"])
        for ch in chains:
            wu = jnp.dot(ch["t"].astype(BF16), ch["rhs"], preferred_element_type=F32)
            ci, h = ch["ci"], ch["h"]
            wq_scr[ci, h] = jnp.concatenate([wu[:, :hd].astype(BF16), ch["qg"]], axis=0)
            u0_scr[ci, h] = wu[:, hd:]
            ak_scr[ci, h] = jnp.concatenate([(ch["kkqk"][c_len:] * ch["decay"]).astype(BF16), ch["kdt"]], axis=0)
        return carry

    lax.fori_loop(0, tm // c_len // GDN_CHUNKS_PER_ITER, phase_a, 0)

    def phase_b(ci, carry):
        rows = pl.ds(pl.multiple_of(ci * c_len, c_len), c_len)
        egl = egl_scr[ci]
        heads = range(GDN_HEADS)
        ss = [s_scr[h] for h in heads]
        wsqs = [jnp.dot(wq_scr[ci, h], ss[h].astype(BF16), preferred_element_type=F32) for h in heads]
        us = [(u0_scr[ci, h] - wsqs[h][:c_len]).astype(BF16) for h in heads]
        auku = [jnp.dot(ak_scr[ci, h], us[h], preferred_element_type=F32) for h in heads]
        for h in heads:
            lo = h * hd
            gl = GDN_HEADS + h
            o = wsqs[h][c_len:] + auku[h][:c_len]
            s_scr[h] = egl[0:1, gl:gl + 1] * ss[h] + auku[h][c_len:]
            o_ref[rows, lo:lo + hd] = _gated_norm(o, gn, z_ref[rows, lo:lo + hd])
        return carry

    lax.fori_loop(0, tm // c_len, phase_b, 0)

    @pl.when(i == last)
    def _():
        sout_ref[0] = s_scr[...]


def _gdn_prompt(qkv, ba, z, l, cw, alog, dtb, gn, conv0, s0, tm):
    n_seq = s0.shape[0]
    t = qkv.shape[0]
    nt = t // n_seq // tm
    row = lambda n: pl.BlockSpec((tm, n), lambda b, i: (b * nt + i, 0))
    lay = lambda a: _layer_spec(a, l)
    hd = GDN_HEAD_DIM
    return pl.pallas_call(
        functools.partial(_gdn_prompt_kernel, tm=tm),
        grid=(n_seq, nt),
        in_specs=[row(QKV_WIDTH), row(LANES), row(GDN_WIDTH), lay(cw), lay(alog), lay(dtb), lay(gn),
                  pl.BlockSpec((1, SUBLANES, QKV_WIDTH), lambda b, i: (b, 0, 0)),
                  pl.BlockSpec((1, GDN_HEADS, hd, hd), lambda b, i: (b, 0, 0, 0))],
        out_specs=[row(GDN_WIDTH),
                   pl.BlockSpec((1, GDN_HEADS, hd, hd), lambda b, i: (b, 0, 0, 0)),
                   pl.BlockSpec((1, SUBLANES, QKV_WIDTH), lambda b, i: (b, 0, 0))],
        out_shape=[jax.ShapeDtypeStruct((t, GDN_WIDTH), F32),
                   jax.ShapeDtypeStruct((n_seq, GDN_HEADS, hd, hd), F32),
                   jax.ShapeDtypeStruct((n_seq, SUBLANES, QKV_WIDTH), F32)],
        scratch_shapes=[pltpu.VMEM((tm + SUBLANES, QKV_WIDTH), F32), pltpu.VMEM((tm, QKV_WIDTH), F32),
                        pltpu.VMEM((tm, LANES), F32), pltpu.VMEM((tm, LANES), F32),
                        pltpu.VMEM((GDN_HEADS, hd, hd), F32),
                        pltpu.VMEM((tm // GDN_CHUNK, GDN_HEADS, 2 * GDN_CHUNK, hd), BF16),
                        pltpu.VMEM((tm // GDN_CHUNK, GDN_HEADS, 2 * GDN_CHUNK, hd), BF16),
                        pltpu.VMEM((tm // GDN_CHUNK, GDN_HEADS, GDN_CHUNK, hd), F32),
                        pltpu.VMEM((tm // GDN_CHUNK, SUBLANES, LANES), F32)],
        compiler_params=_cparams(("arbitrary", "arbitrary")),
        name="gdn_prompt",
    )(qkv, ba, z, cw, alog, dtb, gn, conv0, s0)


def _gdn_step_kernel(qkv_ref, ba_ref, z_ref, cw_ref, alog_ref, dtb_ref, gn_ref, conv_ref, s_ref,
                     o_ref, sout_ref, convout_ref):
    nb = SUBLANES
    hd = GDN_HEAD_DIM
    x = qkv_ref[...]
    cb = conv_ref[...]
    cw = cw_ref[...]
    b0 = cb[:, 0:QKV_WIDTH]
    b1 = cb[:, QKV_WIDTH:2 * QKV_WIDTH]
    b2 = cb[:, 2 * QKV_WIDTH:3 * QKV_WIDTH]
    y = b0 * cw[0:1, :]
    y = y + b1 * cw[1:2, :]
    y = y + b2 * cw[2:3, :]
    y = y + x * cw[3:4, :]
    y = _silu(y)
    convout_ref[...] = jnp.concatenate([b1, b2, x], axis=1)

    beta, g = _gdn_gates(ba_ref[...], alog_ref[...], dtb_ref[...])
    a = jnp.exp(g)
    gn = gn_ref[...]
    zpad = jnp.zeros((hd - nb, hd), F32)
    for h in range(GDN_HEADS):
        lo = h * hd
        q = _l2n(y[:, lo:lo + hd]) * (hd ** -0.5)
        k = _l2n(y[:, GDN_WIDTH + lo:GDN_WIDTH + lo + hd])
        v = y[:, 2 * GDN_WIDTH + lo:2 * GDN_WIDTH + lo + hd]
        kt = jnp.concatenate([k, zpad], axis=0).T
        qt = jnp.concatenate([q, zpad], axis=0).T
        kq = jnp.sum(k * q, axis=-1, keepdims=True)
        bh = beta[:, h:h + 1]
        ah = a[:, GDN_HEADS + h:GDN_HEADS + h + 1]
        o_rows = []
        for n in range(nb):
            s = s_ref[n, h]
            kc = kt[:, n:n + 1]
            qc = qt[:, n:n + 1]
            rk = jnp.sum(s * kc, axis=0, keepdims=True)
            rq = jnp.sum(s * qc, axis=0, keepdims=True)
            an = ah[n:n + 1, :]
            un = bh[n:n + 1, :] * (v[n:n + 1, :] - an * rk)
            sout_ref[n, h] = an * s + kc * un
            o_rows.append(an * rq + kq[n:n + 1, :] * un)
        o = jnp.concatenate(o_rows, axis=0)
        o_ref[:, lo:lo + hd] = _gated_norm(o, gn, z_ref[:, lo:lo + hd])


def _gdn_step(qkv, ba, z, l, cw, alog, dtb, gn, conv, s):
    n_seq = qkv.shape[0]
    nb = SUBLANES
    hd = GDN_HEAD_DIM
    row = lambda n: pl.BlockSpec((nb, n), lambda i: (i, 0))
    lay = lambda a: _layer_spec(a, l)
    sspec = pl.BlockSpec((nb, GDN_HEADS, hd, hd), lambda i: (i, 0, 0, 0))
    return pl.pallas_call(
        _gdn_step_kernel,
        grid=(n_seq // nb,),
        in_specs=[row(QKV_WIDTH), row(LANES), row(GDN_WIDTH), lay(cw), lay(alog), lay(dtb), lay(gn),
                  pl.BlockSpec((None, nb, 3 * QKV_WIDTH), lambda i: (l, i, 0)),
                  pl.BlockSpec((None, nb, GDN_HEADS, hd, hd), lambda i: (l, i, 0, 0, 0))],
        out_specs=[row(GDN_WIDTH), sspec, row(3 * QKV_WIDTH)],
        out_shape=[jax.ShapeDtypeStruct((n_seq, GDN_WIDTH), F32),
                   jax.ShapeDtypeStruct((n_seq, GDN_HEADS, hd, hd), F32),
                   jax.ShapeDtypeStruct((n_seq, 3 * QKV_WIDTH), F32)],
        compiler_params=_cparams(("arbitrary",)),
        name="gdn_step",
    )(qkv, ba, z, cw, alog, dtb, gn, conv, s)


def _s5_param_kernel(lre_ref, lim_ref, ldt_ref, bre_ref, bim_ref, are_ref, aim_ref, bbre_ref, bbim_ref):
    lre = lre_ref[...]
    lim = lim_ref[...]
    dt = jnp.exp(ldt_ref[...])
    mag = jnp.exp(lre * dt)
    are = mag * jnp.cos(lim * dt)
    aim = mag * jnp.sin(lim * dt)
    are_ref[...] = are
    aim_ref[...] = aim
    nre = are - 1.0
    den = lre * lre + lim * lim
    cre = (nre * lre + aim * lim) / den
    cim = (aim * lre - nre * lim) / den
    cre = cre[:, None, :]
    cim = cim[:, None, :]
    bre = bre_ref[...]
    bim = bim_ref[...]
    bbre_ref[...] = cre * bre - cim * bim
    bbim_ref[...] = cre * bim + cim * bre


def _s5_params(lam_re, lam_im, log_dt, b_re, b_im):
    p = lam_re.shape[-1]
    lam_re = lam_re.reshape(-1, p)
    lam_im = lam_im.reshape(-1, p)
    g = lam_re.shape[0]
    bt_re = jnp.swapaxes(b_re.reshape((g,) + b_re.shape[2:]), 1, 2)
    bt_im = jnp.swapaxes(b_im.reshape((g,) + b_im.shape[2:]), 1, 2)
    cg = bt_re.shape[1]
    return pl.pallas_call(
        _s5_param_kernel,
        out_shape=[jax.ShapeDtypeStruct((g, p), F32), jax.ShapeDtypeStruct((g, p), F32),
                   jax.ShapeDtypeStruct((g, cg, p), F32), jax.ShapeDtypeStruct((g, cg, p), F32)],
        name="s5_params",
    )(lam_re, lam_im, log_dt.reshape(g, 1), bt_re, bt_im)


def _slab_blockdiag(m):
    g, cg, p = m.shape
    gl = S5_GROUPS // S5_SLABS
    m4 = m.reshape(g // gl, gl, cg, p)
    eye = jnp.eye(gl, dtype=m.dtype)
    return jnp.einsum('igcp,gh->igchp', m4, eye).reshape(g // S5_GROUPS, S5_SLABS, gl * cg, gl * p)


def _slab_vec(v):
    g, p = v.shape
    return v.reshape(g // S5_GROUPS, S5_SLABS, (S5_GROUPS // S5_SLABS) * p)


def _gelu_tanh(x):
    return 0.5 * x * (1.0 + jnp.tanh(math.sqrt(2.0 / math.pi) * (x + 0.044715 * (x * x * x))))


def _s5_kernel(u_ref, x0_ref, a_ref, bmat_ref, cre_ref, cim_ref, d_ref, gluw_ref, glub_ref, ng_ref,
               o_ref, xout_ref, utb_scr, xs_scr, x_scr, y_scr, ab_scr, *, nb, tt, interleave):
    rows = nb * tt
    ss = S5_SLAB_STATE
    i = pl.program_id(0)

    @pl.when(i == 0)
    def _():
        x_scr[...] = x0_ref[...]
        ab_scr[...] = jnp.broadcast_to(a_ref[...], ab_scr.shape)

    if interleave:
        for b in range(nb):
            for s in range(S5_SLABS):
                utb_scr[pl.ds(s * rows + b, tt, stride=nb), :] = u_ref[b, :, s * LANES:(s + 1) * LANES]
    else:
        for s in range(S5_SLABS):
            utb_scr[s * rows:(s + 1) * rows, :] = u_ref[:, s * LANES:(s + 1) * LANES]

    for s in range(S5_SLABS):
        xs_scr[:, s * 2 * ss:(s + 1) * 2 * ss] = jnp.dot(utb_scr[s * rows:(s + 1) * rows, :].astype(BF16),
                                                       bmat_ref[s], preferred_element_type=F32)

    def step(t, carry):
        r0 = pl.multiple_of(t * nb, nb)
        for s in range(S5_SLABS):
            lo = s * 2 * ss
            ar = ab_scr[:, lo:lo + ss]
            ai = ab_scr[:, lo + ss:lo + 2 * ss]
            xr = x_scr[:, lo:lo + ss]
            xi = x_scr[:, lo + ss:lo + 2 * ss]
            nr = (ar * xr - ai * xi) + xs_scr[pl.ds(r0, nb), lo:lo + ss]
            ni = (ar * xi + ai * xr) + xs_scr[pl.ds(r0, nb), lo + ss:lo + 2 * ss]
            x_scr[:, lo:lo + ss] = nr
            x_scr[:, lo + ss:lo + 2 * ss] = ni
            xs_scr[pl.ds(r0, nb), lo:lo + ss] = nr
            xs_scr[pl.ds(r0, nb), lo + ss:lo + 2 * ss] = ni
        return carry

    lax.fori_loop(0, tt, step, 0, unroll=min(tt, 4))

    @pl.when(i == pl.num_programs(0) - 1)
    def _():
        xout_ref[...] = x_scr[...]

    ys = []
    for s in range(S5_SLABS):
        lo = s * 2 * ss
        yr = jnp.dot(xs_scr[:, lo:lo + ss].astype(BF16), cre_ref[s], preferred_element_type=F32)
        yi = jnp.dot(xs_scr[:, lo + ss:lo + 2 * ss].astype(BF16), cim_ref[s], preferred_element_type=F32)
        ys.append((yr - yi) + d_ref[0:1, s * LANES:(s + 1) * LANES] * utb_scr[s * rows:(s + 1) * rows, :])
    y = _gelu_tanh(jnp.concatenate(ys, axis=1))
    y = y * _sigmoid(_mm(y, gluw_ref[...]) + glub_ref[...])
    y = y * lax.rsqrt(jnp.mean(y * y, axis=-1, keepdims=True) + NORM_EPS) * ng_ref[...]
    if interleave:
        for s in range(S5_SLABS):
            y_scr[s * rows:(s + 1) * rows, :] = y[:, s * LANES:(s + 1) * LANES]
        for b in range(nb):
            for s in range(S5_SLABS):
                o_ref[b, :, s * LANES:(s + 1) * LANES] = y_scr[pl.ds(s * rows + b, tt, stride=nb), :]
    else:
        o_ref[...] = y


def _s5(u, x0, l, a, bmat, cre, cim, d, gluw, glub, ng, nb, tt, interleave):
    rows = nb * tt
    nstate = x0.shape[1]
    full = lambda arr: pl.BlockSpec(arr.shape, lambda i: (0,) * arr.ndim)
    lay = lambda arr: _layer_spec(arr, l)
    if interleave:
        steps = u.shape[1] // tt
        uspec = pl.BlockSpec((nb, tt, S5_WIDTH), lambda i: (0, i, 0))
        oshape = jax.ShapeDtypeStruct(u.shape, F32)
    else:
        steps = 1
        uspec = pl.BlockSpec((nb, S5_WIDTH), lambda i: (0, 0))
        oshape = jax.ShapeDtypeStruct(u.shape, F32)
    return pl.pallas_call(
        functools.partial(_s5_kernel, nb=nb, tt=tt, interleave=interleave),
        grid=(steps,),
        in_specs=[uspec, full(x0), lay(a), lay(bmat), lay(cre), lay(cim), lay(d), lay(gluw), lay(glub),
                  lay(ng)],
        out_specs=[uspec, full(x0)],
        out_shape=[oshape, jax.ShapeDtypeStruct(x0.shape, F32)],
        scratch_shapes=[pltpu.VMEM((S5_SLABS * rows, LANES), F32), pltpu.VMEM((rows, nstate), F32),
                        pltpu.VMEM((nb, nstate), F32), pltpu.VMEM((S5_SLABS * rows, LANES), F32),
                        pltpu.VMEM((nb, nstate), F32)],
        compiler_params=_cparams(("arbitrary",)),
        name="s5_scan",
    )(u, x0, a, bmat, cre, cim, d, gluw, glub, ng)


def _router(logits_t, bias_col):
    scores = _sigmoid(logits_t)
    sel = scores + bias_col
    s = [sel[e:e + 1, :] for e in range(N_EXPERTS)]
    n_groups = N_EXPERTS // EXPERTS_PER_GROUP
    gs = []
    for gi in range(n_groups):
        m = s[gi * EXPERTS_PER_GROUP: (gi + 1) * EXPERTS_PER_GROUP]
        best = None
        for p in range(EXPERTS_PER_GROUP):
            for q in range(p + 1, EXPERTS_PER_GROUP):
                ps = m[p] + m[q]
                best = ps if best is None else jnp.maximum(best, ps)
        gs.append(best)
    gmax = functools.reduce(jnp.maximum, gs)
    taken = None
    in_best = []
    for gi in range(n_groups):
        hit = gs[gi] == gmax
        if taken is None:
            cur = hit
            taken = hit
        else:
            cur = jnp.logical_and(hit, jnp.logical_not(taken))
            taken = jnp.logical_or(taken, hit)
        in_best.append(cur)
    selm = []
    picked = []
    for e in range(N_EXPERTS):
        gi = e // EXPERTS_PER_GROUP
        cnt = jnp.zeros_like(s[e])
        for j in range(gi * EXPERTS_PER_GROUP, (gi + 1) * EXPERTS_PER_GROUP):
            if j == e:
                continue
            beats = (s[j] >= s[e]) if j < e else (s[j] > s[e])
            cnt = cnt + jnp.where(beats, 1.0, 0.0)
        sel_e = jnp.logical_and(in_best[gi], cnt < 1.5)
        selm.append(jnp.where(sel_e, 1.0, 0.0))
        picked.append(jnp.where(sel_e, scores[e:e + 1, :], 0.0))
    denom = functools.reduce(lambda x, y: x + y, picked)
    comb = [p / denom for p in picked]
    return comb, selm, in_best


def _router_dense(logits_t, bias_col):
    comb, _, _ = _router(logits_t, bias_col)
    return jnp.concatenate(comb, axis=0)


def _router_sparse(logits_t, bias_col):
    comb, selm, in_best = _router(logits_t, bias_col)
    cls = None
    wlo = None
    whi = None
    for gi in range(N_EXPERTS // EXPERTS_PER_GROUP):
        term = jnp.where(in_best[gi], float(len(PAIR_CODES) * gi) - 1.0, 0.0)
        cls = term if cls is None else cls + term
        seen = None
        for j in range(EXPERTS_PER_GROUP):
            e = gi * EXPERTS_PER_GROUP + j
            cls = cls + selm[e] * PAIR_VALUE[j]
            first = selm[e] if seen is None else selm[e] * (1.0 - seen)
            seen = selm[e] if seen is None else jnp.maximum(seen, selm[e])
            lo_term = first * comb[e]
            hi_term = (selm[e] - first) * comb[e]
            wlo = lo_term if wlo is None else wlo + lo_term
            whi = hi_term if whi is None else whi + hi_term
    return cls, wlo, whi


def _outproj_kernel(oa_ref, ob_ref, x_ref, gt_ref, sc_ref, sh_ref, g_ref, wout_ref, wrt_ref, br_ref,
                    xo_ref, hn_ref, comb_ref, *, tm):
    o = jnp.concatenate([oa_ref[...].astype(BF16), ob_ref[...].astype(BF16)], axis=1)
    mix = jnp.dot(o, wout_ref[...], preferred_element_type=F32)
    x = x_ref[...] + gt_ref[...] * mix
    xo_ref[...] = x
    ms = jnp.mean(x * x, axis=-1, keepdims=True)
    hn = x * lax.rsqrt(ms + NORM_EPS) * g_ref[...]
    hn = hn * (1.0 + sc_ref[...]) + sh_ref[...]
    hb = hn.astype(BF16)
    hn_ref[...] = hb
    logits_t = lax.dot_general(wrt_ref[...], hb, (((1,), (1,)), ((), ())), preferred_element_type=F32)
    comb_t = _router_dense(logits_t, br_ref[...])
    pad = jnp.zeros((LANES - N_EXPERTS, tm), F32)
    comb_ref[...] = jnp.concatenate([comb_t, pad], axis=0).T


def _outproj(oa, ob, x, mod, l, g, wout, wrt, br, tm, rows_per_seq):
    t = x.shape[0]
    full = lambda a: pl.BlockSpec(a.shape, lambda i: (0,) * a.ndim)
    lay = lambda a: _layer_spec(a, l)
    row = lambda n: pl.BlockSpec((tm, n), lambda i: (i, 0))
    ms = lambda j: _mod_spec(mod, l, j, tm, rows_per_seq)
    return pl.pallas_call(
        functools.partial(_outproj_kernel, tm=tm),
        grid=(t // tm,),
        in_specs=[row(GDN_WIDTH), row(S5_WIDTH), row(D_MODEL), ms(2), ms(4), ms(3), lay(g), lay(wout),
                  full(wrt), full(br)],
        out_specs=[row(D_MODEL), row(D_MODEL), row(LANES)],
        out_shape=[jax.ShapeDtypeStruct((t, D_MODEL), F32), jax.ShapeDtypeStruct((t, D_MODEL), BF16),
                   jax.ShapeDtypeStruct((t, LANES), F32)],
        compiler_params=_cparams(("arbitrary",)),
        name="outproj_router",
    )(oa, ob, x, mod, mod, mod, g, wout, wrt, br)


def _outproj_sparse_kernel(oa_ref, ob_ref, x_ref, gt_ref, sc_ref, sh_ref, g_ref, wout_ref, wrt_ref, br_ref,
                           xo_ref, hx_ref, route_ref, cnt_ref, run_scr, *, tm):
    i = pl.program_id(0)

    @pl.when(i == 0)
    def _():
        run_scr[...] = jnp.zeros_like(run_scr)

    o = jnp.concatenate([oa_ref[...].astype(BF16), ob_ref[...].astype(BF16)], axis=1)
    mix = jnp.dot(o, wout_ref[...], preferred_element_type=F32)
    x = x_ref[...] + gt_ref[...] * mix
    xo_ref[...] = x
    ms = jnp.mean(x * x, axis=-1, keepdims=True)
    hn = x * lax.rsqrt(ms + NORM_EPS) * g_ref[...]
    hn = hn * (1.0 + sc_ref[...]) + sh_ref[...]
    hb = hn.astype(BF16)
    logits_t = lax.dot_general(wrt_ref[...], hb, (((1,), (1,)), ((), ())), preferred_element_type=F32)
    cls, wlo, whi = _router_sparse(logits_t, br_ref[...])
    sub = lax.broadcasted_iota(jnp.int32, (CLASS_ROWS, tm), 0).astype(F32)
    onehot = sub == cls
    r = lax.broadcasted_iota(jnp.int32, (tm, tm), 0)
    c = lax.broadcasted_iota(jnp.int32, (tm, tm), 1)
    before = jnp.where(r < c, 1.0, 0.0).astype(BF16)
    prefix = jnp.dot(jnp.where(onehot, 1.0, 0.0).astype(BF16), before, preferred_element_type=F32)
    run = run_scr[...]
    rank = jnp.sum(jnp.where(onehot, prefix + run[:, 0:1], 0.0), axis=0, keepdims=True)
    run_scr[...] = run + jnp.sum(jnp.where(onehot, 1.0, 0.0), axis=1, keepdims=True)
    cnt_ref[...] = run_scr[...]
    route = jnp.concatenate([cls, rank, wlo, whi, jnp.zeros((SUBLANES - 4, tm), F32)], axis=0)
    route_ref[...] = route
    hx_ref[:, :D_MODEL] = hn
    hx_ref[:, D_MODEL:] = jnp.concatenate([route, jnp.zeros((LANES - SUBLANES, tm), F32)], axis=0).T


def _outproj_sparse(oa, ob, x, mod, l, g, wout, wrt, br, tm, rows_per_seq):
    t = x.shape[0]
    full = lambda a: pl.BlockSpec(a.shape, lambda i: (0,) * a.ndim)
    lay = lambda a: _layer_spec(a, l)
    row = lambda n: pl.BlockSpec((tm, n), lambda i: (i, 0))
    ms = lambda j: _mod_spec(mod, l, j, tm, rows_per_seq)
    return pl.pallas_call(
        functools.partial(_outproj_sparse_kernel, tm=tm),
        grid=(t // tm,),
        in_specs=[row(GDN_WIDTH), row(S5_WIDTH), row(D_MODEL), ms(2), ms(4), ms(3), lay(g), lay(wout),
                  full(wrt), full(br)],
        out_specs=[row(D_MODEL), row(HX_WIDTH), pl.BlockSpec((SUBLANES, tm), lambda i: (0, i)),
                   pl.BlockSpec((CLASS_ROWS, LANES), lambda i: (0, 0))],
        out_shape=[jax.ShapeDtypeStruct((t, D_MODEL), F32), jax.ShapeDtypeStruct((t, HX_WIDTH), F32),
                   jax.ShapeDtypeStruct((SUBLANES, t), F32), jax.ShapeDtypeStruct((CLASS_ROWS, LANES), F32)],
        scratch_shapes=[pltpu.VMEM((CLASS_ROWS, LANES), F32)],
        compiler_params=_cparams(("arbitrary",)),
        name="outproj_route",
    )(oa, ob, x, mod, mod, mod, g, wout, wrt, br)


def _dispatch_kernel(pos_ref, tail_ref, hx_ref, sorted_ref, stage, zero_scr, sems, zsem, *, tm, n_steps):
    i = pl.program_id(0)
    slot = i % 2

    def tail_copy(cidx):
        start = pl.multiple_of(jnp.maximum(tail_ref[cidx], 0), MOE_TILE)
        return pltpu.make_async_copy(zero_scr, sorted_ref.at[pl.ds(start, MOE_TILE), :], zsem)

    @pl.when(i == 0)
    def _():
        zero_scr[...] = jnp.zeros_like(zero_scr)
        for cidx in range(2 * N_CLASSES):
            @pl.when(tail_ref[cidx] >= 0)
            def _():
                tail_copy(cidx).start()
        for cidx in range(2 * N_CLASSES):
            @pl.when(tail_ref[cidx] >= 0)
            def _():
                tail_copy(cidx).wait()

    def row_copy(s, r, p):
        return pltpu.make_async_copy(stage.at[s, pl.ds(r, 1), :], sorted_ref.at[pl.ds(p, 1), :], sems.at[s])

    def wait_slot(s):
        pltpu.make_async_copy(stage.at[s], sorted_ref.at[pl.ds(0, tm), :], sems.at[s]).wait()

    @pl.when(i >= 2)
    def _():
        wait_slot(slot)

    stage[slot] = hx_ref[...]
    base = i * tm

    def issue(g, carry):
        r8 = pl.multiple_of(g * SUBLANES, SUBLANES)
        for k in range(SUBLANES):
            row_copy(slot, r8 + k, pos_ref[base + r8 + k]).start(priority=k % 2)
        return carry

    lax.fori_loop(0, tm // SUBLANES, issue, 0)

    @pl.when(i == n_steps - 1)
    def _():
        wait_slot(slot)
        if n_steps > 1:
            wait_slot(1 - slot)


def _dispatch(pos, tail, hx, n_rows, tm):
    t = hx.shape[0]
    n_steps = t // tm
    return pl.pallas_call(
        functools.partial(_dispatch_kernel, tm=tm, n_steps=n_steps),
        grid_spec=pltpu.PrefetchScalarGridSpec(
            num_scalar_prefetch=2,
            grid=(n_steps,),
            in_specs=[pl.BlockSpec((tm, HX_WIDTH), lambda i, pos, tail: (i, 0))],
            out_specs=pl.BlockSpec(memory_space=pl.ANY),
            scratch_shapes=[pltpu.VMEM((2, tm, HX_WIDTH), F32), pltpu.VMEM((MOE_TILE, HX_WIDTH), F32),
                            pltpu.SemaphoreType.DMA((2,)), pltpu.SemaphoreType.DMA(())]),
        out_shape=jax.ShapeDtypeStruct((n_rows, HX_WIDTH), F32),
        compiler_params=_cparams(("arbitrary",)),
        name="moe_dispatch",
    )(pos, tail, hx)


def _moe_sorted_kernel(lo_ref, hi_ref, src_ref, nv_ref, xs_ref, wg_lo_ref, wu_lo_ref, wd_lo_ref, wg_hi_ref,
                       wu_hi_ref, wd_hi_ref, o_ref):
    j = pl.program_id(0)

    @pl.when(j < nv_ref[0])
    def _():
        xs = xs_ref[...]
        x = xs[:, :D_MODEL].astype(BF16)
        wlo = xs[:, D_MODEL + 2:D_MODEL + 3]
        whi = xs[:, D_MODEL + 3:D_MODEL + 4]
        dot = functools.partial(jnp.dot, preferred_element_type=F32)
        a_lo = _silu(dot(x, wg_lo_ref[...])) * dot(x, wu_lo_ref[...]) * wlo
        a_hi = _silu(dot(x, wg_hi_ref[...])) * dot(x, wu_hi_ref[...]) * whi
        o_ref[...] = dot(a_lo.astype(BF16), wd_lo_ref[...]) + dot(a_hi.astype(BF16), wd_hi_ref[...])

    @pl.when(j >= nv_ref[0])
    def _():
        o_ref[...] = jnp.zeros_like(o_ref)


def _moe_sorted(tile_lo, tile_hi, tile_src, n_valid, xs, l, wg, wu, wd):
    n_rows = xs.shape[0]
    n_tiles = n_rows // MOE_TILE
    wspec = lambda w, which: pl.BlockSpec(
        (None, None) + w.shape[2:], lambda j, lo, hi, src, nv: (l, (lo, hi)[which][j], 0, 0))
    return pl.pallas_call(
        _moe_sorted_kernel,
        grid_spec=pltpu.PrefetchScalarGridSpec(
            num_scalar_prefetch=4,
            grid=(n_tiles,),
            in_specs=[pl.BlockSpec((MOE_TILE, HX_WIDTH), lambda j, lo, hi, src, nv: (src[j], 0)),
                      wspec(wg, 0), wspec(wu, 0), wspec(wd, 0), wspec(wg, 1), wspec(wu, 1), wspec(wd, 1)],
            out_specs=pl.BlockSpec((MOE_TILE, D_MODEL), lambda j, lo, hi, src, nv: (j, 0))),
        out_shape=jax.ShapeDtypeStruct((n_rows, D_MODEL), F32),
        compiler_params=_cparams(("arbitrary",)),
        name="moe_sorted",
    )(tile_lo, tile_hi, tile_src, n_valid, xs, wg, wu, wd, wg, wu, wd)


def _combine_kernel(pos_ref, ys_ref, x_ref, gt_ref, fg_ref, o_ref, buf, sems, *, tm, n_steps, final_norm):
    i = pl.program_id(0)
    slot = i % 2

    def row_copy(s, r, p):
        return pltpu.make_async_copy(ys_ref.at[pl.ds(p, 1), :], buf.at[s, pl.ds(r, 1), :], sems.at[s])

    def issue(step, s):
        base = step * tm

        def body(g, carry):
            r8 = pl.multiple_of(g * SUBLANES, SUBLANES)
            for k in range(SUBLANES):
                row_copy(s, r8 + k, pos_ref[base + r8 + k]).start(priority=k % 2)
            return carry

        lax.fori_loop(0, tm // SUBLANES, body, 0)

    @pl.when(i == 0)
    def _():
        issue(0, 0)

    @pl.when(i + 1 < n_steps)
    def _():
        issue(i + 1, 1 - slot)

    pltpu.make_async_copy(ys_ref.at[pl.ds(0, tm), :], buf.at[slot], sems.at[slot]).wait()
    x = x_ref[...] + gt_ref[...] * buf[slot]
    if final_norm:
        x = x * lax.rsqrt(jnp.mean(x * x, axis=-1, keepdims=True) + NORM_EPS) * fg_ref[...]
    o_ref[...] = x


def _combine(pos, ys, x, mod, l, fg, tm, rows_per_seq, final_norm):
    t = x.shape[0]
    n_steps = t // tm
    return pl.pallas_call(
        functools.partial(_combine_kernel, tm=tm, n_steps=n_steps, final_norm=final_norm),
        grid_spec=pltpu.PrefetchScalarGridSpec(
            num_scalar_prefetch=1,
            grid=(n_steps,),
            in_specs=[pl.BlockSpec(memory_space=pl.ANY),
                      pl.BlockSpec((tm, D_MODEL), lambda i, pos: (i, 0)),
                      _mod_spec(mod, l, 5, tm, rows_per_seq),
                      pl.BlockSpec(fg.shape, lambda i, pos: (0, 0))],
            out_specs=pl.BlockSpec((tm, D_MODEL), lambda i, pos: (i, 0)),
            scratch_shapes=[pltpu.VMEM((2, tm, D_MODEL), F32), pltpu.SemaphoreType.DMA((2,))]),
        out_shape=jax.ShapeDtypeStruct((t, D_MODEL), F32),
        compiler_params=_cparams(("arbitrary",)),
        name="moe_combine",
    )(pos, ys, x, mod, fg)


def _route_plan(route, counts, n_tokens):
    cnt = counts[:N_CLASSES, 0].astype(jnp.int32)
    padded = ((cnt + MOE_TILE - 1) // MOE_TILE) * MOE_TILE
    ends = jnp.cumsum(padded)
    offsets = ends - padded
    cls = route[0].astype(jnp.int32)
    pos = route[1].astype(jnp.int32) + jnp.sum(
        jnp.where(cls[:, None] == jnp.arange(N_CLASSES, dtype=jnp.int32)[None, :], offsets[None, :], 0), axis=1)
    n_tiles = n_tokens // MOE_TILE + N_CLASSES
    n_valid = ends[-1] // MOE_TILE
    src = jnp.minimum(jnp.arange(n_tiles, dtype=jnp.int32), jnp.maximum(n_valid - 1, 0))
    tile_cls = jnp.minimum(jnp.sum((src[:, None] * MOE_TILE >= ends[None, :]).astype(jnp.int32), axis=1),
                           N_CLASSES - 1)
    group = tile_cls // len(PAIR_CODES)
    pair = jnp.asarray(PAIR_CODES, dtype=jnp.int32)[tile_cls % len(PAIR_CODES)]
    tile_lo = group * EXPERTS_PER_GROUP + pair[:, 0]
    tile_hi = group * EXPERTS_PER_GROUP + pair[:, 1]
    unused = (n_valid + jnp.arange(N_CLASSES, dtype=jnp.int32)) * MOE_TILE
    tail = jnp.concatenate([jnp.where(cnt > 0, ends - MOE_TILE, -1),
                            jnp.where(unused < n_tiles * MOE_TILE, unused, -1)]).astype(jnp.int32)
    return pos, tail, tile_lo, tile_hi, src, n_valid.reshape(1).astype(jnp.int32), n_tiles * MOE_TILE


def _moe_kernel(hn_ref, comb_ref, x_ref, gt_ref, wg_ref, wu_ref, wd_ref, fg_ref, o_ref, acc_scr, *, final_norm):
    e = pl.program_id(1)

    @pl.when(e == 0)
    def _():
        acc_scr[...] = jnp.zeros_like(acc_scr)

    hg = jnp.dot(hn_ref[...], wg_ref[...], preferred_element_type=F32)
    hu = jnp.dot(hn_ref[...], wu_ref[...], preferred_element_type=F32)
    lane = lax.broadcasted_iota(jnp.int32, (1, LANES), 1)
    ce = jnp.sum(jnp.where(lane == e, comb_ref[...], 0.0), axis=-1, keepdims=True)
    act = _silu(hg) * hu * ce
    acc_scr[...] += jnp.dot(act.astype(BF16), wd_ref[...], preferred_element_type=F32)

    @pl.when(e == pl.num_programs(1) - 1)
    def _():
        x = x_ref[...] + gt_ref[...] * acc_scr[...]
        if final_norm:
            x = x * lax.rsqrt(jnp.mean(x * x, axis=-1, keepdims=True) + NORM_EPS) * fg_ref[...]
        o_ref[...] = x


def _moe(hn, comb, x, mod, l, wg, wu, wd, fg, tm, rows_per_seq, final_norm):
    t = x.shape[0]
    row = lambda n: pl.BlockSpec((tm, n), lambda i, e: (i, 0))
    wspec = lambda w: pl.BlockSpec((None, None) + w.shape[2:], lambda i, e: (l, e, 0, 0))
    return pl.pallas_call(
        functools.partial(_moe_kernel, final_norm=final_norm),
        grid=(t // tm, N_EXPERTS),
        in_specs=[row(D_MODEL), row(LANES), row(D_MODEL), _mod_spec(mod, l, 5, tm, rows_per_seq),
                  wspec(wg), wspec(wu), wspec(wd),
                  pl.BlockSpec(fg.shape, lambda i, e: (0, 0))],
        out_specs=row(D_MODEL),
        out_shape=jax.ShapeDtypeStruct((t, D_MODEL), F32),
        scratch_shapes=[pltpu.VMEM((tm, D_MODEL), F32)],
        compiler_params=_cparams(("arbitrary", "arbitrary")),
        name="moe",
    )(hn, comb, x, mod, wg, wu, wd, fg)


def _pad_lanes(v, offset):
    return jnp.pad(v, ((0, 0), (offset, LANES - offset - v.shape[1])))[:, None, :]


def _state_to_slab(re, im):
    lead = re.shape[:2]
    r = re.reshape(lead + (S5_SLABS, S5_SLAB_STATE))
    i = im.reshape(lead + (S5_SLABS, S5_SLAB_STATE))
    return jnp.stack([r, i], axis=3).reshape(lead + (S5_SLABS * 2 * S5_SLAB_STATE,))


def _slab_to_state(x):
    lead = x.shape[:2]
    x4 = x.reshape(lead + (S5_SLABS, 2, S5_SLAB_STATE))
    re = x4[:, :, :, 0].reshape(lead + (S5_GROUPS, S5_STATE))
    im = x4[:, :, :, 1].reshape(lead + (S5_GROUPS, S5_STATE))
    return re, im


def kernel(x_prompt, x_sample, c_prompt, c_sample, state_conv, state_gdn, state_s5_re, state_s5_im, norm1_g, norm2_g, w_ada, b_ada, w_in, conv_w, a_log, dt_bias, gdn_norm_g, s5_lambda_re, s5_lambda_im, s5_log_dt, s5_b_re, s5_b_im, s5_c_re, s5_c_im, s5_d, s5_glu_w, s5_glu_b, s5_norm_g, w_out, w_router, b_router, w_gate, w_up, w_down, final_g):
    bp, seq, _ = x_prompt.shape
    bs = x_sample.shape[0]
    tp = bp * seq
    tm_p = 512
    tt = 64 if seq % 64 == 0 else seq

    mod = _ada(jnp.concatenate([c_prompt, c_sample], axis=0), w_ada, b_ada)
    mod_p = mod[:, :bp].reshape(DEPTH, bp, 1, 6 * D_MODEL)
    mod_s = mod[:, bp:]

    nba = 2 * GDN_HEADS
    u_start = QKV_WIDTH + GDN_WIDTH
    w_in_packed = jnp.concatenate(
        [w_in[:, :, :u_start], w_in[:, :, u_start + nba:], w_in[:, :, u_start:u_start + nba],
         jnp.zeros((DEPTH, D_MODEL, LANES - nba), F32)], axis=2).astype(BF16)
    g1 = norm1_g[:, None, :]
    g2 = norm2_g[:, None, :]
    alog = _pad_lanes(a_log, GDN_HEADS)
    dtb = _pad_lanes(dt_bias, GDN_HEADS)
    gn = gdn_norm_g[:, None, :]
    a_re, a_im, bb_re, bb_im = _s5_params(s5_lambda_re, s5_lambda_im, s5_log_dt, s5_b_re, s5_b_im)
    a_vec = jnp.concatenate([_slab_vec(a_re), _slab_vec(a_im)], axis=2).reshape(DEPTH, 1, -1)
    bmat = jnp.concatenate([_slab_blockdiag(bb_re), _slab_blockdiag(bb_im)], axis=3).astype(BF16)
    flat = lambda m: m.reshape((DEPTH * S5_GROUPS,) + m.shape[2:])
    cre = jnp.swapaxes(_slab_blockdiag(flat(s5_c_re)), 2, 3).astype(BF16)
    cim = jnp.swapaxes(_slab_blockdiag(flat(s5_c_im)), 2, 3).astype(BF16)
    dvec = s5_d[:, None, :]
    gluw = s5_glu_w.astype(BF16)
    glub = s5_glu_b[:, None, :]
    ng = s5_norm_g[:, None, :]
    wout = w_out.astype(BF16)
    wg = w_gate.astype(BF16)
    wu = w_up.astype(BF16)
    wd = w_down.astype(BF16)
    wrt = jnp.transpose(w_router).astype(BF16)
    br = b_router.reshape(N_EXPERTS, 1)
    fg = final_g.reshape(1, D_MODEL)
    conv_s = state_conv.reshape(DEPTH, bs, (CONV_WIDTH - 1) * QKV_WIDTH)

    xp = x_prompt.reshape(tp, D_MODEL)
    xs = x_sample.reshape(bs, D_MODEL)
    outs_p = {k: [] for k in ("conv", "gdn", "s5")}
    outs_s = {k: [] for k in ("conv", "gdn", "s5")}
    zero_conv = jnp.zeros((bp, SUBLANES, QKV_WIDTH), F32)
    zero_gdn = jnp.zeros((bp, GDN_HEADS, GDN_HEAD_DIM, GDN_HEAD_DIM), F32)
    zero_s5 = jnp.zeros((bp, S5_SLABS * 2 * S5_SLAB_STATE), F32)
    x0_s = _state_to_slab(state_s5_re, state_s5_im)

    for l in range(DEPTH):
        last = l == DEPTH - 1

        qkv, z, ba, u = _inproj(xp, mod_p, l, g1, w_in_packed, tm_p, seq)
        oa, sg, cv = _gdn_prompt(qkv, ba, z, l, conv_w, alog, dtb, gn, zero_conv, zero_gdn, tm_p)
        ob, xst = _s5(u.reshape(bp, seq, S5_WIDTH), zero_s5, l, a_vec, bmat, cre, cim, dvec, gluw, glub, ng,
                      bp, tt, True)
        xp, hx, route, counts = _outproj_sparse(oa, ob.reshape(tp, S5_WIDTH), xp, mod_p, l, g2, wout, wrt, br,
                                                tm_p, seq)
        pos, tail, tile_lo, tile_hi, tile_src, n_valid, n_rows = _route_plan(route, counts, tp)
        xsorted = _dispatch(pos, tail, hx, n_rows, tm_p)
        ys = _moe_sorted(tile_lo, tile_hi, tile_src, n_valid, xsorted, l, wg, wu, wd)
        xp = _combine(pos, ys, xp, mod_p, l, fg, tm_p, seq, last)
        outs_p["conv"].append(cv)
        outs_p["gdn"].append(sg)
        outs_p["s5"].append(xst)

        qkv, z, ba, u = _inproj(xs, mod_s, l, g1, w_in_packed, bs, 1)
        oa, sg, cv = _gdn_step(qkv, ba, z, l, conv_w, alog, dtb, gn, conv_s, state_gdn)
        ob, xst = _s5(u, x0_s[l], l, a_vec, bmat, cre, cim, dvec, gluw, glub, ng, bs, 1, False)
        xs, hn, comb = _outproj(oa, ob, xs, mod_s, l, g2, wout, wrt, br, bs, 1)
        xs = _moe(hn, comb, xs, mod_s, l, wg, wu, wd, fg, bs, 1, last)
        outs_s["conv"].append(cv)
        outs_s["gdn"].append(sg)
        outs_s["s5"].append(xst)

    st = lambda d, k: jnp.stack(d[k])
    re_p, im_p = _slab_to_state(st(outs_p, "s5"))
    re_s, im_s = _slab_to_state(st(outs_s, "s5"))
    conv_p = st(outs_p, "conv")[:, :, SUBLANES - (CONV_WIDTH - 1):, :]
    conv_s_new = st(outs_s, "conv").reshape(DEPTH, bs, CONV_WIDTH - 1, QKV_WIDTH)
    return (xp.reshape(bp, seq, D_MODEL), xs.reshape(bs, 1, D_MODEL),
            conv_p, st(outs_p, "gdn"), re_p, im_p, conv_s_new, st(outs_s, "gdn"), re_s, im_s)
```

```python
import functools
import math

import jax
import jax.numpy as jnp
from jax import lax
from jax.experimental import pallas as pl
from jax.experimental.pallas import tpu as pltpu

F32 = jnp.float32
BF16 = jnp.bfloat16

D_MODEL = 1024
DEPTH = 2
GDN_HEAD_DIM = 128
GDN_WIDTH = 512
GDN_HEADS = 4
CONV_WIDTH = 4
S5_CH_PER_GROUP = 16
S5_WIDTH = 512
S5_GROUPS = 32
S5_STATE = 64
QKV_WIDTH = 3 * GDN_WIDTH
N_EXPERTS = 16
EXPERTS_PER_GROUP = 4
D_EXPERT = 256
NORM_EPS = 1e-6

LANES = 128
SUBLANES = 8
GDN_CHUNK = 128
GDN_CHUNKS_PER_ITER = 2
S5_SLABS = 4
S5_SLAB_STATE = 512
MOE_TILE = 256
PAIR_VALUE = (0.0, 1.0, 2.0, 4.0)
PAIR_CODES = ((0, 1), (0, 2), (1, 2), (0, 3), (1, 3), (2, 3))
N_CLASSES = (N_EXPERTS // EXPERTS_PER_GROUP) * len(PAIR_CODES)
CLASS_ROWS = 32
HX_WIDTH = D_MODEL + LANES
VMEM_LIMIT = 56 * 1024 * 1024


def _cparams(sem):
    return pltpu.CompilerParams(dimension_semantics=sem, vmem_limit_bytes=VMEM_LIMIT)


def _sigmoid(x):
    return 1.0 / (1.0 + jnp.exp(-x))


def _silu(x):
    return x * _sigmoid(x)


def _softplus(x):
    return jnp.maximum(x, 0.0) + jnp.log1p(jnp.exp(-jnp.abs(x)))


def _mm(a, b):
    return jnp.dot(a.astype(BF16), b.astype(BF16), preferred_element_type=F32)


def _ada_kernel(c_ref, w_ref, b_ref, o_ref):
    c = c_ref[...]
    o_ref[0] = _mm(_silu(c), w_ref[0]) + b_ref[0]


def _ada(c_all, w_ada, b_ada):
    rows = c_all.shape[0]
    n_out = w_ada.shape[-1]
    tn = 1536
    return pl.pallas_call(
        _ada_kernel,
        grid=(DEPTH, n_out // tn),
        in_specs=[
            pl.BlockSpec((rows, D_MODEL), lambda l, j: (0, 0)),
            pl.BlockSpec((1, D_MODEL, tn), lambda l, j: (l, 0, j)),
            pl.BlockSpec((1, 1, tn), lambda l, j: (l, 0, j)),
        ],
        out_specs=pl.BlockSpec((1, rows, tn), lambda l, j: (l, 0, j)),
        out_shape=jax.ShapeDtypeStruct((DEPTH, rows, n_out), F32),
        compiler_params=_cparams(("arbitrary", "arbitrary")),
        name="ada_mod",
    )(c_all, w_ada, b_ada.reshape(DEPTH, 1, n_out))


def _inproj_kernel(x_ref, sc_ref, sh_ref, g_ref, wqkv_ref, wz_ref, wba_ref, wu_ref,
                   qkv_ref, z_ref, ba_ref, u_ref):
    x = x_ref[...]
    ms = jnp.mean(x * x, axis=-1, keepdims=True)
    hn = x * lax.rsqrt(ms + NORM_EPS) * g_ref[...]
    hn = hn * (1.0 + sc_ref[...]) + sh_ref[...]
    hb = hn.astype(BF16)
    qkv_ref[...] = jnp.dot(hb, wqkv_ref[...], preferred_element_type=F32)
    z_ref[...] = jnp.dot(hb, wz_ref[...], preferred_element_type=F32)
    ba_ref[...] = jnp.dot(hb, wba_ref[...], preferred_element_type=F32)
    u_ref[...] = jnp.dot(hb, wu_ref[...], preferred_element_type=F32)


def _mod_spec(mod, l, j, tm, rows_per_seq):
    if mod.ndim == 4:
        tiles_per_seq = rows_per_seq // tm
        return pl.BlockSpec((None, None, 1, D_MODEL), lambda i, *_: (l, i // tiles_per_seq, 0, j))
    return pl.BlockSpec((None, tm, D_MODEL), lambda i, *_: (l, i, j))


def _layer_spec(arr, l):
    return pl.BlockSpec((None,) + arr.shape[1:], lambda *_: (l,) + (0,) * (arr.ndim - 1))


def _inproj(x, mod, l, g, w_in_packed, tm, rows_per_seq):
    t = x.shape[0]
    row = lambda n: pl.BlockSpec((tm, n), lambda i: (i, 0))
    wcol = lambda width, start: pl.BlockSpec((None, D_MODEL, width), lambda i: (l, 0, start // width))
    u_start = QKV_WIDTH + GDN_WIDTH
    return pl.pallas_call(
        _inproj_kernel,
        grid=(t // tm,),
        in_specs=[row(D_MODEL), _mod_spec(mod, l, 1, tm, rows_per_seq), _mod_spec(mod, l, 0, tm, rows_per_seq),
                  _layer_spec(g, l), wcol(QKV_WIDTH, 0), wcol(GDN_WIDTH, QKV_WIDTH),
                  wcol(LANES, u_start + S5_WIDTH), wcol(S5_WIDTH, u_start)],
        out_specs=[row(QKV_WIDTH), row(GDN_WIDTH), row(LANES), row(S5_WIDTH)],
        out_shape=[jax.ShapeDtypeStruct((t, QKV_WIDTH), F32), jax.ShapeDtypeStruct((t, GDN_WIDTH), F32),
                   jax.ShapeDtypeStruct((t, LANES), F32), jax.ShapeDtypeStruct((t, S5_WIDTH), F32)],
        compiler_params=_cparams(("arbitrary",)),
        name="inproj",
    )(x, mod, mod, g, w_in_packed, w_in_packed, w_in_packed, w_in_packed)


def _gdn_gates(ba, alog, dtb):
    beta = _sigmoid(ba)
    g = -jnp.exp(alog) * _softplus(ba + dtb)
    return beta, g


def _l2n(x):
    return x * lax.rsqrt(jnp.sum(x * x, axis=-1, keepdims=True) + NORM_EPS)


def _gated_norm(o, gn, z):
    on = o * lax.rsqrt(jnp.mean(o * o, axis=-1, keepdims=True) + NORM_EPS) * gn
    return on * _silu(z)


def _gdn_prompt_kernel(qkv_ref, ba_ref, z_ref, cw_ref, alog_ref, dtb_ref, gn_ref, conv0_ref, s0_ref,
                       o_ref, sout_ref, convout_ref, xp_scr, y_scr, g_scr, b_scr, s_scr, wq_scr, ak_scr, u0_scr,
                       egl_scr, *, tm):
    c_len = GDN_CHUNK
    hd = GDN_HEAD_DIM
    pair_rows = GDN_CHUNKS_PER_ITER * c_len
    n_pairs = tm // pair_rows
    i = pl.program_id(1)
    last = pl.num_programs(1) - 1

    @pl.when(i == 0)
    def _():
        xp_scr[0:SUBLANES, :] = conv0_ref[0]
        s_scr[...] = s0_ref[0]

    xp_scr[SUBLANES:SUBLANES + tm, :] = qkv_ref[...]
    cw = cw_ref[...]
    cw_rows = [cw[j:j + 1, :].reshape(1, 1, QKV_WIDTH) for j in range(CONV_WIDTH)]
    sub = lax.broadcasted_iota(jnp.int32, (1, SUBLANES, QKV_WIDTH), 1)

    def conv_rows(r0, n_rows):
        x3 = xp_scr[r0:r0 + n_rows + SUBLANES, :].reshape(n_rows // SUBLANES + 1, SUBLANES, QKV_WIDTH)

        def delayed(s):
            rot = pltpu.roll(x3, s, axis=1)
            return jnp.where(sub >= s, rot[1:], rot[:-1])

        y = delayed(3) * cw_rows[0]
        y = y + delayed(2) * cw_rows[1]
        y = y + delayed(1) * cw_rows[2]
        y = y + x3[1:] * cw_rows[3]
        y_scr[r0:r0 + n_rows, :] = _silu(y).reshape(n_rows, QKV_WIDTH)

    beta, g = _gdn_gates(ba_ref[...], alog_ref[...], dtb_ref[...])
    b_scr[...] = beta
    g_scr[...] = g

    r = lax.broadcasted_iota(jnp.int32, (c_len, c_len), 0)
    c = lax.broadcasted_iota(jnp.int32, (c_len, c_len), 1)
    ge = r >= c
    gt = r > c
    tri = jnp.where(ge, 1.0, 0.0).astype(BF16)
    eye = jnp.where(r == c, 1.0, 0.0).astype(F32)
    blk16 = (r // 16) == (c // 16)
    pair_masks = [((r // (2 * s)) == (c // (2 * s))) & ((r // s) != (c // s)) for s in (16, 32, 64)]
    gn = gn_ref[...]
    scale = hd ** -0.5

    def prep_stages(p):
        chains = []

        def load():
            for ck in range(GDN_CHUNKS_PER_ITER):
                ci = p * GDN_CHUNKS_PER_ITER + ck
                rows = slice(ci * c_len, (ci + 1) * c_len)
                gch = g_scr[rows, :]
                bch = b_scr[rows, :]
                g1 = gch.astype(BF16)
                r1 = gch - g1.astype(F32)
                g2 = r1.astype(BF16)
                g3 = (r1 - g2.astype(F32)).astype(BF16)
                gcum = (jnp.dot(tri, g1, preferred_element_type=F32)
                        + jnp.dot(tri, g2, preferred_element_type=F32)
                        + jnp.dot(tri, g3, preferred_element_type=F32))
                gcum_t = gcum.T
                glast = gcum[c_len - 1:c_len, :]
                egl_scr[ci] = jnp.broadcast_to(jnp.exp(glast), (SUBLANES, LANES))
                for h in range(GDN_HEADS):
                    lo = h * hd
                    q = _l2n(y_scr[rows, lo:lo + hd]) * scale
                    k = _l2n(y_scr[rows, GDN_WIDTH + lo:GDN_WIDTH + lo + hd])
                    v = y_scr[rows, 2 * GDN_WIDTH + lo:2 * GDN_WIDTH + lo + hd]
                    gl = GDN_HEADS + h
                    gcb = jnp.broadcast_to(gcum[:, gl:gl + 1], (c_len, c_len))
                    bcol = jnp.broadcast_to(bch[:, h:h + 1], (c_len, c_len))
                    egc = jnp.exp(gcb)
                    kdf = jnp.exp(glast[:, gl:gl + 1] - gcb)
                    decay = jnp.where(ge, jnp.exp(gcb - gcum_t[gl:gl + 1, :]), 0.0)
                    kb = k.astype(BF16)
                    chains.append(dict(
                        ci=ci, h=h, decay=decay, bcol=bcol, kb=kb,
                        kq=jnp.concatenate([kb, q.astype(BF16)], axis=0),
                        rhs=jnp.concatenate([((bcol * egc) * k).astype(BF16), (bcol * v).astype(BF16)], axis=1),
                        qg=(q * egc).astype(BF16),
                        kdt=(k * kdf).T.astype(BF16)))

        def gram():
            for ch in chains:
                ch["kkqk"] = lax.dot_general(ch["kq"], ch["kb"], (((1,), (1,)), ((), ())),
                                             preferred_element_type=F32)

        def neumann0():
            for ch in chains:
                ch["lmat"] = jnp.where(gt, ch["bcol"] * ch["kkqk"][:c_len] * ch["decay"], 0.0)
                ch["n1"] = jnp.where(blk16, -ch["lmat"], 0.0)
                ch["t"] = eye + ch["n1"]
            for ch in chains:
                ch["n2"] = _mm(ch["n1"], ch["n1"])

        def neumann1():
            for ch in chains:
                ch["n4"] = _mm(ch["n2"], ch["n2"])
                ch["t"] = ch["t"] + _mm(ch["t"], ch["n2"])

        def neumann2():
            for ch in chains:
                ch["n8"] = _mm(ch["n4"], ch["n4"])
                ch["t"] = ch["t"] + _mm(ch["t"], ch["n4"])

        def neumann3():
            for ch in chains:
                ch["t"] = ch["t"] + _mm(ch["t"], ch["n8"])

        def merge_a(pm):
            def run():
                for ch in chains:
                    ch["x"] = _mm(ch["t"], jnp.where(pm, ch["lmat"], 0.0))
            return run

        def merge_b():
            for ch in chains:
                ch["t"] = ch["t"] - _mm(ch["x"], ch["t"])

        def finish():
            for ch in chains:
                wu = jnp.dot(ch["t"].astype(BF16), ch["rhs"], preferred_element_type=F32)
                ci, h = ch["ci"], ch["h"]
                wq_scr[ci, h] = jnp.concatenate([wu[:, :hd].astype(BF16), ch["qg"]], axis=0)
                u0_scr[ci, h] = wu[:, hd:]
                ak_scr[ci, h] = jnp.concatenate([(ch["kkqk"][c_len:] * ch["decay"]).astype(BF16), ch["kdt"]],
                                                axis=0)

        stages = [load, gram, neumann0, neumann1, neumann2, neumann3]
        for pm in pair_masks:
            stages += [merge_a(pm), merge_b]
        return stages + [finish]

    def recur_stages(ci):
        rows = slice(ci * c_len, (ci + 1) * c_len)
        heads = range(GDN_HEADS)
        st = {}

        def first():
            st["ss"] = [s_scr[h] for h in heads]
            st["wsqs"] = [jnp.dot(wq_scr[ci, h], st["ss"][h].astype(BF16), preferred_element_type=F32)
                          for h in heads]

        def second():
            us = [(u0_scr[ci, h] - st["wsqs"][h][:c_len]).astype(BF16) for h in heads]
            st["auku"] = [jnp.dot(ak_scr[ci, h], us[h], preferred_element_type=F32) for h in heads]

        def third():
            egl = egl_scr[ci]
            for h in heads:
                lo = h * hd
                gl = GDN_HEADS + h
                o = st["wsqs"][h][c_len:] + st["auku"][h][:c_len]
                s_scr[h] = egl[0:1, gl:gl + 1] * st["ss"][h] + st["auku"][h][c_len:]
                o_ref[rows, lo:lo + hd] = _gated_norm(o, gn, z_ref[rows, lo:lo + hd]).astype(o_ref.dtype)

        return [first, second, third]

    def run_interleaved(main, early, side):
        n_main = len(main)
        done = 0
        for idx, stage in enumerate(main):
            stage()
            if idx == 0:
                for extra in early:
                    extra()
            want = (len(side) * (idx + 1)) // n_main
            while done < want:
                side[done]()
                done += 1

    conv_rows(0, pair_rows)
    for p in range(n_pairs):
        early = []
        side = []
        if p + 1 < n_pairs:
            early.append(functools.partial(conv_rows, (p + 1) * pair_rows, pair_rows))
        if p > 0:
            for ck in range(GDN_CHUNKS_PER_ITER):
                side += recur_stages((p - 1) * GDN_CHUNKS_PER_ITER + ck)
        run_interleaved(prep_stages(p), early, side)
    for ck in range(GDN_CHUNKS_PER_ITER):
        for stage in recur_stages((n_pairs - 1) * GDN_CHUNKS_PER_ITER + ck):
            stage()

    tail = xp_scr[tm:tm + SUBLANES, :]
    xp_scr[0:SUBLANES, :] = tail

    @pl.when(i == last)
    def _():
        convout_ref[0] = tail
        sout_ref[0] = s_scr[...]


def _gdn_prompt(qkv, ba, z, l, cw, alog, dtb, gn, conv0, s0, tm):
    n_seq = s0.shape[0]
    t = qkv.shape[0]
    nt = t // n_seq // tm
    row = lambda n: pl.BlockSpec((tm, n), lambda b, i: (b * nt + i, 0))
    lay = lambda a: _layer_spec(a, l)
    hd = GDN_HEAD_DIM
    return pl.pallas_call(
        functools.partial(_gdn_prompt_kernel, tm=tm),
        grid=(n_seq, nt),
        in_specs=[row(QKV_WIDTH), row(LANES), row(GDN_WIDTH), lay(cw), lay(alog), lay(dtb), lay(gn),
                  pl.BlockSpec((1, SUBLANES, QKV_WIDTH), lambda b, i: (b, 0, 0)),
                  pl.BlockSpec((1, GDN_HEADS, hd, hd), lambda b, i: (b, 0, 0, 0))],
        out_specs=[row(GDN_WIDTH),
                   pl.BlockSpec((1, GDN_HEADS, hd, hd), lambda b, i: (b, 0, 0, 0)),
                   pl.BlockSpec((1, SUBLANES, QKV_WIDTH), lambda b, i: (b, 0, 0))],
        out_shape=[jax.ShapeDtypeStruct((t, GDN_WIDTH), BF16),
                   jax.ShapeDtypeStruct((n_seq, GDN_HEADS, hd, hd), F32),
                   jax.ShapeDtypeStruct((n_seq, SUBLANES, QKV_WIDTH), F32)],
        scratch_shapes=[pltpu.VMEM((tm + SUBLANES, QKV_WIDTH), F32), pltpu.VMEM((tm, QKV_WIDTH), F32),
                        pltpu.VMEM((tm, LANES), F32), pltpu.VMEM((tm, LANES), F32),
                        pltpu.VMEM((GDN_HEADS, hd, hd), F32),
                        pltpu.VMEM((tm // GDN_CHUNK, GDN_HEADS, 2 * GDN_CHUNK, hd), BF16),
                        pltpu.VMEM((tm // GDN_CHUNK, GDN_HEADS, 2 * GDN_CHUNK, hd), BF16),
                        pltpu.VMEM((tm // GDN_CHUNK, GDN_HEADS, GDN_CHUNK, hd), F32),
                        pltpu.VMEM((tm // GDN_CHUNK, SUBLANES, LANES), F32)],
        compiler_params=_cparams(("arbitrary", "arbitrary")),
        name="gdn_prompt",
    )(qkv, ba, z, cw, alog, dtb, gn, conv0, s0)


def _gdn_step_kernel(qkv_ref, ba_ref, z_ref, cw_ref, alog_ref, dtb_ref, gn_ref, conv_ref, s_ref,
                     o_ref, sout_ref, convout_ref):
    nb = SUBLANES
    hd = GDN_HEAD_DIM
    x = qkv_ref[...]
    cb = conv_ref[...]
    cw = cw_ref[...]
    b0 = cb[:, 0:QKV_WIDTH]
    b1 = cb[:, QKV_WIDTH:2 * QKV_WIDTH]
    b2 = cb[:, 2 * QKV_WIDTH:3 * QKV_WIDTH]
    y = b0 * cw[0:1, :]
    y = y + b1 * cw[1:2, :]
    y = y + b2 * cw[2:3, :]
    y = y + x * cw[3:4, :]
    y = _silu(y)
    convout_ref[...] = jnp.concatenate([b1, b2, x], axis=1)

    beta, g = _gdn_gates(ba_ref[...], alog_ref[...], dtb_ref[...])
    a = jnp.exp(g)
    gn = gn_ref[...]
    zpad = jnp.zeros((hd - nb, hd), F32)
    for h in range(GDN_HEADS):
        lo = h * hd
        q = _l2n(y[:, lo:lo + hd]) * (hd ** -0.5)
        k = _l2n(y[:, GDN_WIDTH + lo:GDN_WIDTH + lo + hd])
        v = y[:, 2 * GDN_WIDTH + lo:2 * GDN_WIDTH + lo + hd]
        kt = jnp.concatenate([k, zpad], axis=0).T
        qt = jnp.concatenate([q, zpad], axis=0).T
        kq = jnp.sum(k * q, axis=-1, keepdims=True)
        bh = beta[:, h:h + 1]
        ah = a[:, GDN_HEADS + h:GDN_HEADS + h + 1]
        o_rows = []
        for n in range(nb):
            s = s_ref[n, h]
            kc = kt[:, n:n + 1]
            qc = qt[:, n:n + 1]
            rk = jnp.sum(s * kc, axis=0, keepdims=True)
            rq = jnp.sum(s * qc, axis=0, keepdims=True)
            an = ah[n:n + 1, :]
            un = bh[n:n + 1, :] * (v[n:n + 1, :] - an * rk)
            sout_ref[n, h] = an * s + kc * un
            o_rows.append(an * rq + kq[n:n + 1, :] * un)
        o = jnp.concatenate(o_rows, axis=0)
        o_ref[:, lo:lo + hd] = _gated_norm(o, gn, z_ref[:, lo:lo + hd])


def _gdn_step(qkv, ba, z, l, cw, alog, dtb, gn, conv, s):
    n_seq = qkv.shape[0]
    nb = SUBLANES
    hd = GDN_HEAD_DIM
    row = lambda n: pl.BlockSpec((nb, n), lambda i: (i, 0))
    lay = lambda a: _layer_spec(a, l)
    sspec = pl.BlockSpec((nb, GDN_HEADS, hd, hd), lambda i: (i, 0, 0, 0))
    return pl.pallas_call(
        _gdn_step_kernel,
        grid=(n_seq // nb,),
        in_specs=[row(QKV_WIDTH), row(LANES), row(GDN_WIDTH), lay(cw), lay(alog), lay(dtb), lay(gn),
                  pl.BlockSpec((None, nb, 3 * QKV_WIDTH), lambda i: (l, i, 0)),
                  pl.BlockSpec((None, nb, GDN_HEADS, hd, hd), lambda i: (l, i, 0, 0, 0))],
        out_specs=[row(GDN_WIDTH), sspec, row(3 * QKV_WIDTH)],
        out_shape=[jax.ShapeDtypeStruct((n_seq, GDN_WIDTH), F32),
                   jax.ShapeDtypeStruct((n_seq, GDN_HEADS, hd, hd), F32),
                   jax.ShapeDtypeStruct((n_seq, 3 * QKV_WIDTH), F32)],
        compiler_params=_cparams(("arbitrary",)),
        name="gdn_step",
    )(qkv, ba, z, cw, alog, dtb, gn, conv, s)


def _s5_param_kernel(lre_ref, lim_ref, ldt_ref, bre_ref, bim_ref, are_ref, aim_ref, bbre_ref, bbim_ref):
    lre = lre_ref[...]
    lim = lim_ref[...]
    dt = jnp.exp(ldt_ref[...])
    mag = jnp.exp(lre * dt)
    are = mag * jnp.cos(lim * dt)
    aim = mag * jnp.sin(lim * dt)
    are_ref[...] = are
    aim_ref[...] = aim
    nre = are - 1.0
    den = lre * lre + lim * lim
    cre = (nre * lre + aim * lim) / den
    cim = (aim * lre - nre * lim) / den
    cre = cre[:, None, :]
    cim = cim[:, None, :]
    bre = bre_ref[...]
    bim = bim_ref[...]
    bbre_ref[...] = cre * bre - cim * bim
    bbim_ref[...] = cre * bim + cim * bre


def _s5_params(lam_re, lam_im, log_dt, b_re, b_im):
    p = lam_re.shape[-1]
    lam_re = lam_re.reshape(-1, p)
    lam_im = lam_im.reshape(-1, p)
    g = lam_re.shape[0]
    bt_re = jnp.swapaxes(b_re.reshape((g,) + b_re.shape[2:]), 1, 2)
    bt_im = jnp.swapaxes(b_im.reshape((g,) + b_im.shape[2:]), 1, 2)
    cg = bt_re.shape[1]
    return pl.pallas_call(
        _s5_param_kernel,
        out_shape=[jax.ShapeDtypeStruct((g, p), F32), jax.ShapeDtypeStruct((g, p), F32),
                   jax.ShapeDtypeStruct((g, cg, p), F32), jax.ShapeDtypeStruct((g, cg, p), F32)],
        name="s5_params",
    )(lam_re, lam_im, log_dt.reshape(g, 1), bt_re, bt_im)


def _slab_blockdiag(m):
    g, cg, p = m.shape
    gl = S5_GROUPS // S5_SLABS
    m4 = m.reshape(g // gl, gl, cg, p)
    eye = jnp.eye(gl, dtype=m.dtype)
    return jnp.einsum('igcp,gh->igchp', m4, eye).reshape(g // S5_GROUPS, S5_SLABS, gl * cg, gl * p)


def _slab_vec(v):
    g, p = v.shape
    return v.reshape(g // S5_GROUPS, S5_SLABS, (S5_GROUPS // S5_SLABS) * p)


def _gelu_tanh(x):
    return 0.5 * x * (1.0 + jnp.tanh(math.sqrt(2.0 / math.pi) * (x + 0.044715 * (x * x * x))))


def _s5_kernel(u_ref, x0_ref, a_ref, bmat_ref, cre_ref, cim_ref, d_ref, gluw_ref, glub_ref, ng_ref,
               o_ref, xout_ref, utb_scr, xs_scr, x_scr, y_scr, ab_scr, *, nb, tt, interleave):
    rows = nb * tt
    ss = S5_SLAB_STATE
    i = pl.program_id(0)

    @pl.when(i == 0)
    def _():
        x_scr[...] = x0_ref[...]
        ab_scr[...] = jnp.broadcast_to(a_ref[...], ab_scr.shape)

    if interleave:
        for b in range(nb):
            for s in range(S5_SLABS):
                utb_scr[pl.ds(s * rows + b, tt, stride=nb), :] = u_ref[b, :, s * LANES:(s + 1) * LANES]
    else:
        for s in range(S5_SLABS):
            utb_scr[s * rows:(s + 1) * rows, :] = u_ref[:, s * LANES:(s + 1) * LANES]

    for s in range(S5_SLABS):
        xs_scr[:, s * 2 * ss:(s + 1) * 2 * ss] = jnp.dot(utb_scr[s * rows:(s + 1) * rows, :].astype(BF16),
                                                       bmat_ref[s], preferred_element_type=F32)

    def step(t, carry):
        r0 = pl.multiple_of(t * nb, nb)
        for s in range(S5_SLABS):
            lo = s * 2 * ss
            ar = ab_scr[:, lo:lo + ss]
            ai = ab_scr[:, lo + ss:lo + 2 * ss]
            xr = x_scr[:, lo:lo + ss]
            xi = x_scr[:, lo + ss:lo + 2 * ss]
            nr = (ar * xr - ai * xi) + xs_scr[pl.ds(r0, nb), lo:lo + ss]
            ni = (ar * xi + ai * xr) + xs_scr[pl.ds(r0, nb), lo + ss:lo + 2 * ss]
            x_scr[:, lo:lo + ss] = nr
            x_scr[:, lo + ss:lo + 2 * ss] = ni
            xs_scr[pl.ds(r0, nb), lo:lo + ss] = nr
            xs_scr[pl.ds(r0, nb), lo + ss:lo + 2 * ss] = ni
        return carry

    lax.fori_loop(0, tt, step, 0, unroll=min(tt, 4))

    @pl.when(i == pl.num_programs(0) - 1)
    def _():
        xout_ref[...] = x_scr[...]

    ys = []
    for s in range(S5_SLABS):
        lo = s * 2 * ss
        yr = jnp.dot(xs_scr[:, lo:lo + ss].astype(BF16), cre_ref[s], preferred_element_type=F32)
        yi = jnp.dot(xs_scr[:, lo + ss:lo + 2 * ss].astype(BF16), cim_ref[s], preferred_element_type=F32)
        ys.append((yr - yi) + d_ref[0:1, s * LANES:(s + 1) * LANES] * utb_scr[s * rows:(s + 1) * rows, :])
    y = _gelu_tanh(jnp.concatenate(ys, axis=1))
    y = y * _sigmoid(_mm(y, gluw_ref[...]) + glub_ref[...])
    y = y * lax.rsqrt(jnp.mean(y * y, axis=-1, keepdims=True) + NORM_EPS) * ng_ref[...]
    if interleave:
        for s in range(S5_SLABS):
            y_scr[s * rows:(s + 1) * rows, :] = y[:, s * LANES:(s + 1) * LANES]
        for b in range(nb):
            for s in range(S5_SLABS):
                o_ref[b, :, s * LANES:(s + 1) * LANES] = y_scr[pl.ds(s * rows + b, tt, stride=nb), :].astype(
                    o_ref.dtype)
    else:
        o_ref[...] = y


def _s5(u, x0, l, a, bmat, cre, cim, d, gluw, glub, ng, nb, tt, interleave):
    rows = nb * tt
    nstate = x0.shape[1]
    full = lambda arr: pl.BlockSpec(arr.shape, lambda i: (0,) * arr.ndim)
    lay = lambda arr: _layer_spec(arr, l)
    if interleave:
        steps = u.shape[1] // tt
        uspec = pl.BlockSpec((nb, tt, S5_WIDTH), lambda i: (0, i, 0))
        oshape = jax.ShapeDtypeStruct(u.shape, BF16)
    else:
        steps = 1
        uspec = pl.BlockSpec((nb, S5_WIDTH), lambda i: (0, 0))
        oshape = jax.ShapeDtypeStruct(u.shape, F32)
    return pl.pallas_call(
        functools.partial(_s5_kernel, nb=nb, tt=tt, interleave=interleave),
        grid=(steps,),
        in_specs=[uspec, full(x0), lay(a), lay(bmat), lay(cre), lay(cim), lay(d), lay(gluw), lay(glub),
                  lay(ng)],
        out_specs=[uspec, full(x0)],
        out_shape=[oshape, jax.ShapeDtypeStruct(x0.shape, F32)],
        scratch_shapes=[pltpu.VMEM((S5_SLABS * rows, LANES), F32), pltpu.VMEM((rows, nstate), F32),
                        pltpu.VMEM((nb, nstate), F32), pltpu.VMEM((S5_SLABS * rows, LANES), F32),
                        pltpu.VMEM((nb, nstate), F32)],
        compiler_params=_cparams(("arbitrary",)),
        name="s5_scan",
    )(u, x0, a, bmat, cre, cim, d, gluw, glub, ng)


def _router(logits_t, bias_col):
    scores = _sigmoid(logits_t)
    sel = scores + bias_col
    s = [sel[e:e + 1, :] for e in range(N_EXPERTS)]
    n_groups = N_EXPERTS // EXPERTS_PER_GROUP
    gs = []
    for gi in range(n_groups):
        m = s[gi * EXPERTS_PER_GROUP: (gi + 1) * EXPERTS_PER_GROUP]
        best = None
        for p in range(EXPERTS_PER_GROUP):
            for q in range(p + 1, EXPERTS_PER_GROUP):
                ps = m[p] + m[q]
                best = ps if best is None else jnp.maximum(best, ps)
        gs.append(best)
    gmax = functools.reduce(jnp.maximum, gs)
    taken = None
    in_best = []
    for gi in range(n_groups):
        hit = gs[gi] == gmax
        if taken is None:
            cur = hit
            taken = hit
        else:
            cur = jnp.logical_and(hit, jnp.logical_not(taken))
            taken = jnp.logical_or(taken, hit)
        in_best.append(cur)
    selm = []
    picked = []
    for e in range(N_EXPERTS):
        gi = e // EXPERTS_PER_GROUP
        cnt = jnp.zeros_like(s[e])
        for j in range(gi * EXPERTS_PER_GROUP, (gi + 1) * EXPERTS_PER_GROUP):
            if j == e:
                continue
            beats = (s[j] >= s[e]) if j < e else (s[j] > s[e])
            cnt = cnt + jnp.where(beats, 1.0, 0.0)
        sel_e = jnp.logical_and(in_best[gi], cnt < 1.5)
        selm.append(jnp.where(sel_e, 1.0, 0.0))
        picked.append(jnp.where(sel_e, scores[e:e + 1, :], 0.0))
    denom = functools.reduce(lambda x, y: x + y, picked)
    comb = [p / denom for p in picked]
    return comb, selm, in_best


def _router_dense(logits_t, bias_col):
    comb, _, _ = _router(logits_t, bias_col)
    return jnp.concatenate(comb, axis=0)


def _router_sparse(logits_t, bias_col):
    comb, selm, in_best = _router(logits_t, bias_col)
    cls = None
    wlo = None
    whi = None
    for gi in range(N_EXPERTS // EXPERTS_PER_GROUP):
        term = jnp.where(in_best[gi], float(len(PAIR_CODES) * gi) - 1.0, 0.0)
        cls = term if cls is None else cls + term
        seen = None
        for j in range(EXPERTS_PER_GROUP):
            e = gi * EXPERTS_PER_GROUP + j
            cls = cls + selm[e] * PAIR_VALUE[j]
            first = selm[e] if seen is None else selm[e] * (1.0 - seen)
            seen = selm[e] if seen is None else jnp.maximum(seen, selm[e])
            lo_term = first * comb[e]
            hi_term = (selm[e] - first) * comb[e]
            wlo = lo_term if wlo is None else wlo + lo_term
            whi = hi_term if whi is None else whi + hi_term
    return cls, wlo, whi


def _outproj_kernel(oa_ref, ob_ref, x_ref, gt_ref, sc_ref, sh_ref, g_ref, wout_ref, wrt_ref, br_ref,
                    xo_ref, hn_ref, comb_ref, *, tm):
    o = jnp.concatenate([oa_ref[...].astype(BF16), ob_ref[...].astype(BF16)], axis=1)
    mix = jnp.dot(o, wout_ref[...], preferred_element_type=F32)
    x = x_ref[...] + gt_ref[...] * mix
    xo_ref[...] = x
    ms = jnp.mean(x * x, axis=-1, keepdims=True)
    hn = x * lax.rsqrt(ms + NORM_EPS) * g_ref[...]
    hn = hn * (1.0 + sc_ref[...]) + sh_ref[...]
    hb = hn.astype(BF16)
    hn_ref[...] = hb
    logits_t = lax.dot_general(wrt_ref[...], hb, (((1,), (1,)), ((), ())), preferred_element_type=F32)
    comb_t = _router_dense(logits_t, br_ref[...])
    pad = jnp.zeros((LANES - N_EXPERTS, tm), F32)
    comb_ref[...] = jnp.concatenate([comb_t, pad], axis=0).T


def _outproj(oa, ob, x, mod, l, g, wout, wrt, br, tm, rows_per_seq):
    t = x.shape[0]
    full = lambda a: pl.BlockSpec(a.shape, lambda i: (0,) * a.ndim)
    lay = lambda a: _layer_spec(a, l)
    row = lambda n: pl.BlockSpec((tm, n), lambda i: (i, 0))
    ms = lambda j: _mod_spec(mod, l, j, tm, rows_per_seq)
    return pl.pallas_call(
        functools.partial(_outproj_kernel, tm=tm),
        grid=(t // tm,),
        in_specs=[row(GDN_WIDTH), row(S5_WIDTH), row(D_MODEL), ms(2), ms(4), ms(3), lay(g), lay(wout),
                  full(wrt), full(br)],
        out_specs=[row(D_MODEL), row(D_MODEL), row(LANES)],
        out_shape=[jax.ShapeDtypeStruct((t, D_MODEL), F32), jax.ShapeDtypeStruct((t, D_MODEL), BF16),
                   jax.ShapeDtypeStruct((t, LANES), F32)],
        compiler_params=_cparams(("arbitrary",)),
        name="outproj_router",
    )(oa, ob, x, mod, mod, mod, g, wout, wrt, br)


def _outproj_sparse_kernel(oa_ref, ob_ref, x_ref, gt_ref, sc_ref, sh_ref, g_ref, wout_ref, wrt_ref, br_ref,
                           xo_ref, hx_ref, route_ref, cnt_ref, run_scr, *, tm):
    i = pl.program_id(0)

    @pl.when(i == 0)
    def _():
        run_scr[...] = jnp.zeros_like(run_scr)

    o = jnp.concatenate([oa_ref[...].astype(BF16), ob_ref[...].astype(BF16)], axis=1)
    mix = jnp.dot(o, wout_ref[...], preferred_element_type=F32)
    x = x_ref[...] + gt_ref[...] * mix
    xo_ref[...] = x
    ms = jnp.mean(x * x, axis=-1, keepdims=True)
    hn = x * lax.rsqrt(ms + NORM_EPS) * g_ref[...]
    hn = hn * (1.0 + sc_ref[...]) + sh_ref[...]
    hb = hn.astype(BF16)
    logits_t = lax.dot_general(wrt_ref[...], hb, (((1,), (1,)), ((), ())), preferred_element_type=F32)
    cls, wlo, whi = _router_sparse(logits_t, br_ref[...])
    sub = lax.broadcasted_iota(jnp.int32, (CLASS_ROWS, tm), 0).astype(F32)
    onehot = sub == cls
    r = lax.broadcasted_iota(jnp.int32, (tm, tm), 0)
    c = lax.broadcasted_iota(jnp.int32, (tm, tm), 1)
    before = jnp.where(r < c, 1.0, 0.0).astype(BF16)
    prefix = jnp.dot(jnp.where(onehot, 1.0, 0.0).astype(BF16), before, preferred_element_type=F32)
    run = run_scr[...]
    rank = jnp.sum(jnp.where(onehot, prefix + run[:, 0:1], 0.0), axis=0, keepdims=True)
    run_scr[...] = run + jnp.sum(jnp.where(onehot, 1.0, 0.0), axis=1, keepdims=True)
    cnt_ref[...] = run_scr[...]
    route = jnp.concatenate([cls, rank, wlo, whi, jnp.zeros((SUBLANES - 4, tm), F32)], axis=0)
    route_ref[...] = route
    hx_ref[:, :D_MODEL] = hn
    hx_ref[:, D_MODEL:] = jnp.concatenate([route, jnp.zeros((LANES - SUBLANES, tm), F32)], axis=0).T


def _outproj_sparse(oa, ob, x, mod, l, g, wout, wrt, br, tm, rows_per_seq):
    t = x.shape[0]
    full = lambda a: pl.BlockSpec(a.shape, lambda i: (0,) * a.ndim)
    lay = lambda a: _layer_spec(a, l)
    row = lambda n: pl.BlockSpec((tm, n), lambda i: (i, 0))
    ms = lambda j: _mod_spec(mod, l, j, tm, rows_per_seq)
    return pl.pallas_call(
        functools.partial(_outproj_sparse_kernel, tm=tm),
        grid=(t // tm,),
        in_specs=[row(GDN_WIDTH), row(S5_WIDTH), row(D_MODEL), ms(2), ms(4), ms(3), lay(g), lay(wout),
                  full(wrt), full(br)],
        out_specs=[row(D_MODEL), row(HX_WIDTH), pl.BlockSpec((SUBLANES, tm), lambda i: (0, i)),
                   pl.BlockSpec((CLASS_ROWS, LANES), lambda i: (0, 0))],
        out_shape=[jax.ShapeDtypeStruct((t, D_MODEL), F32), jax.ShapeDtypeStruct((t, HX_WIDTH), F32),
                   jax.ShapeDtypeStruct((SUBLANES, t), F32), jax.ShapeDtypeStruct((CLASS_ROWS, LANES), F32)],
        scratch_shapes=[pltpu.VMEM((CLASS_ROWS, LANES), F32)],
        compiler_params=_cparams(("arbitrary",)),
        name="outproj_route",
    )(oa, ob, x, mod, mod, mod, g, wout, wrt, br)


def _dispatch_kernel(pos_ref, tail_ref, hx_ref, sorted_ref, stage, zero_scr, sems, zsem, *, tm, n_steps):
    i = pl.program_id(0)
    slot = i % 2

    def tail_copy(cidx):
        start = pl.multiple_of(jnp.maximum(tail_ref[cidx], 0), MOE_TILE)
        return pltpu.make_async_copy(zero_scr, sorted_ref.at[pl.ds(start, MOE_TILE), :], zsem)

    @pl.when(i == 0)
    def _():
        zero_scr[...] = jnp.zeros_like(zero_scr)
        for cidx in range(2 * N_CLASSES):
            @pl.when(tail_ref[cidx] >= 0)
            def _():
                tail_copy(cidx).start()
        for cidx in range(2 * N_CLASSES):
            @pl.when(tail_ref[cidx] >= 0)
            def _():
                tail_copy(cidx).wait()

    def row_copy(s, r, p):
        return pltpu.make_async_copy(stage.at[s, pl.ds(r, 1), :], sorted_ref.at[pl.ds(p, 1), :], sems.at[s])

    def wait_slot(s):
        pltpu.make_async_copy(stage.at[s], sorted_ref.at[pl.ds(0, tm), :], sems.at[s]).wait()

    @pl.when(i >= 2)
    def _():
        wait_slot(slot)

    stage[slot] = hx_ref[...]
    base = i * tm

    def issue(g, carry):
        r8 = pl.multiple_of(g * SUBLANES, SUBLANES)
        for k in range(SUBLANES):
            row_copy(slot, r8 + k, pos_ref[base + r8 + k]).start(priority=k % 2)
        return carry

    lax.fori_loop(0, tm // SUBLANES, issue, 0)

    @pl.when(i == n_steps - 1)
    def _():
        wait_slot(slot)
        if n_steps > 1:
            wait_slot(1 - slot)


def _dispatch(pos, tail, hx, n_rows, tm):
    t = hx.shape[0]
    n_steps = t // tm
    return pl.pallas_call(
        functools.partial(_dispatch_kernel, tm=tm, n_steps=n_steps),
        grid_spec=pltpu.PrefetchScalarGridSpec(
            num_scalar_prefetch=2,
            grid=(n_steps,),
            in_specs=[pl.BlockSpec((tm, HX_WIDTH), lambda i, pos, tail: (i, 0))],
            out_specs=pl.BlockSpec(memory_space=pl.ANY),
            scratch_shapes=[pltpu.VMEM((2, tm, HX_WIDTH), F32), pltpu.VMEM((MOE_TILE, HX_WIDTH), F32),
                            pltpu.SemaphoreType.DMA((2,)), pltpu.SemaphoreType.DMA(())]),
        out_shape=jax.ShapeDtypeStruct((n_rows, HX_WIDTH), F32),
        compiler_params=_cparams(("arbitrary",)),
        name="moe_dispatch",
    )(pos, tail, hx)


def _moe_sorted_kernel(lo_ref, hi_ref, src_ref, nv_ref, xs_ref, wg_lo_ref, wu_lo_ref, wd_lo_ref, wg_hi_ref,
                       wu_hi_ref, wd_hi_ref, o_ref):
    j = pl.program_id(0)

    @pl.when(j < nv_ref[0])
    def _():
        xs = xs_ref[...]
        x = xs[:, :D_MODEL].astype(BF16)
        wlo = xs[:, D_MODEL + 2:D_MODEL + 3]
        whi = xs[:, D_MODEL + 3:D_MODEL + 4]
        dot = functools.partial(jnp.dot, preferred_element_type=F32)
        a_lo = _silu(dot(x, wg_lo_ref[...])) * dot(x, wu_lo_ref[...]) * wlo
        a_hi = _silu(dot(x, wg_hi_ref[...])) * dot(x, wu_hi_ref[...]) * whi
        o_ref[...] = dot(a_lo.astype(BF16), wd_lo_ref[...]) + dot(a_hi.astype(BF16), wd_hi_ref[...])

    @pl.when(j >= nv_ref[0])
    def _():
        o_ref[...] = jnp.zeros_like(o_ref)


def _moe_sorted(tile_lo, tile_hi, tile_src, n_valid, xs, l, wg, wu, wd):
    n_rows = xs.shape[0]
    n_tiles = n_rows // MOE_TILE
    wspec = lambda w, which: pl.BlockSpec(
        (None, None) + w.shape[2:], lambda j, lo, hi, src, nv: (l, (lo, hi)[which][j], 0, 0))
    return pl.pallas_call(
        _moe_sorted_kernel,
        grid_spec=pltpu.PrefetchScalarGridSpec(
            num_scalar_prefetch=4,
            grid=(n_tiles,),
            in_specs=[pl.BlockSpec((MOE_TILE, HX_WIDTH), lambda j, lo, hi, src, nv: (src[j], 0)),
                      wspec(wg, 0), wspec(wu, 0), wspec(wd, 0), wspec(wg, 1), wspec(wu, 1), wspec(wd, 1)],
            out_specs=pl.BlockSpec((MOE_TILE, D_MODEL), lambda j, lo, hi, src, nv: (j, 0))),
        out_shape=jax.ShapeDtypeStruct((n_rows, D_MODEL), F32),
        compiler_params=_cparams(("arbitrary",)),
        name="moe_sorted",
    )(tile_lo, tile_hi, tile_src, n_valid, xs, wg, wu, wd, wg, wu, wd)


def _combine_kernel(pos_ref, ys_ref, x_ref, gt_ref, fg_ref, o_ref, buf, sems, *, tm, n_steps, final_norm):
    i = pl.program_id(0)
    slot = i % 2

    def row_copy(s, r, p):
        return pltpu.make_async_copy(ys_ref.at[pl.ds(p, 1), :], buf.at[s, pl.ds(r, 1), :], sems.at[s])

    def issue(step, s):
        base = step * tm

        def body(g, carry):
            r8 = pl.multiple_of(g * SUBLANES, SUBLANES)
            for k in range(SUBLANES):
                row_copy(s, r8 + k, pos_ref[base + r8 + k]).start(priority=k % 2)
            return carry

        lax.fori_loop(0, tm // SUBLANES, body, 0)

    @pl.when(i == 0)
    def _():
        issue(0, 0)

    @pl.when(i + 1 < n_steps)
    def _():
        issue(i + 1, 1 - slot)

    pltpu.make_async_copy(ys_ref.at[pl.ds(0, tm), :], buf.at[slot], sems.at[slot]).wait()
    x = x_ref[...] + gt_ref[...] * buf[slot]
    if final_norm:
        x = x * lax.rsqrt(jnp.mean(x * x, axis=-1, keepdims=True) + NORM_EPS) * fg_ref[...]
    o_ref[...] = x


def _combine(pos, ys, x, mod, l, fg, tm, rows_per_seq, final_norm):
    t = x.shape[0]
    n_steps = t // tm
    return pl.pallas_call(
        functools.partial(_combine_kernel, tm=tm, n_steps=n_steps, final_norm=final_norm),
        grid_spec=pltpu.PrefetchScalarGridSpec(
            num_scalar_prefetch=1,
            grid=(n_steps,),
            in_specs=[pl.BlockSpec(memory_space=pl.ANY),
                      pl.BlockSpec((tm, D_MODEL), lambda i, pos: (i, 0)),
                      _mod_spec(mod, l, 5, tm, rows_per_seq),
                      pl.BlockSpec(fg.shape, lambda i, pos: (0, 0))],
            out_specs=pl.BlockSpec((tm, D_MODEL), lambda i, pos: (i, 0)),
            scratch_shapes=[pltpu.VMEM((2, tm, D_MODEL), F32), pltpu.SemaphoreType.DMA((2,))]),
        out_shape=jax.ShapeDtypeStruct((t, D_MODEL), F32),
        compiler_params=_cparams(("arbitrary",)),
        name="moe_combine",
    )(pos, ys, x, mod, fg)


def _route_plan(route, counts, n_tokens):
    cnt = counts[:N_CLASSES, 0].astype(jnp.int32)
    padded = ((cnt + MOE_TILE - 1) // MOE_TILE) * MOE_TILE
    ends = jnp.cumsum(padded)
    offsets = ends - padded
    cls = route[0].astype(jnp.int32)
    pos = route[1].astype(jnp.int32) + jnp.sum(
        jnp.where(cls[:, None] == jnp.arange(N_CLASSES, dtype=jnp.int32)[None, :], offsets[None, :], 0), axis=1)
    n_tiles = n_tokens // MOE_TILE + N_CLASSES
    n_valid = ends[-1] // MOE_TILE
    src = jnp.minimum(jnp.arange(n_tiles, dtype=jnp.int32), jnp.maximum(n_valid - 1, 0))
    tile_cls = jnp.minimum(jnp.sum((src[:, None] * MOE_TILE >= ends[None, :]).astype(jnp.int32), axis=1),
                           N_CLASSES - 1)
    group = tile_cls // len(PAIR_CODES)
    pair = jnp.asarray(PAIR_CODES, dtype=jnp.int32)[tile_cls % len(PAIR_CODES)]
    tile_lo = group * EXPERTS_PER_GROUP + pair[:, 0]
    tile_hi = group * EXPERTS_PER_GROUP + pair[:, 1]
    unused = (n_valid + jnp.arange(N_CLASSES, dtype=jnp.int32)) * MOE_TILE
    tail = jnp.concatenate([jnp.where(cnt > 0, ends - MOE_TILE, -1),
                            jnp.where(unused < n_tiles * MOE_TILE, unused, -1)]).astype(jnp.int32)
    return pos, tail, tile_lo, tile_hi, src, n_valid.reshape(1).astype(jnp.int32), n_tiles * MOE_TILE


def _moe_kernel(hn_ref, comb_ref, x_ref, gt_ref, wg_ref, wu_ref, wd_ref, fg_ref, o_ref, acc_scr, *, final_norm):
    e = pl.program_id(1)

    @pl.when(e == 0)
    def _():
        acc_scr[...] = jnp.zeros_like(acc_scr)

    hg = jnp.dot(hn_ref[...], wg_ref[...], preferred_element_type=F32)
    hu = jnp.dot(hn_ref[...], wu_ref[...], preferred_element_type=F32)
    lane = lax.broadcasted_iota(jnp.int32, (1, LANES), 1)
    ce = jnp.sum(jnp.where(lane == e, comb_ref[...], 0.0), axis=-1, keepdims=True)
    act = _silu(hg) * hu * ce
    acc_scr[...] += jnp.dot(act.astype(BF16), wd_ref[...], preferred_element_type=F32)

    @pl.when(e == pl.num_programs(1) - 1)
    def _():
        x = x_ref[...] + gt_ref[...] * acc_scr[...]
        if final_norm:
            x = x * lax.rsqrt(jnp.mean(x * x, axis=-1, keepdims=True) + NORM_EPS) * fg_ref[...]
        o_ref[...] = x


def _moe(hn, comb, x, mod, l, wg, wu, wd, fg, tm, rows_per_seq, final_norm):
    t = x.shape[0]
    row = lambda n: pl.BlockSpec((tm, n), lambda i, e: (i, 0))
    wspec = lambda w: pl.BlockSpec((None, None) + w.shape[2:], lambda i, e: (l, e, 0, 0))
    return pl.pallas_call(
        functools.partial(_moe_kernel, final_norm=final_norm),
        grid=(t // tm, N_EXPERTS),
        in_specs=[row(D_MODEL), row(LANES), row(D_MODEL), _mod_spec(mod, l, 5, tm, rows_per_seq),
                  wspec(wg), wspec(wu), wspec(wd),
                  pl.BlockSpec(fg.shape, lambda i, e: (0, 0))],
        out_specs=row(D_MODEL),
        out_shape=jax.ShapeDtypeStruct((t, D_MODEL), F32),
        scratch_shapes=[pltpu.VMEM((tm, D_MODEL), F32)],
        compiler_params=_cparams(("arbitrary", "arbitrary")),
        name="moe",
    )(hn, comb, x, mod, wg, wu, wd, fg)


def _pad_lanes(v, offset):
    return jnp.pad(v, ((0, 0), (offset, LANES - offset - v.shape[1])))[:, None, :]


def _state_to_slab(re, im):
    lead = re.shape[:2]
    r = re.reshape(lead + (S5_SLABS, S5_SLAB_STATE))
    i = im.reshape(lead + (S5_SLABS, S5_SLAB_STATE))
    return jnp.stack([r, i], axis=3).reshape(lead + (S5_SLABS * 2 * S5_SLAB_STATE,))


def _slab_to_state(x):
    lead = x.shape[:2]
    x4 = x.reshape(lead + (S5_SLABS, 2, S5_SLAB_STATE))
    re = x4[:, :, :, 0].reshape(lead + (S5_GROUPS, S5_STATE))
    im = x4[:, :, :, 1].reshape(lead + (S5_GROUPS, S5_STATE))
    return re, im


def kernel(x_prompt, x_sample, c_prompt, c_sample, state_conv, state_gdn, state_s5_re, state_s5_im, norm1_g, norm2_g, w_ada, b_ada, w_in, conv_w, a_log, dt_bias, gdn_norm_g, s5_lambda_re, s5_lambda_im, s5_log_dt, s5_b_re, s5_b_im, s5_c_re, s5_c_im, s5_d, s5_glu_w, s5_glu_b, s5_norm_g, w_out, w_router, b_router, w_gate, w_up, w_down, final_g):
    bp, seq, _ = x_prompt.shape
    bs = x_sample.shape[0]
    tp = bp * seq
    tm_p = 512
    tm_gdn = 1024 if seq % 1024 == 0 else tm_p
    tt = 64 if seq % 64 == 0 else seq

    mod = _ada(jnp.concatenate([c_prompt, c_sample], axis=0), w_ada, b_ada)
    mod_p = mod[:, :bp].reshape(DEPTH, bp, 1, 6 * D_MODEL)
    mod_s = mod[:, bp:]

    nba = 2 * GDN_HEADS
    u_start = QKV_WIDTH + GDN_WIDTH
    w_in_packed = jnp.concatenate(
        [w_in[:, :, :u_start], w_in[:, :, u_start + nba:], w_in[:, :, u_start:u_start + nba],
         jnp.zeros((DEPTH, D_MODEL, LANES - nba), F32)], axis=2).astype(BF16)
    g1 = norm1_g[:, None, :]
    g2 = norm2_g[:, None, :]
    alog = _pad_lanes(a_log, GDN_HEADS)
    dtb = _pad_lanes(dt_bias, GDN_HEADS)
    gn = gdn_norm_g[:, None, :]
    a_re, a_im, bb_re, bb_im = _s5_params(s5_lambda_re, s5_lambda_im, s5_log_dt, s5_b_re, s5_b_im)
    a_vec = jnp.concatenate([_slab_vec(a_re), _slab_vec(a_im)], axis=2).reshape(DEPTH, 1, -1)
    bmat = jnp.concatenate([_slab_blockdiag(bb_re), _slab_blockdiag(bb_im)], axis=3).astype(BF16)
    flat = lambda m: m.reshape((DEPTH * S5_GROUPS,) + m.shape[2:])
    cre = jnp.swapaxes(_slab_blockdiag(flat(s5_c_re)), 2, 3).astype(BF16)
    cim = jnp.swapaxes(_slab_blockdiag(flat(s5_c_im)), 2, 3).astype(BF16)
    dvec = s5_d[:, None, :]
    gluw = s5_glu_w.astype(BF16)
    glub = s5_glu_b[:, None, :]
    ng = s5_norm_g[:, None, :]
    wout = w_out.astype(BF16)
    wg = w_gate.astype(BF16)
    wu = w_up.astype(BF16)
    wd = w_down.astype(BF16)
    wrt = jnp.transpose(w_router).astype(BF16)
    br = b_router.reshape(N_EXPERTS, 1)
    fg = final_g.reshape(1, D_MODEL)
    conv_s = state_conv.reshape(DEPTH, bs, (CONV_WIDTH - 1) * QKV_WIDTH)

    xp = x_prompt.reshape(tp, D_MODEL)
    xs = x_sample.reshape(bs, D_MODEL)
    outs_p = {k: [] for k in ("conv", "gdn", "s5")}
    outs_s = {k: [] for k in ("conv", "gdn", "s5")}
    zero_conv = jnp.zeros((bp, SUBLANES, QKV_WIDTH), F32)
    zero_gdn = jnp.zeros((bp, GDN_HEADS, GDN_HEAD_DIM, GDN_HEAD_DIM), F32)
    zero_s5 = jnp.zeros((bp, S5_SLABS * 2 * S5_SLAB_STATE), F32)
    x0_s = _state_to_slab(state_s5_re, state_s5_im)

    for l in range(DEPTH):
        last = l == DEPTH - 1

        qkv, z, ba, u = _inproj(xp, mod_p, l, g1, w_in_packed, tm_p, seq)
        oa, sg, cv = _gdn_prompt(qkv, ba, z, l, conv_w, alog, dtb, gn, zero_conv, zero_gdn, tm_gdn)
        ob, xst = _s5(u.reshape(bp, seq, S5_WIDTH), zero_s5, l, a_vec, bmat, cre, cim, dvec, gluw, glub, ng,
                      bp, tt, True)
        xp, hx, route, counts = _outproj_sparse(oa, ob.reshape(tp, S5_WIDTH), xp, mod_p, l, g2, wout, wrt, br,
                                                tm_p, seq)
        pos, tail, tile_lo, tile_hi, tile_src, n_valid, n_rows = _route_plan(route, counts, tp)
        xsorted = _dispatch(pos, tail, hx, n_rows, tm_p)
        ys = _moe_sorted(tile_lo, tile_hi, tile_src, n_valid, xsorted, l, wg, wu, wd)
        xp = _combine(pos, ys, xp, mod_p, l, fg, tm_p, seq, last)
        outs_p["conv"].append(cv)
        outs_p["gdn"].append(sg)
        outs_p["s5"].append(xst)

        qkv, z, ba, u = _inproj(xs, mod_s, l, g1, w_in_packed, bs, 1)
        oa, sg, cv = _gdn_step(qkv, ba, z, l, conv_w, alog, dtb, gn, conv_s, state_gdn)
        ob, xst = _s5(u, x0_s[l], l, a_vec, bmat, cre, cim, dvec, gluw, glub, ng, bs, 1, False)
        xs, hn, comb = _outproj(oa, ob, xs, mod_s, l, g2, wout, wrt, br, bs, 1)
        xs = _moe(hn, comb, xs, mod_s, l, wg, wu, wd, fg, bs, 1, last)
        outs_s["conv"].append(cv)
        outs_s["gdn"].append(sg)
        outs_s["s5"].append(xst)

    st = lambda d, k: jnp.stack(d[k])
    re_p, im_p = _slab_to_state(st(outs_p, "s5"))
    re_s, im_s = _slab_to_state(st(outs_s, "s5"))
    conv_p = st(outs_p, "conv")[:, :, SUBLANES - (CONV_WIDTH - 1):, :]
    conv_s_new = st(outs_s, "conv").reshape(DEPTH, bs, CONV_WIDTH - 1, QKV_WIDTH)
    return (xp.reshape(bp, seq, D_MODEL), xs.reshape(bs, 1, D_MODEL),
            conv_p, st(outs_p, "gdn"), re_p, im_p, conv_s_new, st(outs_s, "gdn"), re_s, im_s)
```

```python
import functools
import math

import jax
import jax.numpy as jnp
from jax import lax
from jax.experimental import pallas as pl
from jax.experimental.pallas import tpu as pltpu

F32 = jnp.float32
BF16 = jnp.bfloat16

D_MODEL = 1024
DEPTH = 2
GDN_HEAD_DIM = 128
GDN_WIDTH = 512
GDN_HEADS = 4
CONV_WIDTH = 4
S5_CH_PER_GROUP = 16
S5_WIDTH = 512
S5_GROUPS = 32
S5_STATE = 64
QKV_WIDTH = 3 * GDN_WIDTH
N_EXPERTS = 16
EXPERTS_PER_GROUP = 4
D_EXPERT = 256
NORM_EPS = 1e-6

LANES = 128
SUBLANES = 8
GDN_CHUNK = 128
GDN_CHUNKS_PER_ITER = 2
S5_SLABS = 4
S5_SLAB_STATE = 512
MOE_TILE = 512
PAIR_VALUE = (0.0, 1.0, 2.0, 4.0)
PAIR_CODES = ((0, 1), (0, 2), (1, 2), (0, 3), (1, 3), (2, 3))
N_CLASSES = (N_EXPERTS // EXPERTS_PER_GROUP) * len(PAIR_CODES)
CLASS_ROWS = 32
HX_WIDTH = D_MODEL + LANES
VMEM_LIMIT = 56 * 1024 * 1024


def _cparams(sem):
    return pltpu.CompilerParams(dimension_semantics=sem, vmem_limit_bytes=VMEM_LIMIT)


def _sigmoid(x):
    return 1.0 / (1.0 + jnp.exp(-x))


def _silu(x):
    return x * _sigmoid(x)


def _softplus(x):
    return jnp.maximum(x, 0.0) + jnp.log1p(jnp.exp(-jnp.abs(x)))


def _mm(a, b):
    return jnp.dot(a.astype(BF16), b.astype(BF16), preferred_element_type=F32)


def _ada_kernel(c_ref, w_ref, b_ref, o_ref):
    c = c_ref[...]
    o_ref[0] = _mm(_silu(c), w_ref[0]) + b_ref[0]


def _ada(c_all, w_ada, b_ada):
    rows = c_all.shape[0]
    n_out = w_ada.shape[-1]
    tn = 1536
    return pl.pallas_call(
        _ada_kernel,
        grid=(DEPTH, n_out // tn),
        in_specs=[
            pl.BlockSpec((rows, D_MODEL), lambda l, j: (0, 0)),
            pl.BlockSpec((1, D_MODEL, tn), lambda l, j: (l, 0, j)),
            pl.BlockSpec((1, 1, tn), lambda l, j: (l, 0, j)),
        ],
        out_specs=pl.BlockSpec((1, rows, tn), lambda l, j: (l, 0, j)),
        out_shape=jax.ShapeDtypeStruct((DEPTH, rows, n_out), F32),
        compiler_params=_cparams(("arbitrary", "arbitrary")),
        name="ada_mod",
    )(c_all, w_ada, b_ada.reshape(DEPTH, 1, n_out))


def _inproj_kernel(x_ref, sc_ref, sh_ref, g_ref, wqkv_ref, wz_ref, wba_ref, wu_ref,
                   qkv_ref, z_ref, ba_ref, u_ref):
    x = x_ref[...]
    ms = jnp.mean(x * x, axis=-1, keepdims=True)
    hn = x * lax.rsqrt(ms + NORM_EPS) * g_ref[...]
    hn = hn * (1.0 + sc_ref[...]) + sh_ref[...]
    hb = hn.astype(BF16)
    qkv_ref[...] = jnp.dot(hb, wqkv_ref[...], preferred_element_type=F32)
    z_ref[...] = jnp.dot(hb, wz_ref[...], preferred_element_type=F32)
    ba_ref[...] = jnp.dot(hb, wba_ref[...], preferred_element_type=F32)
    u_ref[...] = jnp.dot(hb, wu_ref[...], preferred_element_type=F32)


def _mod_spec(mod, l, j, tm, rows_per_seq):
    if mod.ndim == 4:
        tiles_per_seq = rows_per_seq // tm
        return pl.BlockSpec((None, None, 1, D_MODEL), lambda i, *_: (l, i // tiles_per_seq, 0, j))
    return pl.BlockSpec((None, tm, D_MODEL), lambda i, *_: (l, i, j))


def _layer_spec(arr, l):
    return pl.BlockSpec((None,) + arr.shape[1:], lambda *_: (l,) + (0,) * (arr.ndim - 1))


def _inproj(x, mod, l, g, w_in_packed, tm, rows_per_seq):
    t = x.shape[0]
    row = lambda n: pl.BlockSpec((tm, n), lambda i: (i, 0))
    wcol = lambda width, start: pl.BlockSpec((None, D_MODEL, width), lambda i: (l, 0, start // width))
    u_start = QKV_WIDTH + GDN_WIDTH
    return pl.pallas_call(
        _inproj_kernel,
        grid=(t // tm,),
        in_specs=[row(D_MODEL), _mod_spec(mod, l, 1, tm, rows_per_seq), _mod_spec(mod, l, 0, tm, rows_per_seq),
                  _layer_spec(g, l), wcol(QKV_WIDTH, 0), wcol(GDN_WIDTH, QKV_WIDTH),
                  wcol(LANES, u_start + S5_WIDTH), wcol(S5_WIDTH, u_start)],
        out_specs=[row(QKV_WIDTH), row(GDN_WIDTH), row(LANES), row(S5_WIDTH)],
        out_shape=[jax.ShapeDtypeStruct((t, QKV_WIDTH), F32), jax.ShapeDtypeStruct((t, GDN_WIDTH), F32),
                   jax.ShapeDtypeStruct((t, LANES), F32), jax.ShapeDtypeStruct((t, S5_WIDTH), F32)],
        compiler_params=_cparams(("arbitrary",)),
        name="inproj",
    )(x, mod, mod, g, w_in_packed, w_in_packed, w_in_packed, w_in_packed)


def _gdn_gates(ba, alog, dtb):
    beta = _sigmoid(ba)
    g = -jnp.exp(alog) * _softplus(ba + dtb)
    return beta, g


def _l2n(x):
    return x * lax.rsqrt(jnp.sum(x * x, axis=-1, keepdims=True) + NORM_EPS)


def _gated_norm(o, gn, z):
    on = o * lax.rsqrt(jnp.mean(o * o, axis=-1, keepdims=True) + NORM_EPS) * gn
    return on * _silu(z)


def _gdn_prompt_kernel(qkv_ref, ba_ref, z_ref, cw_ref, alog_ref, dtb_ref, gn_ref, conv0_ref, s0_ref,
                       o_ref, sout_ref, convout_ref, xp_scr, y_scr, g_scr, b_scr, s_scr, wq_scr, ak_scr, u0_scr,
                       egl_scr, *, tm):
    c_len = GDN_CHUNK
    hd = GDN_HEAD_DIM
    pair_rows = GDN_CHUNKS_PER_ITER * c_len
    n_pairs = tm // pair_rows
    i = pl.program_id(1)
    last = pl.num_programs(1) - 1

    @pl.when(i == 0)
    def _():
        xp_scr[0:SUBLANES, :] = conv0_ref[0]
        s_scr[...] = s0_ref[0]

    xp_scr[SUBLANES:SUBLANES + tm, :] = qkv_ref[...]
    cw = cw_ref[...]
    cw_rows = [cw[j:j + 1, :].reshape(1, 1, QKV_WIDTH) for j in range(CONV_WIDTH)]
    sub = lax.broadcasted_iota(jnp.int32, (1, SUBLANES, QKV_WIDTH), 1)

    def conv_rows(r0, n_rows):
        x3 = xp_scr[r0:r0 + n_rows + SUBLANES, :].reshape(n_rows // SUBLANES + 1, SUBLANES, QKV_WIDTH)

        def delayed(s):
            rot = pltpu.roll(x3, s, axis=1)
            return jnp.where(sub >= s, rot[1:], rot[:-1])

        y = delayed(3) * cw_rows[0]
        y = y + delayed(2) * cw_rows[1]
        y = y + delayed(1) * cw_rows[2]
        y = y + x3[1:] * cw_rows[3]
        y_scr[r0:r0 + n_rows, :] = _silu(y).reshape(n_rows, QKV_WIDTH)

    beta, g = _gdn_gates(ba_ref[...], alog_ref[...], dtb_ref[...])
    b_scr[...] = beta
    g_scr[...] = g

    r = lax.broadcasted_iota(jnp.int32, (c_len, c_len), 0)
    c = lax.broadcasted_iota(jnp.int32, (c_len, c_len), 1)
    ge = r >= c
    gt = r > c
    tri = jnp.where(ge, 1.0, 0.0).astype(BF16)
    eye = jnp.where(r == c, 1.0, 0.0).astype(F32)
    blk16 = (r // 16) == (c // 16)
    pair_masks = [((r // (2 * s)) == (c // (2 * s))) & ((r // s) != (c // s)) for s in (16, 32, 64)]
    gn = gn_ref[...]
    scale = hd ** -0.5

    def prep_stages(p):
        chains = []

        def load():
            for ck in range(GDN_CHUNKS_PER_ITER):
                ci = p * GDN_CHUNKS_PER_ITER + ck
                rows = slice(ci * c_len, (ci + 1) * c_len)
                gch = g_scr[rows, :]
                bch = b_scr[rows, :]
                g1 = gch.astype(BF16)
                r1 = gch - g1.astype(F32)
                g2 = r1.astype(BF16)
                g3 = (r1 - g2.astype(F32)).astype(BF16)
                gcum = (jnp.dot(tri, g1, preferred_element_type=F32)
                        + jnp.dot(tri, g2, preferred_element_type=F32)
                        + jnp.dot(tri, g3, preferred_element_type=F32))
                gcum_t = gcum.T
                glast = gcum[c_len - 1:c_len, :]
                egl_scr[ci] = jnp.broadcast_to(jnp.exp(glast), (SUBLANES, LANES))
                for h in range(GDN_HEADS):
                    lo = h * hd
                    q = _l2n(y_scr[rows, lo:lo + hd]) * scale
                    k = _l2n(y_scr[rows, GDN_WIDTH + lo:GDN_WIDTH + lo + hd])
                    v = y_scr[rows, 2 * GDN_WIDTH + lo:2 * GDN_WIDTH + lo + hd]
                    gl = GDN_HEADS + h
                    gcb = jnp.broadcast_to(gcum[:, gl:gl + 1], (c_len, c_len))
                    bcol = jnp.broadcast_to(bch[:, h:h + 1], (c_len, c_len))
                    egc = jnp.exp(gcb)
                    kdf = jnp.exp(glast[:, gl:gl + 1] - gcb)
                    decay = jnp.where(ge, jnp.exp(gcb - gcum_t[gl:gl + 1, :]), 0.0)
                    kb = k.astype(BF16)
                    chains.append(dict(
                        ci=ci, h=h, decay=decay, bcol=bcol, kb=kb,
                        kq=jnp.concatenate([kb, q.astype(BF16)], axis=0),
                        rhs=jnp.concatenate([((bcol * egc) * k).astype(BF16), (bcol * v).astype(BF16)], axis=1),
                        qg=(q * egc).astype(BF16),
                        kdt=(k * kdf).T.astype(BF16)))

        def gram():
            for ch in chains:
                ch["kkqk"] = lax.dot_general(ch["kq"], ch["kb"], (((1,), (1,)), ((), ())),
                                             preferred_element_type=F32)

        def neumann0():
            for ch in chains:
                ch["lmat"] = jnp.where(gt, ch["bcol"] * ch["kkqk"][:c_len] * ch["decay"], 0.0)
                ch["n1"] = jnp.where(blk16, -ch["lmat"], 0.0)
                ch["t"] = eye + ch["n1"]
            for ch in chains:
                ch["n2"] = _mm(ch["n1"], ch["n1"])

        def neumann1():
            for ch in chains:
                ch["n4"] = _mm(ch["n2"], ch["n2"])
                ch["t"] = ch["t"] + _mm(ch["t"], ch["n2"])

        def neumann2():
            for ch in chains:
                ch["n8"] = _mm(ch["n4"], ch["n4"])
                ch["t"] = ch["t"] + _mm(ch["t"], ch["n4"])

        def neumann3():
            for ch in chains:
                ch["t"] = ch["t"] + _mm(ch["t"], ch["n8"])

        def merge_a(pm):
            def run():
                for ch in chains:
                    ch["x"] = _mm(ch["t"], jnp.where(pm, ch["lmat"], 0.0))
            return run

        def merge_b():
            for ch in chains:
                ch["t"] = ch["t"] - _mm(ch["x"], ch["t"])

        def finish():
            for ch in chains:
                wu = jnp.dot(ch["t"].astype(BF16), ch["rhs"], preferred_element_type=F32)
                ci, h = ch["ci"], ch["h"]
                wq_scr[ci, h] = jnp.concatenate([wu[:, :hd].astype(BF16), ch["qg"]], axis=0)
                u0_scr[ci, h] = wu[:, hd:]
                ak_scr[ci, h] = jnp.concatenate([(ch["kkqk"][c_len:] * ch["decay"]).astype(BF16), ch["kdt"]],
                                                axis=0)

        stages = [load, gram, neumann0, neumann1, neumann2, neumann3]
        for pm in pair_masks:
            stages += [merge_a(pm), merge_b]
        return stages + [finish]

    def recur_stages(ci):
        rows = slice(ci * c_len, (ci + 1) * c_len)
        heads = range(GDN_HEADS)
        st = {}

        def first():
            st["ss"] = [s_scr[h] for h in heads]
            st["wsqs"] = [jnp.dot(wq_scr[ci, h], st["ss"][h].astype(BF16), preferred_element_type=F32)
                          for h in heads]

        def second():
            us = [(u0_scr[ci, h] - st["wsqs"][h][:c_len]).astype(BF16) for h in heads]
            st["auku"] = [jnp.dot(ak_scr[ci, h], us[h], preferred_element_type=F32) for h in heads]

        def third():
            egl = egl_scr[ci]
            for h in heads:
                lo = h * hd
                gl = GDN_HEADS + h
                o = st["wsqs"][h][c_len:] + st["auku"][h][:c_len]
                s_scr[h] = egl[0:1, gl:gl + 1] * st["ss"][h] + st["auku"][h][c_len:]
                o_ref[rows, lo:lo + hd] = _gated_norm(o, gn, z_ref[rows, lo:lo + hd]).astype(o_ref.dtype)

        return [first, second, third]

    def run_interleaved(main, early, side):
        n_main = len(main)
        done = 0
        for idx, stage in enumerate(main):
            stage()
            if idx == 0:
                for extra in early:
                    extra()
            want = (len(side) * (idx + 1)) // n_main
            while done < want:
                side[done]()
                done += 1

    conv_rows(0, pair_rows)
    for p in range(n_pairs):
        early = []
        side = []
        if p + 1 < n_pairs:
            early.append(functools.partial(conv_rows, (p + 1) * pair_rows, pair_rows))
        if p > 0:
            for ck in range(GDN_CHUNKS_PER_ITER):
                side += recur_stages((p - 1) * GDN_CHUNKS_PER_ITER + ck)
        run_interleaved(prep_stages(p), early, side)
    for ck in range(GDN_CHUNKS_PER_ITER):
        for stage in recur_stages((n_pairs - 1) * GDN_CHUNKS_PER_ITER + ck):
            stage()

    tail = xp_scr[tm:tm + SUBLANES, :]
    xp_scr[0:SUBLANES, :] = tail

    @pl.when(i == last)
    def _():
        convout_ref[0] = tail
        sout_ref[0] = s_scr[...]


def _gdn_prompt(qkv, ba, z, l, cw, alog, dtb, gn, conv0, s0, tm):
    n_seq = s0.shape[0]
    t = qkv.shape[0]
    nt = t // n_seq // tm
    row = lambda n: pl.BlockSpec((tm, n), lambda b, i: (b * nt + i, 0))
    lay = lambda a: _layer_spec(a, l)
    hd = GDN_HEAD_DIM
    return pl.pallas_call(
        functools.partial(_gdn_prompt_kernel, tm=tm),
        grid=(n_seq, nt),
        in_specs=[row(QKV_WIDTH), row(LANES), row(GDN_WIDTH), lay(cw), lay(alog), lay(dtb), lay(gn),
                  pl.BlockSpec((1, SUBLANES, QKV_WIDTH), lambda b, i: (b, 0, 0)),
                  pl.BlockSpec((1, GDN_HEADS, hd, hd), lambda b, i: (b, 0, 0, 0))],
        out_specs=[row(GDN_WIDTH),
                   pl.BlockSpec((1, GDN_HEADS, hd, hd), lambda b, i: (b, 0, 0, 0)),
                   pl.BlockSpec((1, SUBLANES, QKV_WIDTH), lambda b, i: (b, 0, 0))],
        out_shape=[jax.ShapeDtypeStruct((t, GDN_WIDTH), BF16),
                   jax.ShapeDtypeStruct((n_seq, GDN_HEADS, hd, hd), F32),
                   jax.ShapeDtypeStruct((n_seq, SUBLANES, QKV_WIDTH), F32)],
        scratch_shapes=[pltpu.VMEM((tm + SUBLANES, QKV_WIDTH), F32), pltpu.VMEM((tm, QKV_WIDTH), F32),
                        pltpu.VMEM((tm, LANES), F32), pltpu.VMEM((tm, LANES), F32),
                        pltpu.VMEM((GDN_HEADS, hd, hd), F32),
                        pltpu.VMEM((tm // GDN_CHUNK, GDN_HEADS, 2 * GDN_CHUNK, hd), BF16),
                        pltpu.VMEM((tm // GDN_CHUNK, GDN_HEADS, 2 * GDN_CHUNK, hd), BF16),
                        pltpu.VMEM((tm // GDN_CHUNK, GDN_HEADS, GDN_CHUNK, hd), F32),
                        pltpu.VMEM((tm // GDN_CHUNK, SUBLANES, LANES), F32)],
        compiler_params=_cparams(("arbitrary", "arbitrary")),
        name="gdn_prompt",
    )(qkv, ba, z, cw, alog, dtb, gn, conv0, s0)


def _gdn_step_kernel(qkv_ref, ba_ref, z_ref, cw_ref, alog_ref, dtb_ref, gn_ref, conv_ref, s_ref,
                     o_ref, sout_ref, convout_ref):
    nb = SUBLANES
    hd = GDN_HEAD_DIM
    x = qkv_ref[...]
    cb = conv_ref[...]
    cw = cw_ref[...]
    b0 = cb[:, 0:QKV_WIDTH]
    b1 = cb[:, QKV_WIDTH:2 * QKV_WIDTH]
    b2 = cb[:, 2 * QKV_WIDTH:3 * QKV_WIDTH]
    y = b0 * cw[0:1, :]
    y = y + b1 * cw[1:2, :]
    y = y + b2 * cw[2:3, :]
    y = y + x * cw[3:4, :]
    y = _silu(y)
    convout_ref[...] = jnp.concatenate([b1, b2, x], axis=1)

    beta, g = _gdn_gates(ba_ref[...], alog_ref[...], dtb_ref[...])
    a = jnp.exp(g)
    gn = gn_ref[...]
    zpad = jnp.zeros((hd - nb, hd), F32)
    for h in range(GDN_HEADS):
        lo = h * hd
        q = _l2n(y[:, lo:lo + hd]) * (hd ** -0.5)
        k = _l2n(y[:, GDN_WIDTH + lo:GDN_WIDTH + lo + hd])
        v = y[:, 2 * GDN_WIDTH + lo:2 * GDN_WIDTH + lo + hd]
        kt = jnp.concatenate([k, zpad], axis=0).T
        qt = jnp.concatenate([q, zpad], axis=0).T
        kq = jnp.sum(k * q, axis=-1, keepdims=True)
        bh = beta[:, h:h + 1]
        ah = a[:, GDN_HEADS + h:GDN_HEADS + h + 1]
        o_rows = []
        for n in range(nb):
            s = s_ref[n, h]
            kc = kt[:, n:n + 1]
            qc = qt[:, n:n + 1]
            rk = jnp.sum(s * kc, axis=0, keepdims=True)
            rq = jnp.sum(s * qc, axis=0, keepdims=True)
            an = ah[n:n + 1, :]
            un = bh[n:n + 1, :] * (v[n:n + 1, :] - an * rk)
            sout_ref[n, h] = an * s + kc * un
            o_rows.append(an * rq + kq[n:n + 1, :] * un)
        o = jnp.concatenate(o_rows, axis=0)
        o_ref[:, lo:lo + hd] = _gated_norm(o, gn, z_ref[:, lo:lo + hd])


def _gdn_step(qkv, ba, z, l, cw, alog, dtb, gn, conv, s):
    n_seq = qkv.shape[0]
    nb = SUBLANES
    hd = GDN_HEAD_DIM
    row = lambda n: pl.BlockSpec((nb, n), lambda i: (i, 0))
    lay = lambda a: _layer_spec(a, l)
    sspec = pl.BlockSpec((nb, GDN_HEADS, hd, hd), lambda i: (i, 0, 0, 0))
    return pl.pallas_call(
        _gdn_step_kernel,
        grid=(n_seq // nb,),
        in_specs=[row(QKV_WIDTH), row(LANES), row(GDN_WIDTH), lay(cw), lay(alog), lay(dtb), lay(gn),
                  pl.BlockSpec((None, nb, 3 * QKV_WIDTH), lambda i: (l, i, 0)),
                  pl.BlockSpec((None, nb, GDN_HEADS, hd, hd), lambda i: (l, i, 0, 0, 0))],
        out_specs=[row(GDN_WIDTH), sspec, row(3 * QKV_WIDTH)],
        out_shape=[jax.ShapeDtypeStruct((n_seq, GDN_WIDTH), F32),
                   jax.ShapeDtypeStruct((n_seq, GDN_HEADS, hd, hd), F32),
                   jax.ShapeDtypeStruct((n_seq, 3 * QKV_WIDTH), F32)],
        compiler_params=_cparams(("arbitrary",)),
        name="gdn_step",
    )(qkv, ba, z, cw, alog, dtb, gn, conv, s)


def _s5_param_kernel(lre_ref, lim_ref, ldt_ref, bre_ref, bim_ref, are_ref, aim_ref, bbre_ref, bbim_ref):
    lre = lre_ref[...]
    lim = lim_ref[...]
    dt = jnp.exp(ldt_ref[...])
    mag = jnp.exp(lre * dt)
    are = mag * jnp.cos(lim * dt)
    aim = mag * jnp.sin(lim * dt)
    are_ref[...] = are
    aim_ref[...] = aim
    nre = are - 1.0
    den = lre * lre + lim * lim
    cre = (nre * lre + aim * lim) / den
    cim = (aim * lre - nre * lim) / den
    cre = cre[:, None, :]
    cim = cim[:, None, :]
    bre = bre_ref[...]
    bim = bim_ref[...]
    bbre_ref[...] = cre * bre - cim * bim
    bbim_ref[...] = cre * bim + cim * bre


def _s5_params(lam_re, lam_im, log_dt, b_re, b_im):
    p = lam_re.shape[-1]
    lam_re = lam_re.reshape(-1, p)
    lam_im = lam_im.reshape(-1, p)
    g = lam_re.shape[0]
    bt_re = jnp.swapaxes(b_re.reshape((g,) + b_re.shape[2:]), 1, 2)
    bt_im = jnp.swapaxes(b_im.reshape((g,) + b_im.shape[2:]), 1, 2)
    cg = bt_re.shape[1]
    return pl.pallas_call(
        _s5_param_kernel,
        out_shape=[jax.ShapeDtypeStruct((g, p), F32), jax.ShapeDtypeStruct((g, p), F32),
                   jax.ShapeDtypeStruct((g, cg, p), F32), jax.ShapeDtypeStruct((g, cg, p), F32)],
        name="s5_params",
    )(lam_re, lam_im, log_dt.reshape(g, 1), bt_re, bt_im)


def _slab_blockdiag(m):
    g, cg, p = m.shape
    gl = S5_GROUPS // S5_SLABS
    m4 = m.reshape(g // gl, gl, cg, p)
    eye = jnp.eye(gl, dtype=m.dtype)
    return jnp.einsum('igcp,gh->igchp', m4, eye).reshape(g // S5_GROUPS, S5_SLABS, gl * cg, gl * p)


def _slab_vec(v):
    g, p = v.shape
    return v.reshape(g // S5_GROUPS, S5_SLABS, (S5_GROUPS // S5_SLABS) * p)


def _gelu_tanh(x):
    return 0.5 * x * (1.0 + jnp.tanh(math.sqrt(2.0 / math.pi) * (x + 0.044715 * (x * x * x))))


def _s5_kernel(u_ref, x0_ref, a_ref, bmat_ref, cre_ref, cim_ref, d_ref, gluw_ref, glub_ref, ng_ref,
               o_ref, xout_ref, utb_scr, xs_scr, x_scr, y_scr, ab_scr, *, nb, tt, interleave, n_sub):
    rows = nb * tt
    sub_tt = tt // n_sub
    sub_rows = nb * sub_tt
    ss = S5_SLAB_STATE
    i = pl.program_id(0)

    @pl.when(i == 0)
    def _():
        x_scr[...] = x0_ref[...]
        ab_scr[...] = jnp.broadcast_to(a_ref[...], ab_scr.shape)

    if interleave:
        for b in range(nb):
            for s in range(S5_SLABS):
                utb_scr[pl.ds(s * rows + b, tt, stride=nb), :] = u_ref[b, :, s * LANES:(s + 1) * LANES]
    else:
        for s in range(S5_SLABS):
            utb_scr[s * rows:(s + 1) * rows, :] = u_ref[:, s * LANES:(s + 1) * LANES]

    def u_rows(s, h):
        return utb_scr[s * rows + h * sub_rows:s * rows + (h + 1) * sub_rows, :]

    def in_proj_ops(h):
        def make(s):
            def run():
                xs_scr[h * sub_rows:(h + 1) * sub_rows, s * 2 * ss:(s + 1) * 2 * ss] = jnp.dot(
                    u_rows(s, h).astype(BF16), bmat_ref[s], preferred_element_type=F32)
            return run
        return [make(s) for s in range(S5_SLABS)]

    def scan_ops(h):
        def make(t):
            def run():
                rs = slice(t * nb, (t + 1) * nb)
                for s in range(S5_SLABS):
                    lo = s * 2 * ss
                    ar = ab_scr[:, lo:lo + ss]
                    ai = ab_scr[:, lo + ss:lo + 2 * ss]
                    xr = x_scr[:, lo:lo + ss]
                    xi = x_scr[:, lo + ss:lo + 2 * ss]
                    nr = (ar * xr - ai * xi) + xs_scr[rs, lo:lo + ss]
                    ni = (ar * xi + ai * xr) + xs_scr[rs, lo + ss:lo + 2 * ss]
                    x_scr[:, lo:lo + ss] = nr
                    x_scr[:, lo + ss:lo + 2 * ss] = ni
                    xs_scr[rs, lo:lo + ss] = nr
                    xs_scr[rs, lo + ss:lo + 2 * ss] = ni
            return run
        return [make(t) for t in range(h * sub_tt, (h + 1) * sub_tt)]

    def out_proj_ops(h):
        rs = slice(h * sub_rows, (h + 1) * sub_rows)
        ys = []

        def make(s):
            def run():
                lo = s * 2 * ss
                yr = jnp.dot(xs_scr[rs, lo:lo + ss].astype(BF16), cre_ref[s], preferred_element_type=F32)
                yi = jnp.dot(xs_scr[rs, lo + ss:lo + 2 * ss].astype(BF16), cim_ref[s], preferred_element_type=F32)
                ys.append((yr - yi) + d_ref[0:1, s * LANES:(s + 1) * LANES] * u_rows(s, h))
            return run

        def finish():
            y = _gelu_tanh(jnp.concatenate(ys, axis=1))
            y = y * _sigmoid(_mm(y, gluw_ref[...]) + glub_ref[...])
            y = y * lax.rsqrt(jnp.mean(y * y, axis=-1, keepdims=True) + NORM_EPS) * ng_ref[...]
            if interleave:
                for s in range(S5_SLABS):
                    y_scr[s * rows + h * sub_rows:s * rows + (h + 1) * sub_rows, :] = y[:, s * LANES:(s + 1) * LANES]
            else:
                o_ref[rs, :] = y

        return [make(s) for s in range(S5_SLABS)] + [finish]

    def run_interleaved(main, side):
        done = 0
        for idx, op in enumerate(main):
            op()
            want = (len(side) * (idx + 1)) // len(main)
            while done < want:
                side[done]()
                done += 1

    for op in in_proj_ops(0):
        op()
    for h in range(n_sub):
        side = in_proj_ops(h + 1) if h + 1 < n_sub else []
        if h > 0:
            side = side + out_proj_ops(h - 1)
        run_interleaved(scan_ops(h), side)
    for op in out_proj_ops(n_sub - 1):
        op()

    @pl.when(i == pl.num_programs(0) - 1)
    def _():
        xout_ref[...] = x_scr[...]

    if interleave:
        for b in range(nb):
            for s in range(S5_SLABS):
                o_ref[b, :, s * LANES:(s + 1) * LANES] = y_scr[pl.ds(s * rows + b, tt, stride=nb), :].astype(
                    o_ref.dtype)


def _s5(u, x0, l, a, bmat, cre, cim, d, gluw, glub, ng, nb, tt, interleave, n_sub=1):
    rows = nb * tt
    nstate = x0.shape[1]
    full = lambda arr: pl.BlockSpec(arr.shape, lambda i: (0,) * arr.ndim)
    lay = lambda arr: _layer_spec(arr, l)
    if interleave:
        steps = u.shape[1] // tt
        uspec = pl.BlockSpec((nb, tt, S5_WIDTH), lambda i: (0, i, 0))
        oshape = jax.ShapeDtypeStruct(u.shape, BF16)
    else:
        steps = 1
        uspec = pl.BlockSpec((nb, S5_WIDTH), lambda i: (0, 0))
        oshape = jax.ShapeDtypeStruct(u.shape, F32)
    return pl.pallas_call(
        functools.partial(_s5_kernel, nb=nb, tt=tt, interleave=interleave, n_sub=n_sub),
        grid=(steps,),
        in_specs=[uspec, full(x0), lay(a), lay(bmat), lay(cre), lay(cim), lay(d), lay(gluw), lay(glub),
                  lay(ng)],
        out_specs=[uspec, full(x0)],
        out_shape=[oshape, jax.ShapeDtypeStruct(x0.shape, F32)],
        scratch_shapes=[pltpu.VMEM((S5_SLABS * rows, LANES), F32), pltpu.VMEM((rows, nstate), F32),
                        pltpu.VMEM((nb, nstate), F32), pltpu.VMEM((S5_SLABS * rows, LANES), F32),
                        pltpu.VMEM((nb, nstate), F32)],
        compiler_params=_cparams(("arbitrary",)),
        name="s5_scan",
    )(u, x0, a, bmat, cre, cim, d, gluw, glub, ng)


def _router(logits_t, bias_col):
    scores = _sigmoid(logits_t)
    sel = scores + bias_col
    s = [sel[e:e + 1, :] for e in range(N_EXPERTS)]
    n_groups = N_EXPERTS // EXPERTS_PER_GROUP
    gs = []
    for gi in range(n_groups):
        m = s[gi * EXPERTS_PER_GROUP: (gi + 1) * EXPERTS_PER_GROUP]
        best = None
        for p in range(EXPERTS_PER_GROUP):
            for q in range(p + 1, EXPERTS_PER_GROUP):
                ps = m[p] + m[q]
                best = ps if best is None else jnp.maximum(best, ps)
        gs.append(best)
    gmax = functools.reduce(jnp.maximum, gs)
    taken = None
    in_best = []
    for gi in range(n_groups):
        hit = gs[gi] == gmax
        if taken is None:
            cur = hit
            taken = hit
        else:
            cur = jnp.logical_and(hit, jnp.logical_not(taken))
            taken = jnp.logical_or(taken, hit)
        in_best.append(cur)
    selm = []
    picked = []
    for e in range(N_EXPERTS):
        gi = e // EXPERTS_PER_GROUP
        cnt = jnp.zeros_like(s[e])
        for j in range(gi * EXPERTS_PER_GROUP, (gi + 1) * EXPERTS_PER_GROUP):
            if j == e:
                continue
            beats = (s[j] >= s[e]) if j < e else (s[j] > s[e])
            cnt = cnt + jnp.where(beats, 1.0, 0.0)
        sel_e = jnp.logical_and(in_best[gi], cnt < 1.5)
        selm.append(jnp.where(sel_e, 1.0, 0.0))
        picked.append(jnp.where(sel_e, scores[e:e + 1, :], 0.0))
    denom = functools.reduce(lambda x, y: x + y, picked)
    comb = [p / denom for p in picked]
    return comb, selm, in_best


def _router_dense(logits_t, bias_col):
    comb, _, _ = _router(logits_t, bias_col)
    return jnp.concatenate(comb, axis=0)


def _router_sparse(logits_t, bias_col):
    comb, selm, in_best = _router(logits_t, bias_col)
    cls = None
    wlo = None
    whi = None
    for gi in range(N_EXPERTS // EXPERTS_PER_GROUP):
        term = jnp.where(in_best[gi], float(len(PAIR_CODES) * gi) - 1.0, 0.0)
        cls = term if cls is None else cls + term
        seen = None
        for j in range(EXPERTS_PER_GROUP):
            e = gi * EXPERTS_PER_GROUP + j
            cls = cls + selm[e] * PAIR_VALUE[j]
            first = selm[e] if seen is None else selm[e] * (1.0 - seen)
            seen = selm[e] if seen is None else jnp.maximum(seen, selm[e])
            lo_term = first * comb[e]
            hi_term = (selm[e] - first) * comb[e]
            wlo = lo_term if wlo is None else wlo + lo_term
            whi = hi_term if whi is None else whi + hi_term
    return cls, wlo, whi


def _outproj_kernel(oa_ref, ob_ref, x_ref, gt_ref, sc_ref, sh_ref, g_ref, wout_ref, wrt_ref, br_ref,
                    xo_ref, hn_ref, comb_ref, *, tm):
    o = jnp.concatenate([oa_ref[...].astype(BF16), ob_ref[...].astype(BF16)], axis=1)
    mix = jnp.dot(o, wout_ref[...], preferred_element_type=F32)
    x = x_ref[...] + gt_ref[...] * mix
    xo_ref[...] = x
    ms = jnp.mean(x * x, axis=-1, keepdims=True)
    hn = x * lax.rsqrt(ms + NORM_EPS) * g_ref[...]
    hn = hn * (1.0 + sc_ref[...]) + sh_ref[...]
    hb = hn.astype(BF16)
    hn_ref[...] = hb
    logits_t = lax.dot_general(wrt_ref[...], hb, (((1,), (1,)), ((), ())), preferred_element_type=F32)
    comb_t = _router_dense(logits_t, br_ref[...])
    pad = jnp.zeros((LANES - N_EXPERTS, tm), F32)
    comb_ref[...] = jnp.concatenate([comb_t, pad], axis=0).T


def _outproj(oa, ob, x, mod, l, g, wout, wrt, br, tm, rows_per_seq):
    t = x.shape[0]
    full = lambda a: pl.BlockSpec(a.shape, lambda i: (0,) * a.ndim)
    lay = lambda a: _layer_spec(a, l)
    row = lambda n: pl.BlockSpec((tm, n), lambda i: (i, 0))
    ms = lambda j: _mod_spec(mod, l, j, tm, rows_per_seq)
    return pl.pallas_call(
        functools.partial(_outproj_kernel, tm=tm),
        grid=(t // tm,),
        in_specs=[row(GDN_WIDTH), row(S5_WIDTH), row(D_MODEL), ms(2), ms(4), ms(3), lay(g), lay(wout),
                  full(wrt), full(br)],
        out_specs=[row(D_MODEL), row(D_MODEL), row(LANES)],
        out_shape=[jax.ShapeDtypeStruct((t, D_MODEL), F32), jax.ShapeDtypeStruct((t, D_MODEL), BF16),
                   jax.ShapeDtypeStruct((t, LANES), F32)],
        compiler_params=_cparams(("arbitrary",)),
        name="outproj_router",
    )(oa, ob, x, mod, mod, mod, g, wout, wrt, br)


def _outproj_sparse_kernel(oa_ref, ob_ref, x_ref, gt_ref, sc_ref, sh_ref, g_ref, wout_ref, wrt_ref, br_ref,
                           xo_ref, hx_ref, route_ref, cnt_ref, run_scr, *, tm):
    i = pl.program_id(0)

    @pl.when(i == 0)
    def _():
        run_scr[...] = jnp.zeros_like(run_scr)

    o = jnp.concatenate([oa_ref[...].astype(BF16), ob_ref[...].astype(BF16)], axis=1)
    mix = jnp.dot(o, wout_ref[...], preferred_element_type=F32)
    x = x_ref[...] + gt_ref[...] * mix
    xo_ref[...] = x
    ms = jnp.mean(x * x, axis=-1, keepdims=True)
    hn = x * lax.rsqrt(ms + NORM_EPS) * g_ref[...]
    hn = hn * (1.0 + sc_ref[...]) + sh_ref[...]
    hb = hn.astype(BF16)
    logits_t = lax.dot_general(wrt_ref[...], hb, (((1,), (1,)), ((), ())), preferred_element_type=F32)
    cls, wlo, whi = _router_sparse(logits_t, br_ref[...])
    sub = lax.broadcasted_iota(jnp.int32, (CLASS_ROWS, tm), 0).astype(F32)
    onehot = sub == cls
    r = lax.broadcasted_iota(jnp.int32, (tm, tm), 0)
    c = lax.broadcasted_iota(jnp.int32, (tm, tm), 1)
    before = jnp.where(r < c, 1.0, 0.0).astype(BF16)
    prefix = jnp.dot(jnp.where(onehot, 1.0, 0.0).astype(BF16), before, preferred_element_type=F32)
    run = run_scr[...]
    rank = jnp.sum(jnp.where(onehot, prefix + run[:, 0:1], 0.0), axis=0, keepdims=True)
    run_scr[...] = run + jnp.sum(jnp.where(onehot, 1.0, 0.0), axis=1, keepdims=True)
    cnt_ref[...] = run_scr[...]
    route = jnp.concatenate([cls, rank, wlo, whi, jnp.zeros((SUBLANES - 4, tm), F32)], axis=0)
    route_ref[...] = route
    hx_ref[:, :D_MODEL] = hn
    hx_ref[:, D_MODEL:] = jnp.concatenate([route, jnp.zeros((LANES - SUBLANES, tm), F32)], axis=0).T


def _outproj_sparse(oa, ob, x, mod, l, g, wout, wrt, br, tm, rows_per_seq):
    t = x.shape[0]
    full = lambda a: pl.BlockSpec(a.shape, lambda i: (0,) * a.ndim)
    lay = lambda a: _layer_spec(a, l)
    row = lambda n: pl.BlockSpec((tm, n), lambda i: (i, 0))
    ms = lambda j: _mod_spec(mod, l, j, tm, rows_per_seq)
    return pl.pallas_call(
        functools.partial(_outproj_sparse_kernel, tm=tm),
        grid=(t // tm,),
        in_specs=[row(GDN_WIDTH), row(S5_WIDTH), row(D_MODEL), ms(2), ms(4), ms(3), lay(g), lay(wout),
                  full(wrt), full(br)],
        out_specs=[row(D_MODEL), row(HX_WIDTH), pl.BlockSpec((SUBLANES, tm), lambda i: (0, i)),
                   pl.BlockSpec((CLASS_ROWS, LANES), lambda i: (0, 0))],
        out_shape=[jax.ShapeDtypeStruct((t, D_MODEL), F32), jax.ShapeDtypeStruct((t, HX_WIDTH), F32),
                   jax.ShapeDtypeStruct((SUBLANES, t), F32), jax.ShapeDtypeStruct((CLASS_ROWS, LANES), F32)],
        scratch_shapes=[pltpu.VMEM((CLASS_ROWS, LANES), F32)],
        compiler_params=_cparams(("arbitrary",)),
        name="outproj_route",
    )(oa, ob, x, mod, mod, mod, g, wout, wrt, br)


def _dispatch_kernel(pos_ref, tail_ref, hx_ref, sorted_ref, stage, zero_scr, sems, zsem, *, tm, n_steps):
    i = pl.program_id(0)
    slot = i % 2

    def tail_copy(cidx):
        start = pl.multiple_of(jnp.maximum(tail_ref[cidx], 0), MOE_TILE)
        return pltpu.make_async_copy(zero_scr, sorted_ref.at[pl.ds(start, MOE_TILE), :], zsem)

    @pl.when(i == 0)
    def _():
        zero_scr[...] = jnp.zeros_like(zero_scr)
        for cidx in range(2 * N_CLASSES):
            @pl.when(tail_ref[cidx] >= 0)
            def _():
                tail_copy(cidx).start()
        for cidx in range(2 * N_CLASSES):
            @pl.when(tail_ref[cidx] >= 0)
            def _():
                tail_copy(cidx).wait()

    def row_copy(s, r, p):
        return pltpu.make_async_copy(stage.at[s, pl.ds(r, 1), :], sorted_ref.at[pl.ds(p, 1), :], sems.at[s])

    def wait_slot(s):
        pltpu.make_async_copy(stage.at[s], sorted_ref.at[pl.ds(0, tm), :], sems.at[s]).wait()

    @pl.when(i >= 2)
    def _():
        wait_slot(slot)

    stage[slot] = hx_ref[...]
    base = i * tm

    def issue(g, carry):
        r8 = pl.multiple_of(g * SUBLANES, SUBLANES)
        for k in range(SUBLANES):
            row_copy(slot, r8 + k, pos_ref[base + r8 + k]).start(priority=k % 2)
        return carry

    lax.fori_loop(0, tm // SUBLANES, issue, 0)

    @pl.when(i == n_steps - 1)
    def _():
        wait_slot(slot)
        if n_steps > 1:
            wait_slot(1 - slot)


def _dispatch(pos, tail, hx, n_rows, tm):
    t = hx.shape[0]
    n_steps = t // tm
    return pl.pallas_call(
        functools.partial(_dispatch_kernel, tm=tm, n_steps=n_steps),
        grid_spec=pltpu.PrefetchScalarGridSpec(
            num_scalar_prefetch=2,
            grid=(n_steps,),
            in_specs=[pl.BlockSpec((tm, HX_WIDTH), lambda i, pos, tail: (i, 0))],
            out_specs=pl.BlockSpec(memory_space=pl.ANY),
            scratch_shapes=[pltpu.VMEM((2, tm, HX_WIDTH), F32), pltpu.VMEM((MOE_TILE, HX_WIDTH), F32),
                            pltpu.SemaphoreType.DMA((2,)), pltpu.SemaphoreType.DMA(())]),
        out_shape=jax.ShapeDtypeStruct((n_rows, HX_WIDTH), F32),
        compiler_params=_cparams(("arbitrary",)),
        name="moe_dispatch",
    )(pos, tail, hx)


def _moe_sorted_kernel(lo_ref, hi_ref, src_ref, nv_ref, xs_ref, wg_lo_ref, wu_lo_ref, wd_lo_ref, wg_hi_ref,
                       wu_hi_ref, wd_hi_ref, o_ref):
    j = pl.program_id(0)

    @pl.when(j < nv_ref[0])
    def _():
        xs = xs_ref[...]
        x = xs[:, :D_MODEL].astype(BF16)
        wlo = xs[:, D_MODEL + 2:D_MODEL + 3]
        whi = xs[:, D_MODEL + 3:D_MODEL + 4]
        dot = functools.partial(jnp.dot, preferred_element_type=F32)
        a_lo = _silu(dot(x, wg_lo_ref[...])) * dot(x, wu_lo_ref[...]) * wlo
        a_hi = _silu(dot(x, wg_hi_ref[...])) * dot(x, wu_hi_ref[...]) * whi
        o_ref[...] = dot(a_lo.astype(BF16), wd_lo_ref[...]) + dot(a_hi.astype(BF16), wd_hi_ref[...])

    @pl.when(j >= nv_ref[0])
    def _():
        o_ref[...] = jnp.zeros_like(o_ref)


def _moe_sorted(tile_lo, tile_hi, tile_src, n_valid, xs, l, wg, wu, wd):
    n_rows = xs.shape[0]
    n_tiles = n_rows // MOE_TILE
    wspec = lambda w, which: pl.BlockSpec(
        (None, None) + w.shape[2:], lambda j, lo, hi, src, nv: (l, (lo, hi)[which][j], 0, 0))
    return pl.pallas_call(
        _moe_sorted_kernel,
        grid_spec=pltpu.PrefetchScalarGridSpec(
            num_scalar_prefetch=4,
            grid=(n_tiles,),
            in_specs=[pl.BlockSpec((MOE_TILE, HX_WIDTH), lambda j, lo, hi, src, nv: (src[j], 0)),
                      wspec(wg, 0), wspec(wu, 0), wspec(wd, 0), wspec(wg, 1), wspec(wu, 1), wspec(wd, 1)],
            out_specs=pl.BlockSpec((MOE_TILE, D_MODEL), lambda j, lo, hi, src, nv: (j, 0))),
        out_shape=jax.ShapeDtypeStruct((n_rows, D_MODEL), F32),
        compiler_params=_cparams(("arbitrary",)),
        name="moe_sorted",
    )(tile_lo, tile_hi, tile_src, n_valid, xs, wg, wu, wd, wg, wu, wd)


def _combine_kernel(pos_ref, ys_ref, x_ref, gt_ref, fg_ref, o_ref, buf, sems, *, tm, n_steps, final_norm):
    i = pl.program_id(0)
    slot = i % 2

    def row_copy(s, r, p):
        return pltpu.make_async_copy(ys_ref.at[pl.ds(p, 1), :], buf.at[s, pl.ds(r, 1), :], sems.at[s])

    def issue(step, s):
        base = step * tm

        def body(g, carry):
            r8 = pl.multiple_of(g * SUBLANES, SUBLANES)
            for k in range(SUBLANES):
                row_copy(s, r8 + k, pos_ref[base + r8 + k]).start(priority=k % 2)
            return carry

        lax.fori_loop(0, tm // SUBLANES, body, 0)

    @pl.when(i == 0)
    def _():
        issue(0, 0)

    @pl.when(i + 1 < n_steps)
    def _():
        issue(i + 1, 1 - slot)

    pltpu.make_async_copy(ys_ref.at[pl.ds(0, tm), :], buf.at[slot], sems.at[slot]).wait()
    x = x_ref[...] + gt_ref[...] * buf[slot]
    if final_norm:
        x = x * lax.rsqrt(jnp.mean(x * x, axis=-1, keepdims=True) + NORM_EPS) * fg_ref[...]
    o_ref[...] = x


def _combine(pos, ys, x, mod, l, fg, tm, rows_per_seq, final_norm):
    t = x.shape[0]
    n_steps = t // tm
    return pl.pallas_call(
        functools.partial(_combine_kernel, tm=tm, n_steps=n_steps, final_norm=final_norm),
        grid_spec=pltpu.PrefetchScalarGridSpec(
            num_scalar_prefetch=1,
            grid=(n_steps,),
            in_specs=[pl.BlockSpec(memory_space=pl.ANY),
                      pl.BlockSpec((tm, D_MODEL), lambda i, pos: (i, 0)),
                      _mod_spec(mod, l, 5, tm, rows_per_seq),
                      pl.BlockSpec(fg.shape, lambda i, pos: (0, 0))],
            out_specs=pl.BlockSpec((tm, D_MODEL), lambda i, pos: (i, 0)),
            scratch_shapes=[pltpu.VMEM((2, tm, D_MODEL), F32), pltpu.SemaphoreType.DMA((2,))]),
        out_shape=jax.ShapeDtypeStruct((t, D_MODEL), F32),
        compiler_params=_cparams(("arbitrary",)),
        name="moe_combine",
    )(pos, ys, x, mod, fg)


def _route_plan(route, counts, n_tokens):
    cnt = counts[:N_CLASSES, 0].astype(jnp.int32)
    padded = ((cnt + MOE_TILE - 1) // MOE_TILE) * MOE_TILE
    ends = jnp.cumsum(padded)
    offsets = ends - padded
    cls = route[0].astype(jnp.int32)
    pos = route[1].astype(jnp.int32) + jnp.sum(
        jnp.where(cls[:, None] == jnp.arange(N_CLASSES, dtype=jnp.int32)[None, :], offsets[None, :], 0), axis=1)
    n_tiles = n_tokens // MOE_TILE + N_CLASSES
    n_valid = ends[-1] // MOE_TILE
    src = jnp.minimum(jnp.arange(n_tiles, dtype=jnp.int32), jnp.maximum(n_valid - 1, 0))
    tile_cls = jnp.minimum(jnp.sum((src[:, None] * MOE_TILE >= ends[None, :]).astype(jnp.int32), axis=1),
                           N_CLASSES - 1)
    group = tile_cls // len(PAIR_CODES)
    pair = jnp.asarray(PAIR_CODES, dtype=jnp.int32)[tile_cls % len(PAIR_CODES)]
    tile_lo = group * EXPERTS_PER_GROUP + pair[:, 0]
    tile_hi = group * EXPERTS_PER_GROUP + pair[:, 1]
    unused = (n_valid + jnp.arange(N_CLASSES, dtype=jnp.int32)) * MOE_TILE
    tail = jnp.concatenate([jnp.where(cnt > 0, ends - MOE_TILE, -1),
                            jnp.where(unused < n_tiles * MOE_TILE, unused, -1)]).astype(jnp.int32)
    return pos, tail, tile_lo, tile_hi, src, n_valid.reshape(1).astype(jnp.int32), n_tiles * MOE_TILE


def _moe_kernel(hn_ref, comb_ref, x_ref, gt_ref, wg_ref, wu_ref, wd_ref, fg_ref, o_ref, acc_scr, *, final_norm):
    e = pl.program_id(1)

    @pl.when(e == 0)
    def _():
        acc_scr[...] = jnp.zeros_like(acc_scr)

    hg = jnp.dot(hn_ref[...], wg_ref[...], preferred_element_type=F32)
    hu = jnp.dot(hn_ref[...], wu_ref[...], preferred_element_type=F32)
    lane = lax.broadcasted_iota(jnp.int32, (1, LANES), 1)
    ce = jnp.sum(jnp.where(lane == e, comb_ref[...], 0.0), axis=-1, keepdims=True)
    act = _silu(hg) * hu * ce
    acc_scr[...] += jnp.dot(act.astype(BF16), wd_ref[...], preferred_element_type=F32)

    @pl.when(e == pl.num_programs(1) - 1)
    def _():
        x = x_ref[...] + gt_ref[...] * acc_scr[...]
        if final_norm:
            x = x * lax.rsqrt(jnp.mean(x * x, axis=-1, keepdims=True) + NORM_EPS) * fg_ref[...]
        o_ref[...] = x


def _moe(hn, comb, x, mod, l, wg, wu, wd, fg, tm, rows_per_seq, final_norm):
    t = x.shape[0]
    row = lambda n: pl.BlockSpec((tm, n), lambda i, e: (i, 0))
    wspec = lambda w: pl.BlockSpec((None, None) + w.shape[2:], lambda i, e: (l, e, 0, 0))
    return pl.pallas_call(
        functools.partial(_moe_kernel, final_norm=final_norm),
        grid=(t // tm, N_EXPERTS),
        in_specs=[row(D_MODEL), row(LANES), row(D_MODEL), _mod_spec(mod, l, 5, tm, rows_per_seq),
                  wspec(wg), wspec(wu), wspec(wd),
                  pl.BlockSpec(fg.shape, lambda i, e: (0, 0))],
        out_specs=row(D_MODEL),
        out_shape=jax.ShapeDtypeStruct((t, D_MODEL), F32),
        scratch_shapes=[pltpu.VMEM((tm, D_MODEL), F32)],
        compiler_params=_cparams(("arbitrary", "arbitrary")),
        name="moe",
    )(hn, comb, x, mod, wg, wu, wd, fg)


def _pad_lanes(v, offset):
    return jnp.pad(v, ((0, 0), (offset, LANES - offset - v.shape[1])))[:, None, :]


def _state_to_slab(re, im):
    lead = re.shape[:2]
    r = re.reshape(lead + (S5_SLABS, S5_SLAB_STATE))
    i = im.reshape(lead + (S5_SLABS, S5_SLAB_STATE))
    return jnp.stack([r, i], axis=3).reshape(lead + (S5_SLABS * 2 * S5_SLAB_STATE,))


def _slab_to_state(x):
    lead = x.shape[:2]
    x4 = x.reshape(lead + (S5_SLABS, 2, S5_SLAB_STATE))
    re = x4[:, :, :, 0].reshape(lead + (S5_GROUPS, S5_STATE))
    im = x4[:, :, :, 1].reshape(lead + (S5_GROUPS, S5_STATE))
    return re, im


def kernel(x_prompt, x_sample, c_prompt, c_sample, state_conv, state_gdn, state_s5_re, state_s5_im, norm1_g, norm2_g, w_ada, b_ada, w_in, conv_w, a_log, dt_bias, gdn_norm_g, s5_lambda_re, s5_lambda_im, s5_log_dt, s5_b_re, s5_b_im, s5_c_re, s5_c_im, s5_d, s5_glu_w, s5_glu_b, s5_norm_g, w_out, w_router, b_router, w_gate, w_up, w_down, final_g):
    bp, seq, _ = x_prompt.shape
    bs = x_sample.shape[0]
    tp = bp * seq
    tm_p = 512
    tm_gdn = 1024 if seq % 1024 == 0 else tm_p
    tt = 128 if seq % 128 == 0 else seq

    mod = _ada(jnp.concatenate([c_prompt, c_sample], axis=0), w_ada, b_ada)
    mod_p = mod[:, :bp].reshape(DEPTH, bp, 1, 6 * D_MODEL)
    mod_s = mod[:, bp:]

    nba = 2 * GDN_HEADS
    u_start = QKV_WIDTH + GDN_WIDTH
    w_in_packed = jnp.concatenate(
        [w_in[:, :, :u_start], w_in[:, :, u_start + nba:], w_in[:, :, u_start:u_start + nba],
         jnp.zeros((DEPTH, D_MODEL, LANES - nba), F32)], axis=2).astype(BF16)
    g1 = norm1_g[:, None, :]
    g2 = norm2_g[:, None, :]
    alog = _pad_lanes(a_log, GDN_HEADS)
    dtb = _pad_lanes(dt_bias, GDN_HEADS)
    gn = gdn_norm_g[:, None, :]
    a_re, a_im, bb_re, bb_im = _s5_params(s5_lambda_re, s5_lambda_im, s5_log_dt, s5_b_re, s5_b_im)
    a_vec = jnp.concatenate([_slab_vec(a_re), _slab_vec(a_im)], axis=2).reshape(DEPTH, 1, -1)
    bmat = jnp.concatenate([_slab_blockdiag(bb_re), _slab_blockdiag(bb_im)], axis=3).astype(BF16)
    flat = lambda m: m.reshape((DEPTH * S5_GROUPS,) + m.shape[2:])
    cre = jnp.swapaxes(_slab_blockdiag(flat(s5_c_re)), 2, 3).astype(BF16)
    cim = jnp.swapaxes(_slab_blockdiag(flat(s5_c_im)), 2, 3).astype(BF16)
    dvec = s5_d[:, None, :]
    gluw = s5_glu_w.astype(BF16)
    glub = s5_glu_b[:, None, :]
    ng = s5_norm_g[:, None, :]
    wout = w_out.astype(BF16)
    wg = w_gate.astype(BF16)
    wu = w_up.astype(BF16)
    wd = w_down.astype(BF16)
    wrt = jnp.transpose(w_router).astype(BF16)
    br = b_router.reshape(N_EXPERTS, 1)
    fg = final_g.reshape(1, D_MODEL)
    conv_s = state_conv.reshape(DEPTH, bs, (CONV_WIDTH - 1) * QKV_WIDTH)

    xp = x_prompt.reshape(tp, D_MODEL)
    xs = x_sample.reshape(bs, D_MODEL)
    outs_p = {k: [] for k in ("conv", "gdn", "s5")}
    outs_s = {k: [] for k in ("conv", "gdn", "s5")}
    zero_conv = jnp.zeros((bp, SUBLANES, QKV_WIDTH), F32)
    zero_gdn = jnp.zeros((bp, GDN_HEADS, GDN_HEAD_DIM, GDN_HEAD_DIM), F32)
    zero_s5 = jnp.zeros((bp, S5_SLABS * 2 * S5_SLAB_STATE), F32)
    x0_s = _state_to_slab(state_s5_re, state_s5_im)

    for l in range(DEPTH):
        last = l == DEPTH - 1

        qkv, z, ba, u = _inproj(xp, mod_p, l, g1, w_in_packed, tm_p, seq)
        oa, sg, cv = _gdn_prompt(qkv, ba, z, l, conv_w, alog, dtb, gn, zero_conv, zero_gdn, tm_gdn)
        ob, xst = _s5(u.reshape(bp, seq, S5_WIDTH), zero_s5, l, a_vec, bmat, cre, cim, dvec, gluw, glub, ng,
                      bp, tt, True, n_sub=4)
        xp, hx, route, counts = _outproj_sparse(oa, ob.reshape(tp, S5_WIDTH), xp, mod_p, l, g2, wout, wrt, br,
                                                tm_p, seq)
        pos, tail, tile_lo, tile_hi, tile_src, n_valid, n_rows = _route_plan(route, counts, tp)
        xsorted = _dispatch(pos, tail, hx, n_rows, tm_p)
        ys = _moe_sorted(tile_lo, tile_hi, tile_src, n_valid, xsorted, l, wg, wu, wd)
        xp = _combine(pos, ys, xp, mod_p, l, fg, tm_p, seq, last)
        outs_p["conv"].append(cv)
        outs_p["gdn"].append(sg)
        outs_p["s5"].append(xst)

        qkv, z, ba, u = _inproj(xs, mod_s, l, g1, w_in_packed, bs, 1)
        oa, sg, cv = _gdn_step(qkv, ba, z, l, conv_w, alog, dtb, gn, conv_s, state_gdn)
        ob, xst = _s5(u, x0_s[l], l, a_vec, bmat, cre, cim, dvec, gluw, glub, ng, bs, 1, False)
        xs, hn, comb = _outproj(oa, ob, xs, mod_s, l, g2, wout, wrt, br, bs, 1)
        xs = _moe(hn, comb, xs, mod_s, l, wg, wu, wd, fg, bs, 1, last)
        outs_s["conv"].append(cv)
        outs_s["gdn"].append(sg)
        outs_s["s5"].append(xst)

    st = lambda d, k: jnp.stack(d[k])
    re_p, im_p = _slab_to_state(st(outs_p, "s5"))
    re_s, im_s = _slab_to_state(st(outs_s, "s5"))
    conv_p = st(outs_p, "conv")[:, :, SUBLANES - (CONV_WIDTH - 1):, :]
    conv_s_new = st(outs_s, "conv").reshape(DEPTH, bs, CONV_WIDTH - 1, QKV_WIDTH)
    return (xp.reshape(bp, seq, D_MODEL), xs.reshape(bs, 1, D_MODEL),
            conv_p, st(outs_p, "gdn"), re_p, im_p, conv_s_new, st(outs_s, "gdn"), re_s, im_s)
```

```python
import functools
import math

import jax
import jax.numpy as jnp
from jax import lax
from jax.experimental import pallas as pl
from jax.experimental.pallas import tpu as pltpu

F32 = jnp.float32
BF16 = jnp.bfloat16

D_MODEL = 1024
DEPTH = 2
GDN_HEAD_DIM = 128
GDN_WIDTH = 512
GDN_HEADS = 4
CONV_WIDTH = 4
S5_CH_PER_GROUP = 16
S5_WIDTH = 512
S5_GROUPS = 32
S5_STATE = 64
QKV_WIDTH = 3 * GDN_WIDTH
N_EXPERTS = 16
EXPERTS_PER_GROUP = 4
D_EXPERT = 256
NORM_EPS = 1e-6

LANES = 128
SUBLANES = 8
GDN_CHUNK = 128
GDN_CHUNKS_PER_ITER = 2
S5_SLABS = 4
S5_SLAB_STATE = 512
MOE_TILE = 512
PAIR_VALUE = (0.0, 1.0, 2.0, 4.0)
PAIR_CODES = ((0, 1), (0, 2), (1, 2), (0, 3), (1, 3), (2, 3))
N_CLASSES = (N_EXPERTS // EXPERTS_PER_GROUP) * len(PAIR_CODES)
CLASS_ROWS = 32
HX_WIDTH = D_MODEL + LANES
VMEM_LIMIT = 56 * 1024 * 1024


def _cparams(sem):
    return pltpu.CompilerParams(dimension_semantics=sem, vmem_limit_bytes=VMEM_LIMIT)


def _sigmoid(x):
    return 1.0 / (1.0 + jnp.exp(-x))


def _silu(x):
    return x * _sigmoid(x)


def _softplus(x):
    return jnp.maximum(x, 0.0) + jnp.log1p(jnp.exp(-jnp.abs(x)))


def _mm(a, b):
    return jnp.dot(a.astype(BF16), b.astype(BF16), preferred_element_type=F32)


def _ada_kernel(c_ref, w_ref, b_ref, o_ref):
    c = c_ref[...]
    o_ref[0] = _mm(_silu(c), w_ref[0]) + b_ref[0]


def _ada(c_all, w_ada, b_ada):
    rows = c_all.shape[0]
    n_out = w_ada.shape[-1]
    tn = 1536
    return pl.pallas_call(
        _ada_kernel,
        grid=(DEPTH, n_out // tn),
        in_specs=[
            pl.BlockSpec((rows, D_MODEL), lambda l, j: (0, 0)),
            pl.BlockSpec((1, D_MODEL, tn), lambda l, j: (l, 0, j)),
            pl.BlockSpec((1, 1, tn), lambda l, j: (l, 0, j)),
        ],
        out_specs=pl.BlockSpec((1, rows, tn), lambda l, j: (l, 0, j)),
        out_shape=jax.ShapeDtypeStruct((DEPTH, rows, n_out), F32),
        compiler_params=_cparams(("arbitrary", "arbitrary")),
        name="ada_mod",
    )(c_all, w_ada, b_ada.reshape(DEPTH, 1, n_out))


def _inproj_kernel(x_ref, sc_ref, sh_ref, g_ref, wqkv_ref, wz_ref, wba_ref, wu_ref,
                   qkv_ref, z_ref, ba_ref, u_ref):
    x = x_ref[...]
    ms = jnp.mean(x * x, axis=-1, keepdims=True)
    hn = x * lax.rsqrt(ms + NORM_EPS) * g_ref[...]
    hn = hn * (1.0 + sc_ref[...]) + sh_ref[...]
    hb = hn.astype(BF16)
    qkv_ref[...] = jnp.dot(hb, wqkv_ref[...], preferred_element_type=F32)
    z_ref[...] = jnp.dot(hb, wz_ref[...], preferred_element_type=F32)
    ba_ref[...] = jnp.dot(hb, wba_ref[...], preferred_element_type=F32)
    u_ref[...] = jnp.dot(hb, wu_ref[...], preferred_element_type=F32)


def _mod_spec(mod, l, j, tm, rows_per_seq, t):
    if mod.ndim == 4:
        tiles_per_seq = rows_per_seq // tm
        row0 = mod.shape[1] - t // rows_per_seq
        return pl.BlockSpec((None, None, 1, D_MODEL), lambda i, *_: (l, row0 + i // tiles_per_seq, 0, j))
    return pl.BlockSpec((None, tm, D_MODEL), lambda i, *_: (l, i, j))


def _layer_spec(arr, l):
    return pl.BlockSpec((None,) + arr.shape[1:], lambda *_: (l,) + (0,) * (arr.ndim - 1))


def _inproj(x, mod, l, g, w_in_packed, tm, rows_per_seq):
    t = x.shape[0]
    row = lambda n: pl.BlockSpec((tm, n), lambda i: (i, 0))
    wcol = lambda width, start: pl.BlockSpec((None, D_MODEL, width), lambda i: (l, 0, start // width))
    u_start = QKV_WIDTH + GDN_WIDTH
    return pl.pallas_call(
        _inproj_kernel,
        grid=(t // tm,),
        in_specs=[row(D_MODEL), _mod_spec(mod, l, 1, tm, rows_per_seq, t), _mod_spec(mod, l, 0, tm, rows_per_seq, t),
                  _layer_spec(g, l), wcol(QKV_WIDTH, 0), wcol(GDN_WIDTH, QKV_WIDTH),
                  wcol(LANES, u_start + S5_WIDTH), wcol(S5_WIDTH, u_start)],
        out_specs=[row(QKV_WIDTH), row(GDN_WIDTH), row(LANES), row(S5_WIDTH)],
        out_shape=[jax.ShapeDtypeStruct((t, QKV_WIDTH), F32), jax.ShapeDtypeStruct((t, GDN_WIDTH), F32),
                   jax.ShapeDtypeStruct((t, LANES), F32), jax.ShapeDtypeStruct((t, S5_WIDTH), F32)],
        compiler_params=_cparams(("arbitrary",)),
        name="inproj",
    )(x, mod, mod, g, w_in_packed, w_in_packed, w_in_packed, w_in_packed)


def _gdn_gates(ba, alog, dtb):
    beta = _sigmoid(ba)
    g = -jnp.exp(alog) * _softplus(ba + dtb)
    return beta, g


def _l2n(x):
    return x * lax.rsqrt(jnp.sum(x * x, axis=-1, keepdims=True) + NORM_EPS)


def _gated_norm(o, gn, z):
    on = o * lax.rsqrt(jnp.mean(o * o, axis=-1, keepdims=True) + NORM_EPS) * gn
    return on * _silu(z)


def _gdn_prompt_kernel(qkv_ref, ba_ref, z_ref, cw_ref, alog_ref, dtb_ref, gn_ref, conv0_ref, s0_ref,
                       o_ref, sout_ref, convout_ref, xp_scr, y_scr, g_scr, b_scr, s_scr, wq_scr, ak_scr, u0_scr,
                       egl_scr, *, tm):
    c_len = GDN_CHUNK
    hd = GDN_HEAD_DIM
    pair_rows = GDN_CHUNKS_PER_ITER * c_len
    n_pairs = tm // pair_rows
    i = pl.program_id(1)
    last = pl.num_programs(1) - 1

    @pl.when(i == 0)
    def _():
        xp_scr[0:SUBLANES, :] = conv0_ref[0]
        s_scr[...] = s0_ref[0]

    xp_scr[SUBLANES:SUBLANES + tm, :] = qkv_ref[...]
    cw = cw_ref[...]
    cw_rows = [cw[j:j + 1, :].reshape(1, 1, QKV_WIDTH) for j in range(CONV_WIDTH)]
    sub = lax.broadcasted_iota(jnp.int32, (1, SUBLANES, QKV_WIDTH), 1)

    def conv_rows(r0, n_rows):
        x3 = xp_scr[r0:r0 + n_rows + SUBLANES, :].reshape(n_rows // SUBLANES + 1, SUBLANES, QKV_WIDTH)

        def delayed(s):
            rot = pltpu.roll(x3, s, axis=1)
            return jnp.where(sub >= s, rot[1:], rot[:-1])

        y = delayed(3) * cw_rows[0]
        y = y + delayed(2) * cw_rows[1]
        y = y + delayed(1) * cw_rows[2]
        y = y + x3[1:] * cw_rows[3]
        y_scr[r0:r0 + n_rows, :] = _silu(y).reshape(n_rows, QKV_WIDTH)

    beta, g = _gdn_gates(ba_ref[...], alog_ref[...], dtb_ref[...])
    b_scr[...] = beta
    g_scr[...] = g

    r = lax.broadcasted_iota(jnp.int32, (c_len, c_len), 0)
    c = lax.broadcasted_iota(jnp.int32, (c_len, c_len), 1)
    ge = r >= c
    gt = r > c
    tri = jnp.where(ge, 1.0, 0.0).astype(BF16)
    eye = jnp.where(r == c, 1.0, 0.0).astype(F32)
    blk16 = (r // 16) == (c // 16)
    pair_masks = [((r // (2 * s)) == (c // (2 * s))) & ((r // s) != (c // s)) for s in (16, 32, 64)]
    gn = gn_ref[...]
    scale = hd ** -0.5

    def prep_stages(p):
        chains = []

        def load():
            for ck in range(GDN_CHUNKS_PER_ITER):
                ci = p * GDN_CHUNKS_PER_ITER + ck
                rows = slice(ci * c_len, (ci + 1) * c_len)
                gch = g_scr[rows, :]
                bch = b_scr[rows, :]
                g1 = gch.astype(BF16)
                r1 = gch - g1.astype(F32)
                g2 = r1.astype(BF16)
                g3 = (r1 - g2.astype(F32)).astype(BF16)
                gcum = (jnp.dot(tri, g1, preferred_element_type=F32)
                        + jnp.dot(tri, g2, preferred_element_type=F32)
                        + jnp.dot(tri, g3, preferred_element_type=F32))
                gcum_t = gcum.T
                glast = gcum[c_len - 1:c_len, :]
                egl_scr[ci] = jnp.broadcast_to(jnp.exp(glast), (SUBLANES, LANES))
                for h in range(GDN_HEADS):
                    lo = h * hd
                    q = _l2n(y_scr[rows, lo:lo + hd]) * scale
                    k = _l2n(y_scr[rows, GDN_WIDTH + lo:GDN_WIDTH + lo + hd])
                    v = y_scr[rows, 2 * GDN_WIDTH + lo:2 * GDN_WIDTH + lo + hd]
                    gl = GDN_HEADS + h
                    gcb = jnp.broadcast_to(gcum[:, gl:gl + 1], (c_len, c_len))
                    bcol = jnp.broadcast_to(bch[:, h:h + 1], (c_len, c_len))
                    egc = jnp.exp(gcb)
                    kdf = jnp.exp(glast[:, gl:gl + 1] - gcb)
                    decay = jnp.where(ge, jnp.exp(gcb - gcum_t[gl:gl + 1, :]), 0.0)
                    kb = k.astype(BF16)
                    chains.append(dict(
                        ci=ci, h=h, decay=decay, bcol=bcol, kb=kb,
                        kq=jnp.concatenate([kb, q.astype(BF16)], axis=0),
                        rhs=jnp.concatenate([((bcol * egc) * k).astype(BF16), (bcol * v).astype(BF16)], axis=1),
                        qg=(q * egc).astype(BF16),
                        kdt=(k * kdf).T.astype(BF16)))

        def gram():
            for ch in chains:
                ch["kkqk"] = lax.dot_general(ch["kq"], ch["kb"], (((1,), (1,)), ((), ())),
                                             preferred_element_type=F32)

        def neumann0():
            for ch in chains:
                ch["lmat"] = jnp.where(gt, ch["bcol"] * ch["kkqk"][:c_len] * ch["decay"], 0.0)
                ch["n1"] = jnp.where(blk16, -ch["lmat"], 0.0)
                ch["t"] = eye + ch["n1"]
            for ch in chains:
                ch["n2"] = _mm(ch["n1"], ch["n1"])

        def neumann1():
            for ch in chains:
                ch["n4"] = _mm(ch["n2"], ch["n2"])
                ch["t"] = ch["t"] + _mm(ch["t"], ch["n2"])

        def neumann2():
            for ch in chains:
                ch["n8"] = _mm(ch["n4"], ch["n4"])
                ch["t"] = ch["t"] + _mm(ch["t"], ch["n4"])

        def neumann3():
            for ch in chains:
                ch["t"] = ch["t"] + _mm(ch["t"], ch["n8"])

        def merge_a(pm):
            def run():
                for ch in chains:
                    ch["x"] = _mm(ch["t"], jnp.where(pm, ch["lmat"], 0.0))
            return run

        def merge_b():
            for ch in chains:
                ch["t"] = ch["t"] - _mm(ch["x"], ch["t"])

        def finish():
            for ch in chains:
                wu = jnp.dot(ch["t"].astype(BF16), ch["rhs"], preferred_element_type=F32)
                ci, h = ch["ci"], ch["h"]
                wq_scr[ci, h] = jnp.concatenate([wu[:, :hd].astype(BF16), ch["qg"]], axis=0)
                u0_scr[ci, h] = wu[:, hd:]
                ak_scr[ci, h] = jnp.concatenate([(ch["kkqk"][c_len:] * ch["decay"]).astype(BF16), ch["kdt"]],
                                                axis=0)

        stages = [load, gram, neumann0, neumann1, neumann2, neumann3]
        for pm in pair_masks:
            stages += [merge_a(pm), merge_b]
        return stages + [finish]

    def recur_stages(ci):
        rows = slice(ci * c_len, (ci + 1) * c_len)
        heads = range(GDN_HEADS)
        st = {}

        def first():
            st["ss"] = [s_scr[h] for h in heads]
            st["wsqs"] = [jnp.dot(wq_scr[ci, h], st["ss"][h].astype(BF16), preferred_element_type=F32)
                          for h in heads]

        def second():
            us = [(u0_scr[ci, h] - st["wsqs"][h][:c_len]).astype(BF16) for h in heads]
            st["auku"] = [jnp.dot(ak_scr[ci, h], us[h], preferred_element_type=F32) for h in heads]

        def third():
            egl = egl_scr[ci]
            for h in heads:
                lo = h * hd
                gl = GDN_HEADS + h
                o = st["wsqs"][h][c_len:] + st["auku"][h][:c_len]
                s_scr[h] = egl[0:1, gl:gl + 1] * st["ss"][h] + st["auku"][h][c_len:]
                o_ref[rows, lo:lo + hd] = _gated_norm(o, gn, z_ref[rows, lo:lo + hd]).astype(o_ref.dtype)

        return [first, second, third]

    def run_interleaved(main, early, side):
        n_main = len(main)
        done = 0
        for idx, stage in enumerate(main):
            stage()
            if idx == 0:
                for extra in early:
                    extra()
            want = (len(side) * (idx + 1)) // n_main
            while done < want:
                side[done]()
                done += 1

    conv_rows(0, pair_rows)
    for p in range(n_pairs):
        early = []
        side = []
        if p + 1 < n_pairs:
            early.append(functools.partial(conv_rows, (p + 1) * pair_rows, pair_rows))
        if p > 0:
            for ck in range(GDN_CHUNKS_PER_ITER):
                side += recur_stages((p - 1) * GDN_CHUNKS_PER_ITER + ck)
        run_interleaved(prep_stages(p), early, side)
    for ck in range(GDN_CHUNKS_PER_ITER):
        for stage in recur_stages((n_pairs - 1) * GDN_CHUNKS_PER_ITER + ck):
            stage()

    tail = xp_scr[tm:tm + SUBLANES, :]
    xp_scr[0:SUBLANES, :] = tail

    @pl.when(i == last)
    def _():
        convout_ref[0] = tail
        sout_ref[0] = s_scr[...]


def _gdn_prompt(qkv, ba, z, l, cw, alog, dtb, gn, conv0, s0, tm):
    n_seq = s0.shape[0]
    t = qkv.shape[0]
    nt = t // n_seq // tm
    row = lambda n: pl.BlockSpec((tm, n), lambda b, i: (b * nt + i, 0))
    lay = lambda a: _layer_spec(a, l)
    hd = GDN_HEAD_DIM
    return pl.pallas_call(
        functools.partial(_gdn_prompt_kernel, tm=tm),
        grid=(n_seq, nt),
        in_specs=[row(QKV_WIDTH), row(LANES), row(GDN_WIDTH), lay(cw), lay(alog), lay(dtb), lay(gn),
                  pl.BlockSpec((1, SUBLANES, QKV_WIDTH), lambda b, i: (b, 0, 0)),
                  pl.BlockSpec((1, GDN_HEADS, hd, hd), lambda b, i: (b, 0, 0, 0))],
        out_specs=[row(GDN_WIDTH),
                   pl.BlockSpec((1, GDN_HEADS, hd, hd), lambda b, i: (b, 0, 0, 0)),
                   pl.BlockSpec((1, SUBLANES, QKV_WIDTH), lambda b, i: (b, 0, 0))],
        out_shape=[jax.ShapeDtypeStruct((t, GDN_WIDTH), BF16),
                   jax.ShapeDtypeStruct((n_seq, GDN_HEADS, hd, hd), F32),
                   jax.ShapeDtypeStruct((n_seq, SUBLANES, QKV_WIDTH), F32)],
        scratch_shapes=[pltpu.VMEM((tm + SUBLANES, QKV_WIDTH), F32), pltpu.VMEM((tm, QKV_WIDTH), F32),
                        pltpu.VMEM((tm, LANES), F32), pltpu.VMEM((tm, LANES), F32),
                        pltpu.VMEM((GDN_HEADS, hd, hd), F32),
                        pltpu.VMEM((tm // GDN_CHUNK, GDN_HEADS, 2 * GDN_CHUNK, hd), BF16),
                        pltpu.VMEM((tm // GDN_CHUNK, GDN_HEADS, 2 * GDN_CHUNK, hd), BF16),
                        pltpu.VMEM((tm // GDN_CHUNK, GDN_HEADS, GDN_CHUNK, hd), F32),
                        pltpu.VMEM((tm // GDN_CHUNK, SUBLANES, LANES), F32)],
        compiler_params=_cparams(("arbitrary", "arbitrary")),
        name="gdn_prompt",
    )(qkv, ba, z, cw, alog, dtb, gn, conv0, s0)


def _gdn_step_kernel(qkv_ref, ba_ref, z_ref, cw_ref, alog_ref, dtb_ref, gn_ref, conv_ref, s_ref, *refs, first):
    nb = SUBLANES
    hd = GDN_HEAD_DIM
    if first:
        o_ref, sout_all_ref, convout_ref = refs
        sout_ref = sout_all_ref.at[0]
        for d in range(1, DEPTH):
            sout_all_ref[d] = jnp.zeros(sout_all_ref.shape[1:], F32)
    else:
        _, o_ref, sout_ref, convout_ref = refs
    x = qkv_ref[...]
    cb = conv_ref[...]
    cw = cw_ref[...]
    b0 = cb[:, 0:QKV_WIDTH]
    b1 = cb[:, QKV_WIDTH:2 * QKV_WIDTH]
    b2 = cb[:, 2 * QKV_WIDTH:3 * QKV_WIDTH]
    y = b0 * cw[0:1, :]
    y = y + b1 * cw[1:2, :]
    y = y + b2 * cw[2:3, :]
    y = y + x * cw[3:4, :]
    y = _silu(y)
    convout_ref[...] = jnp.concatenate([b1, b2, x], axis=1)

    beta, g = _gdn_gates(ba_ref[...], alog_ref[...], dtb_ref[...])
    a = jnp.exp(g)
    gn = gn_ref[...]
    zpad = jnp.zeros((hd - nb, hd), F32)
    for h in range(GDN_HEADS):
        lo = h * hd
        q = _l2n(y[:, lo:lo + hd]) * (hd ** -0.5)
        k = _l2n(y[:, GDN_WIDTH + lo:GDN_WIDTH + lo + hd])
        v = y[:, 2 * GDN_WIDTH + lo:2 * GDN_WIDTH + lo + hd]
        kt = jnp.concatenate([k, zpad], axis=0).T
        qt = jnp.concatenate([q, zpad], axis=0).T
        kq = jnp.sum(k * q, axis=-1, keepdims=True)
        bh = beta[:, h:h + 1]
        ah = a[:, GDN_HEADS + h:GDN_HEADS + h + 1]
        o_rows = []
        for n in range(nb):
            s = s_ref[n, h]
            kc = kt[:, n:n + 1]
            qc = qt[:, n:n + 1]
            rk = jnp.sum(s * kc, axis=0, keepdims=True)
            rq = jnp.sum(s * qc, axis=0, keepdims=True)
            an = ah[n:n + 1, :]
            un = bh[n:n + 1, :] * (v[n:n + 1, :] - an * rk)
            sout_ref[n, h] = an * s + kc * un
            o_rows.append(an * rq + kq[n:n + 1, :] * un)
        o = jnp.concatenate(o_rows, axis=0)
        o_ref[:, lo:lo + hd] = _gated_norm(o, gn, z_ref[:, lo:lo + hd])


def _gdn_step(qkv, ba, z, l, cw, alog, dtb, gn, conv, s, s_new):
    n_seq = qkv.shape[0]
    nb = SUBLANES
    hd = GDN_HEAD_DIM
    first = s_new is None
    row = lambda n: pl.BlockSpec((nb, n), lambda i: (i, 0))
    lay = lambda a: _layer_spec(a, l)
    if first:
        sspec = pl.BlockSpec((DEPTH, nb, GDN_HEADS, hd, hd), lambda i: (0, i, 0, 0, 0))
    else:
        sspec = pl.BlockSpec((None, nb, GDN_HEADS, hd, hd), lambda i: (l, i, 0, 0, 0))
    in_specs = [row(QKV_WIDTH), row(LANES), row(GDN_WIDTH), lay(cw), lay(alog), lay(dtb), lay(gn),
                pl.BlockSpec((None, nb, 3 * QKV_WIDTH), lambda i: (l, i, 0)),
                pl.BlockSpec((None, nb, GDN_HEADS, hd, hd), lambda i: (l, i, 0, 0, 0))]
    args = [qkv, ba, z, cw, alog, dtb, gn, conv, s]
    aliases = {}
    if not first:
        in_specs.append(pl.BlockSpec(memory_space=pl.ANY))
        args.append(s_new)
        aliases = {len(args) - 1: 1}
    return pl.pallas_call(
        functools.partial(_gdn_step_kernel, first=first),
        grid=(n_seq // nb,),
        in_specs=in_specs,
        out_specs=[row(GDN_WIDTH), sspec, row(3 * QKV_WIDTH)],
        out_shape=[jax.ShapeDtypeStruct((n_seq, GDN_WIDTH), F32),
                   jax.ShapeDtypeStruct((DEPTH, n_seq, GDN_HEADS, hd, hd), F32),
                   jax.ShapeDtypeStruct((n_seq, 3 * QKV_WIDTH), F32)],
        input_output_aliases=aliases,
        compiler_params=_cparams(("arbitrary",)),
        name="gdn_step",
    )(*args)


def _s5_param_kernel(lre_ref, lim_ref, ldt_ref, bre_ref, bim_ref, are_ref, aim_ref, bbre_ref, bbim_ref):
    lre = lre_ref[...]
    lim = lim_ref[...]
    dt = jnp.exp(ldt_ref[...])
    mag = jnp.exp(lre * dt)
    are = mag * jnp.cos(lim * dt)
    aim = mag * jnp.sin(lim * dt)
    are_ref[...] = are
    aim_ref[...] = aim
    nre = are - 1.0
    den = lre * lre + lim * lim
    cre = (nre * lre + aim * lim) / den
    cim = (aim * lre - nre * lim) / den
    cre = cre[:, None, :]
    cim = cim[:, None, :]
    bre = bre_ref[...]
    bim = bim_ref[...]
    bbre_ref[...] = cre * bre - cim * bim
    bbim_ref[...] = cre * bim + cim * bre


def _s5_params(lam_re, lam_im, log_dt, b_re, b_im):
    p = lam_re.shape[-1]
    lam_re = lam_re.reshape(-1, p)
    lam_im = lam_im.reshape(-1, p)
    g = lam_re.shape[0]
    bt_re = jnp.swapaxes(b_re.reshape((g,) + b_re.shape[2:]), 1, 2)
    bt_im = jnp.swapaxes(b_im.reshape((g,) + b_im.shape[2:]), 1, 2)
    cg = bt_re.shape[1]
    return pl.pallas_call(
        _s5_param_kernel,
        out_shape=[jax.ShapeDtypeStruct((g, p), F32), jax.ShapeDtypeStruct((g, p), F32),
                   jax.ShapeDtypeStruct((g, cg, p), F32), jax.ShapeDtypeStruct((g, cg, p), F32)],
        name="s5_params",
    )(lam_re, lam_im, log_dt.reshape(g, 1), bt_re, bt_im)


def _slab_blockdiag(m):
    g, cg, p = m.shape
    gl = S5_GROUPS // S5_SLABS
    m4 = m.reshape(g // gl, gl, cg, p)
    eye = jnp.eye(gl, dtype=m.dtype)
    return jnp.einsum('igcp,gh->igchp', m4, eye).reshape(g // S5_GROUPS, S5_SLABS, gl * cg, gl * p)


def _slab_vec(v):
    g, p = v.shape
    return v.reshape(g // S5_GROUPS, S5_SLABS, (S5_GROUPS // S5_SLABS) * p)


def _gelu_tanh(x):
    return 0.5 * x * (1.0 + jnp.tanh(math.sqrt(2.0 / math.pi) * (x + 0.044715 * (x * x * x))))


def _s5_kernel(u_ref, x0_ref, a_ref, bmat_ref, cre_ref, cim_ref, d_ref, gluw_ref, glub_ref, ng_ref,
               o_ref, xout_ref, utb_scr, xs_scr, x_scr, y_scr, ab_scr, *, nb, tt, interleave, n_sub):
    rows = nb * tt
    sub_tt = tt // n_sub
    sub_rows = nb * sub_tt
    ss = S5_SLAB_STATE
    i = pl.program_id(0)

    @pl.when(i == 0)
    def _():
        x_scr[...] = x0_ref[...]
        ab_scr[...] = jnp.broadcast_to(a_ref[...], ab_scr.shape)

    if interleave:
        for b in range(nb):
            for s in range(S5_SLABS):
                utb_scr[pl.ds(s * rows + b, tt, stride=nb), :] = u_ref[b, :, s * LANES:(s + 1) * LANES]
    else:
        for s in range(S5_SLABS):
            utb_scr[s * rows:(s + 1) * rows, :] = u_ref[:, s * LANES:(s + 1) * LANES]

    def u_rows(s, h):
        return utb_scr[s * rows + h * sub_rows:s * rows + (h + 1) * sub_rows, :]

    def in_proj_ops(h):
        def make(s):
            def run():
                xs_scr[h * sub_rows:(h + 1) * sub_rows, s * 2 * ss:(s + 1) * 2 * ss] = jnp.dot(
                    u_rows(s, h).astype(BF16), bmat_ref[s], preferred_element_type=F32)
            return run
        return [make(s) for s in range(S5_SLABS)]

    def scan_ops(h):
        def make(t):
            def run():
                rs = slice(t * nb, (t + 1) * nb)
                for s in range(S5_SLABS):
                    lo = s * 2 * ss
                    ar = ab_scr[:, lo:lo + ss]
                    ai = ab_scr[:, lo + ss:lo + 2 * ss]
                    xr = x_scr[:, lo:lo + ss]
                    xi = x_scr[:, lo + ss:lo + 2 * ss]
                    nr = (ar * xr - ai * xi) + xs_scr[rs, lo:lo + ss]
                    ni = (ar * xi + ai * xr) + xs_scr[rs, lo + ss:lo + 2 * ss]
                    x_scr[:, lo:lo + ss] = nr
                    x_scr[:, lo + ss:lo + 2 * ss] = ni
                    xs_scr[rs, lo:lo + ss] = nr
                    xs_scr[rs, lo + ss:lo + 2 * ss] = ni
            return run
        return [make(t) for t in range(h * sub_tt, (h + 1) * sub_tt)]

    def out_proj_ops(h):
        rs = slice(h * sub_rows, (h + 1) * sub_rows)
        ys = []

        def make(s):
            def run():
                lo = s * 2 * ss
                yr = jnp.dot(xs_scr[rs, lo:lo + ss].astype(BF16), cre_ref[s], preferred_element_type=F32)
                yi = jnp.dot(xs_scr[rs, lo + ss:lo + 2 * ss].astype(BF16), cim_ref[s], preferred_element_type=F32)
                ys.append((yr - yi) + d_ref[0:1, s * LANES:(s + 1) * LANES] * u_rows(s, h))
            return run

        def finish():
            y = _gelu_tanh(jnp.concatenate(ys, axis=1))
            y = y * _sigmoid(_mm(y, gluw_ref[...]) + glub_ref[...])
            y = y * lax.rsqrt(jnp.mean(y * y, axis=-1, keepdims=True) + NORM_EPS) * ng_ref[...]
            if interleave:
                for s in range(S5_SLABS):
                    y_scr[s * rows + h * sub_rows:s * rows + (h + 1) * sub_rows, :] = y[:, s * LANES:(s + 1) * LANES]
            else:
                o_ref[rs, :] = y

        return [make(s) for s in range(S5_SLABS)] + [finish]

    def run_interleaved(main, side):
        done = 0
        for idx, op in enumerate(main):
            op()
            want = (len(side) * (idx + 1)) // len(main)
            while done < want:
                side[done]()
                done += 1

    for op in in_proj_ops(0):
        op()
    for h in range(n_sub):
        side = in_proj_ops(h + 1) if h + 1 < n_sub else []
        if h > 0:
            side = side + out_proj_ops(h - 1)
        run_interleaved(scan_ops(h), side)
    for op in out_proj_ops(n_sub - 1):
        op()

    @pl.when(i == pl.num_programs(0) - 1)
    def _():
        xout_ref[...] = x_scr[...]

    if interleave:
        for b in range(nb):
            for s in range(S5_SLABS):
                o_ref[b, :, s * LANES:(s + 1) * LANES] = y_scr[pl.ds(s * rows + b, tt, stride=nb), :].astype(
                    o_ref.dtype)


def _s5(u, x0, l, a, bmat, cre, cim, d, gluw, glub, ng, nb, tt, interleave, n_sub=1):
    rows = nb * tt
    nstate = x0.shape[1]
    full = lambda arr: pl.BlockSpec(arr.shape, lambda i: (0,) * arr.ndim)
    lay = lambda arr: _layer_spec(arr, l)
    if interleave:
        steps = u.shape[1] // tt
        uspec = pl.BlockSpec((nb, tt, S5_WIDTH), lambda i: (0, i, 0))
        oshape = jax.ShapeDtypeStruct(u.shape, BF16)
    else:
        steps = 1
        uspec = pl.BlockSpec((nb, S5_WIDTH), lambda i: (0, 0))
        oshape = jax.ShapeDtypeStruct(u.shape, F32)
    return pl.pallas_call(
        functools.partial(_s5_kernel, nb=nb, tt=tt, interleave=interleave, n_sub=n_sub),
        grid=(steps,),
        in_specs=[uspec, full(x0), lay(a), lay(bmat), lay(cre), lay(cim), lay(d), lay(gluw), lay(glub),
                  lay(ng)],
        out_specs=[uspec, full(x0)],
        out_shape=[oshape, jax.ShapeDtypeStruct(x0.shape, F32)],
        scratch_shapes=[pltpu.VMEM((S5_SLABS * rows, LANES), F32), pltpu.VMEM((rows, nstate), F32),
                        pltpu.VMEM((nb, nstate), F32), pltpu.VMEM((S5_SLABS * rows, LANES), F32),
                        pltpu.VMEM((nb, nstate), F32)],
        compiler_params=_cparams(("arbitrary",)),
        name="s5_scan",
    )(u, x0, a, bmat, cre, cim, d, gluw, glub, ng)


def _router(logits_t, bias_col):
    scores = _sigmoid(logits_t)
    sel = scores + bias_col
    s = [sel[e:e + 1, :] for e in range(N_EXPERTS)]
    n_groups = N_EXPERTS // EXPERTS_PER_GROUP
    gs = []
    for gi in range(n_groups):
        m = s[gi * EXPERTS_PER_GROUP: (gi + 1) * EXPERTS_PER_GROUP]
        best = None
        for p in range(EXPERTS_PER_GROUP):
            for q in range(p + 1, EXPERTS_PER_GROUP):
                ps = m[p] + m[q]
                best = ps if best is None else jnp.maximum(best, ps)
        gs.append(best)
    gmax = functools.reduce(jnp.maximum, gs)
    taken = None
    in_best = []
    for gi in range(n_groups):
        hit = gs[gi] == gmax
        if taken is None:
            cur = hit
            taken = hit
        else:
            cur = jnp.logical_and(hit, jnp.logical_not(taken))
            taken = jnp.logical_or(taken, hit)
        in_best.append(cur)
    selm = []
    picked = []
    for e in range(N_EXPERTS):
        gi = e // EXPERTS_PER_GROUP
        cnt = jnp.zeros_like(s[e])
        for j in range(gi * EXPERTS_PER_GROUP, (gi + 1) * EXPERTS_PER_GROUP):
            if j == e:
                continue
            beats = (s[j] >= s[e]) if j < e else (s[j] > s[e])
            cnt = cnt + jnp.where(beats, 1.0, 0.0)
        sel_e = jnp.logical_and(in_best[gi], cnt < 1.5)
        selm.append(jnp.where(sel_e, 1.0, 0.0))
        picked.append(jnp.where(sel_e, scores[e:e + 1, :], 0.0))
    denom = functools.reduce(lambda x, y: x + y, picked)
    comb = [p / denom for p in picked]
    return comb, selm, in_best


def _router_dense(logits_t, bias_col):
    comb, _, _ = _router(logits_t, bias_col)
    return jnp.concatenate(comb, axis=0)


def _router_sparse(logits_t, bias_col):
    comb, selm, in_best = _router(logits_t, bias_col)
    cls = None
    wlo = None
    whi = None
    for gi in range(N_EXPERTS // EXPERTS_PER_GROUP):
        term = jnp.where(in_best[gi], float(len(PAIR_CODES) * gi) - 1.0, 0.0)
        cls = term if cls is None else cls + term
        seen = None
        for j in range(EXPERTS_PER_GROUP):
            e = gi * EXPERTS_PER_GROUP + j
            cls = cls + selm[e] * PAIR_VALUE[j]
            first = selm[e] if seen is None else selm[e] * (1.0 - seen)
            seen = selm[e] if seen is None else jnp.maximum(seen, selm[e])
            lo_term = first * comb[e]
            hi_term = (selm[e] - first) * comb[e]
            wlo = lo_term if wlo is None else wlo + lo_term
            whi = hi_term if whi is None else whi + hi_term
    return cls, wlo, whi


def _outproj_kernel(oa_ref, ob_ref, x_ref, gt_ref, sc_ref, sh_ref, g_ref, wout_ref, wrt_ref, br_ref,
                    xo_ref, hn_ref, comb_ref, *, tm):
    o = jnp.concatenate([oa_ref[...].astype(BF16), ob_ref[...].astype(BF16)], axis=1)
    mix = jnp.dot(o, wout_ref[...], preferred_element_type=F32)
    x = x_ref[...] + gt_ref[...] * mix
    xo_ref[...] = x
    ms = jnp.mean(x * x, axis=-1, keepdims=True)
    hn = x * lax.rsqrt(ms + NORM_EPS) * g_ref[...]
    hn = hn * (1.0 + sc_ref[...]) + sh_ref[...]
    hb = hn.astype(BF16)
    hn_ref[...] = hb
    logits_t = lax.dot_general(wrt_ref[...], hb, (((1,), (1,)), ((), ())), preferred_element_type=F32)
    comb_t = _router_dense(logits_t, br_ref[...])
    pad = jnp.zeros((LANES - N_EXPERTS, tm), F32)
    comb_ref[...] = jnp.concatenate([comb_t, pad], axis=0).T


def _outproj(oa, ob, x, mod, l, g, wout, wrt, br, tm, rows_per_seq):
    t = x.shape[0]
    full = lambda a: pl.BlockSpec(a.shape, lambda i: (0,) * a.ndim)
    lay = lambda a: _layer_spec(a, l)
    row = lambda n: pl.BlockSpec((tm, n), lambda i: (i, 0))
    ms = lambda j: _mod_spec(mod, l, j, tm, rows_per_seq, t)
    return pl.pallas_call(
        functools.partial(_outproj_kernel, tm=tm),
        grid=(t // tm,),
        in_specs=[row(GDN_WIDTH), row(S5_WIDTH), row(D_MODEL), ms(2), ms(4), ms(3), lay(g), lay(wout),
                  full(wrt), full(br)],
        out_specs=[row(D_MODEL), row(D_MODEL), row(LANES)],
        out_shape=[jax.ShapeDtypeStruct((t, D_MODEL), F32), jax.ShapeDtypeStruct((t, D_MODEL), BF16),
                   jax.ShapeDtypeStruct((t, LANES), F32)],
        compiler_params=_cparams(("arbitrary",)),
        name="outproj_router",
    )(oa, ob, x, mod, mod, mod, g, wout, wrt, br)


def _outproj_sparse_kernel(oa_ref, ob_ref, x_ref, gt_ref, sc_ref, sh_ref, g_ref, wout_ref, wrt_ref, br_ref,
                           xo_ref, hx_ref, route_ref, cnt_ref, run_scr, *, tm):
    i = pl.program_id(0)

    @pl.when(i == 0)
    def _():
        run_scr[...] = jnp.zeros_like(run_scr)

    o = jnp.concatenate([oa_ref[...].astype(BF16), ob_ref[...].astype(BF16)], axis=1)
    mix = jnp.dot(o, wout_ref[...], preferred_element_type=F32)
    x = x_ref[...] + gt_ref[...] * mix
    xo_ref[...] = x
    ms = jnp.mean(x * x, axis=-1, keepdims=True)
    hn = x * lax.rsqrt(ms + NORM_EPS) * g_ref[...]
    hn = hn * (1.0 + sc_ref[...]) + sh_ref[...]
    hb = hn.astype(BF16)
    logits_t = lax.dot_general(wrt_ref[...], hb, (((1,), (1,)), ((), ())), preferred_element_type=F32)
    cls, wlo, whi = _router_sparse(logits_t, br_ref[...])
    sub = lax.broadcasted_iota(jnp.int32, (CLASS_ROWS, tm), 0).astype(F32)
    onehot = sub == cls
    r = lax.broadcasted_iota(jnp.int32, (tm, tm), 0)
    c = lax.broadcasted_iota(jnp.int32, (tm, tm), 1)
    before = jnp.where(r < c, 1.0, 0.0).astype(BF16)
    prefix = jnp.dot(jnp.where(onehot, 1.0, 0.0).astype(BF16), before, preferred_element_type=F32)
    run = run_scr[...]
    rank = jnp.sum(jnp.where(onehot, prefix + run[:, 0:1], 0.0), axis=0, keepdims=True)
    run_scr[...] = run + jnp.sum(jnp.where(onehot, 1.0, 0.0), axis=1, keepdims=True)
    cnt_ref[...] = run_scr[...]
    route = jnp.concatenate([cls, rank, wlo, whi, jnp.zeros((SUBLANES - 4, tm), F32)], axis=0)
    route_ref[...] = route
    hx_ref[:, :D_MODEL] = hn
    hx_ref[:, D_MODEL:] = jnp.concatenate([route, jnp.zeros((LANES - SUBLANES, tm), F32)], axis=0).T


def _outproj_sparse(oa, ob, x, mod, l, g, wout, wrt, br, tm, rows_per_seq):
    t = x.shape[0]
    full = lambda a: pl.BlockSpec(a.shape, lambda i: (0,) * a.ndim)
    lay = lambda a: _layer_spec(a, l)
    row = lambda n: pl.BlockSpec((tm, n), lambda i: (i, 0))
    ms = lambda j: _mod_spec(mod, l, j, tm, rows_per_seq, t)
    return pl.pallas_call(
        functools.partial(_outproj_sparse_kernel, tm=tm),
        grid=(t // tm,),
        in_specs=[row(GDN_WIDTH), row(S5_WIDTH), row(D_MODEL), ms(2), ms(4), ms(3), lay(g), lay(wout),
                  full(wrt), full(br)],
        out_specs=[row(D_MODEL), row(HX_WIDTH), pl.BlockSpec((SUBLANES, tm), lambda i: (0, i)),
                   pl.BlockSpec((CLASS_ROWS, LANES), lambda i: (0, 0))],
        out_shape=[jax.ShapeDtypeStruct((t, D_MODEL), F32), jax.ShapeDtypeStruct((t, HX_WIDTH), F32),
                   jax.ShapeDtypeStruct((SUBLANES, t), F32), jax.ShapeDtypeStruct((CLASS_ROWS, LANES), F32)],
        scratch_shapes=[pltpu.VMEM((CLASS_ROWS, LANES), F32)],
        compiler_params=_cparams(("arbitrary",)),
        name="outproj_route",
    )(oa, ob, x, mod, mod, mod, g, wout, wrt, br)


def _dispatch_kernel(pos_ref, tail_ref, hx_ref, sorted_ref, stage, zero_scr, sems, zsem, *, tm, n_steps):
    i = pl.program_id(0)
    slot = i % 2

    def tail_copy(cidx):
        start = pl.multiple_of(jnp.maximum(tail_ref[cidx], 0), MOE_TILE)
        return pltpu.make_async_copy(zero_scr, sorted_ref.at[pl.ds(start, MOE_TILE), :], zsem)

    @pl.when(i == 0)
    def _():
        zero_scr[...] = jnp.zeros_like(zero_scr)
        for cidx in range(2 * N_CLASSES):
            @pl.when(tail_ref[cidx] >= 0)
            def _():
                tail_copy(cidx).start()
        for cidx in range(2 * N_CLASSES):
            @pl.when(tail_ref[cidx] >= 0)
            def _():
                tail_copy(cidx).wait()

    def row_copy(s, r, p):
        return pltpu.make_async_copy(stage.at[s, pl.ds(r, 1), :], sorted_ref.at[pl.ds(p, 1), :], sems.at[s])

    def wait_slot(s):
        pltpu.make_async_copy(stage.at[s], sorted_ref.at[pl.ds(0, tm), :], sems.at[s]).wait()

    @pl.when(i >= 2)
    def _():
        wait_slot(slot)

    stage[slot] = hx_ref[...]
    base = i * tm

    def issue(g, carry):
        r8 = pl.multiple_of(g * SUBLANES, SUBLANES)
        for k in range(SUBLANES):
            row_copy(slot, r8 + k, pos_ref[base + r8 + k]).start(priority=k % 2)
        return carry

    lax.fori_loop(0, tm // SUBLANES, issue, 0)

    @pl.when(i == n_steps - 1)
    def _():
        wait_slot(slot)
        if n_steps > 1:
            wait_slot(1 - slot)


def _dispatch(pos, tail, hx, n_rows, tm):
    t = hx.shape[0]
    n_steps = t // tm
    return pl.pallas_call(
        functools.partial(_dispatch_kernel, tm=tm, n_steps=n_steps),
        grid_spec=pltpu.PrefetchScalarGridSpec(
            num_scalar_prefetch=2,
            grid=(n_steps,),
            in_specs=[pl.BlockSpec((tm, HX_WIDTH), lambda i, pos, tail: (i, 0))],
            out_specs=pl.BlockSpec(memory_space=pl.ANY),
            scratch_shapes=[pltpu.VMEM((2, tm, HX_WIDTH), F32), pltpu.VMEM((MOE_TILE, HX_WIDTH), F32),
                            pltpu.SemaphoreType.DMA((2,)), pltpu.SemaphoreType.DMA(())]),
        out_shape=jax.ShapeDtypeStruct((n_rows, HX_WIDTH), F32),
        compiler_params=_cparams(("arbitrary",)),
        name="moe_dispatch",
    )(pos, tail, hx)


def _moe_sorted_kernel(lo_ref, hi_ref, src_ref, nv_ref, xs_ref, wg_lo_ref, wu_lo_ref, wd_lo_ref, wg_hi_ref,
                       wu_hi_ref, wd_hi_ref, o_ref):
    j = pl.program_id(0)

    @pl.when(j < nv_ref[0])
    def _():
        xs = xs_ref[...]
        x = xs[:, :D_MODEL].astype(BF16)
        wlo = xs[:, D_MODEL + 2:D_MODEL + 3]
        whi = xs[:, D_MODEL + 3:D_MODEL + 4]
        dot = functools.partial(jnp.dot, preferred_element_type=F32)
        a_lo = _silu(dot(x, wg_lo_ref[...])) * dot(x, wu_lo_ref[...]) * wlo
        a_hi = _silu(dot(x, wg_hi_ref[...])) * dot(x, wu_hi_ref[...]) * whi
        o_ref[...] = dot(a_lo.astype(BF16), wd_lo_ref[...]) + dot(a_hi.astype(BF16), wd_hi_ref[...])

    @pl.when(j >= nv_ref[0])
    def _():
        o_ref[...] = jnp.zeros_like(o_ref)


def _moe_sorted(tile_lo, tile_hi, tile_src, n_valid, xs, l, wg, wu, wd):
    n_rows = xs.shape[0]
    n_tiles = n_rows // MOE_TILE
    wspec = lambda w, which: pl.BlockSpec(
        (None, None) + w.shape[2:], lambda j, lo, hi, src, nv: (l, (lo, hi)[which][j], 0, 0))
    return pl.pallas_call(
        _moe_sorted_kernel,
        grid_spec=pltpu.PrefetchScalarGridSpec(
            num_scalar_prefetch=4,
            grid=(n_tiles,),
            in_specs=[pl.BlockSpec((MOE_TILE, HX_WIDTH), lambda j, lo, hi, src, nv: (src[j], 0)),
                      wspec(wg, 0), wspec(wu, 0), wspec(wd, 0), wspec(wg, 1), wspec(wu, 1), wspec(wd, 1)],
            out_specs=pl.BlockSpec((MOE_TILE, D_MODEL), lambda j, lo, hi, src, nv: (j, 0))),
        out_shape=jax.ShapeDtypeStruct((n_rows, D_MODEL), F32),
        compiler_params=_cparams(("arbitrary",)),
        name="moe_sorted",
    )(tile_lo, tile_hi, tile_src, n_valid, xs, wg, wu, wd, wg, wu, wd)


def _combine_kernel(pos_ref, ys_ref, x_ref, gt_ref, fg_ref, o_ref, buf, sems, *, tm, n_steps, final_norm):
    i = pl.program_id(0)
    slot = i % 2

    def row_copy(s, r, p):
        return pltpu.make_async_copy(ys_ref.at[pl.ds(p, 1), :], buf.at[s, pl.ds(r, 1), :], sems.at[s])

    def issue(step, s):
        base = step * tm

        def body(g, carry):
            r8 = pl.multiple_of(g * SUBLANES, SUBLANES)
            for k in range(SUBLANES):
                row_copy(s, r8 + k, pos_ref[base + r8 + k]).start(priority=k % 2)
            return carry

        lax.fori_loop(0, tm // SUBLANES, body, 0)

    @pl.when(i == 0)
    def _():
        issue(0, 0)

    @pl.when(i + 1 < n_steps)
    def _():
        issue(i + 1, 1 - slot)

    pltpu.make_async_copy(ys_ref.at[pl.ds(0, tm), :], buf.at[slot], sems.at[slot]).wait()
    x = x_ref[...] + gt_ref[...] * buf[slot]
    if final_norm:
        x = x * lax.rsqrt(jnp.mean(x * x, axis=-1, keepdims=True) + NORM_EPS) * fg_ref[...]
    o_ref[...] = x


def _combine(pos, ys, x, mod, l, fg, tm, rows_per_seq, final_norm):
    t = x.shape[0]
    n_steps = t // tm
    return pl.pallas_call(
        functools.partial(_combine_kernel, tm=tm, n_steps=n_steps, final_norm=final_norm),
        grid_spec=pltpu.PrefetchScalarGridSpec(
            num_scalar_prefetch=1,
            grid=(n_steps,),
            in_specs=[pl.BlockSpec(memory_space=pl.ANY),
                      pl.BlockSpec((tm, D_MODEL), lambda i, pos: (i, 0)),
                      _mod_spec(mod, l, 5, tm, rows_per_seq, t),
                      pl.BlockSpec(fg.shape, lambda i, pos: (0, 0))],
            out_specs=pl.BlockSpec((tm, D_MODEL), lambda i, pos: (i, 0)),
            scratch_shapes=[pltpu.VMEM((2, tm, D_MODEL), F32), pltpu.SemaphoreType.DMA((2,))]),
        out_shape=jax.ShapeDtypeStruct((t, D_MODEL), F32),
        compiler_params=_cparams(("arbitrary",)),
        name="moe_combine",
    )(pos, ys, x, mod, fg)


def _route_plan(route, counts, n_tokens):
    cnt = counts[:N_CLASSES, 0].astype(jnp.int32)
    padded = ((cnt + MOE_TILE - 1) // MOE_TILE) * MOE_TILE
    ends = jnp.cumsum(padded)
    offsets = ends - padded
    cls = route[0].astype(jnp.int32)
    pos = route[1].astype(jnp.int32) + jnp.sum(
        jnp.where(cls[:, None] == jnp.arange(N_CLASSES, dtype=jnp.int32)[None, :], offsets[None, :], 0), axis=1)
    n_tiles = n_tokens // MOE_TILE + N_CLASSES
    n_valid = ends[-1] // MOE_TILE
    src = jnp.minimum(jnp.arange(n_tiles, dtype=jnp.int32), jnp.maximum(n_valid - 1, 0))
    tile_cls = jnp.minimum(jnp.sum((src[:, None] * MOE_TILE >= ends[None, :]).astype(jnp.int32), axis=1),
                           N_CLASSES - 1)
    group = tile_cls // len(PAIR_CODES)
    pair = jnp.asarray(PAIR_CODES, dtype=jnp.int32)[tile_cls % len(PAIR_CODES)]
    tile_lo = group * EXPERTS_PER_GROUP + pair[:, 0]
    tile_hi = group * EXPERTS_PER_GROUP + pair[:, 1]
    unused = (n_valid + jnp.arange(N_CLASSES, dtype=jnp.int32)) * MOE_TILE
    tail = jnp.concatenate([jnp.where(cnt > 0, ends - MOE_TILE, -1),
                            jnp.where(unused < n_tiles * MOE_TILE, unused, -1)]).astype(jnp.int32)
    return pos, tail, tile_lo, tile_hi, src, n_valid.reshape(1).astype(jnp.int32), n_tiles * MOE_TILE


def _moe_kernel(hn_ref, comb_ref, x_ref, gt_ref, wg_ref, wu_ref, wd_ref, fg_ref, o_ref, acc_scr, *, final_norm):
    e = pl.program_id(1)

    @pl.when(e == 0)
    def _():
        acc_scr[...] = jnp.zeros_like(acc_scr)

    hg = jnp.dot(hn_ref[...], wg_ref[...], preferred_element_type=F32)
    hu = jnp.dot(hn_ref[...], wu_ref[...], preferred_element_type=F32)
    lane = lax.broadcasted_iota(jnp.int32, (1, LANES), 1)
    ce = jnp.sum(jnp.where(lane == e, comb_ref[...], 0.0), axis=-1, keepdims=True)
    act = _silu(hg) * hu * ce
    acc_scr[...] += jnp.dot(act.astype(BF16), wd_ref[...], preferred_element_type=F32)

    @pl.when(e == pl.num_programs(1) - 1)
    def _():
        x = x_ref[...] + gt_ref[...] * acc_scr[...]
        if final_norm:
            x = x * lax.rsqrt(jnp.mean(x * x, axis=-1, keepdims=True) + NORM_EPS) * fg_ref[...]
        o_ref[...] = x


def _moe(hn, comb, x, mod, l, wg, wu, wd, fg, tm, rows_per_seq, final_norm):
    t = x.shape[0]
    row = lambda n: pl.BlockSpec((tm, n), lambda i, e: (i, 0))
    wspec = lambda w: pl.BlockSpec((None, None) + w.shape[2:], lambda i, e: (l, e, 0, 0))
    return pl.pallas_call(
        functools.partial(_moe_kernel, final_norm=final_norm),
        grid=(t // tm, N_EXPERTS),
        in_specs=[row(D_MODEL), row(LANES), row(D_MODEL), _mod_spec(mod, l, 5, tm, rows_per_seq, t),
                  wspec(wg), wspec(wu), wspec(wd),
                  pl.BlockSpec(fg.shape, lambda i, e: (0, 0))],
        out_specs=row(D_MODEL),
        out_shape=jax.ShapeDtypeStruct((t, D_MODEL), F32),
        scratch_shapes=[pltpu.VMEM((tm, D_MODEL), F32)],
        compiler_params=_cparams(("arbitrary", "arbitrary")),
        name="moe",
    )(hn, comb, x, mod, wg, wu, wd, fg)


def _pad_lanes(v, offset):
    return jnp.pad(v, ((0, 0), (offset, LANES - offset - v.shape[1])))[:, None, :]


def _state_to_slab(re, im):
    lead = re.shape[:2]
    r = re.reshape(lead + (S5_SLABS, S5_SLAB_STATE))
    i = im.reshape(lead + (S5_SLABS, S5_SLAB_STATE))
    return jnp.stack([r, i], axis=3).reshape(lead + (S5_SLABS * 2 * S5_SLAB_STATE,))


def _slab_to_state(x):
    lead = x.shape[:2]
    x4 = x.reshape(lead + (S5_SLABS, 2, S5_SLAB_STATE))
    re = x4[:, :, :, 0].reshape(lead + (S5_GROUPS, S5_STATE))
    im = x4[:, :, :, 1].reshape(lead + (S5_GROUPS, S5_STATE))
    return re, im


def kernel(x_prompt, x_sample, c_prompt, c_sample, state_conv, state_gdn, state_s5_re, state_s5_im, norm1_g, norm2_g, w_ada, b_ada, w_in, conv_w, a_log, dt_bias, gdn_norm_g, s5_lambda_re, s5_lambda_im, s5_log_dt, s5_b_re, s5_b_im, s5_c_re, s5_c_im, s5_d, s5_glu_w, s5_glu_b, s5_norm_g, w_out, w_router, b_router, w_gate, w_up, w_down, final_g):
    bp, seq, _ = x_prompt.shape
    bs = x_sample.shape[0]
    tp = bp * seq
    tm_p = 512
    tm_gdn = 1024 if seq % 1024 == 0 else tm_p
    tt = 128 if seq % 128 == 0 else seq

    mod_s = _ada(jnp.concatenate([c_sample, c_prompt], axis=0), w_ada, b_ada)
    mod_p = mod_s.reshape(DEPTH, bs + bp, 1, 6 * D_MODEL)

    nba = 2 * GDN_HEADS
    u_start = QKV_WIDTH + GDN_WIDTH
    w_in_packed = jnp.concatenate(
        [w_in[:, :, :u_start], w_in[:, :, u_start + nba:], w_in[:, :, u_start:u_start + nba],
         jnp.zeros((DEPTH, D_MODEL, LANES - nba), F32)], axis=2).astype(BF16)
    g1 = norm1_g[:, None, :]
    g2 = norm2_g[:, None, :]
    alog = _pad_lanes(a_log, GDN_HEADS)
    dtb = _pad_lanes(dt_bias, GDN_HEADS)
    gn = gdn_norm_g[:, None, :]
    a_re, a_im, bb_re, bb_im = _s5_params(s5_lambda_re, s5_lambda_im, s5_log_dt, s5_b_re, s5_b_im)
    a_vec = jnp.concatenate([_slab_vec(a_re), _slab_vec(a_im)], axis=2).reshape(DEPTH, 1, -1)
    bmat = jnp.concatenate([_slab_blockdiag(bb_re), _slab_blockdiag(bb_im)], axis=3).astype(BF16)
    flat = lambda m: m.reshape((DEPTH * S5_GROUPS,) + m.shape[2:])
    cre = jnp.swapaxes(_slab_blockdiag(flat(s5_c_re)), 2, 3).astype(BF16)
    cim = jnp.swapaxes(_slab_blockdiag(flat(s5_c_im)), 2, 3).astype(BF16)
    dvec = s5_d[:, None, :]
    gluw = s5_glu_w.astype(BF16)
    glub = s5_glu_b[:, None, :]
    ng = s5_norm_g[:, None, :]
    wout = w_out.astype(BF16)
    wg = w_gate.astype(BF16)
    wu = w_up.astype(BF16)
    wd = w_down.astype(BF16)
    wrt = jnp.transpose(w_router).astype(BF16)
    br = b_router.reshape(N_EXPERTS, 1)
    fg = final_g.reshape(1, D_MODEL)
    conv_s = state_conv.reshape(DEPTH, bs, (CONV_WIDTH - 1) * QKV_WIDTH)

    xp = x_prompt.reshape(tp, D_MODEL)
    xs = x_sample.reshape(bs, D_MODEL)
    outs_p = {k: [] for k in ("conv", "gdn", "s5")}
    outs_s = {k: [] for k in ("conv", "s5")}
    gdn_s_new = None
    zero_conv = jnp.zeros((bp, SUBLANES, QKV_WIDTH), F32)
    zero_gdn = jnp.zeros((bp, GDN_HEADS, GDN_HEAD_DIM, GDN_HEAD_DIM), F32)
    zero_s5 = jnp.zeros((bp, S5_SLABS * 2 * S5_SLAB_STATE), F32)
    x0_s = _state_to_slab(state_s5_re, state_s5_im)

    for l in range(DEPTH):
        last = l == DEPTH - 1

        qkv, z, ba, u = _inproj(xp, mod_p, l, g1, w_in_packed, tm_p, seq)
        oa, sg, cv = _gdn_prompt(qkv, ba, z, l, conv_w, alog, dtb, gn, zero_conv, zero_gdn, tm_gdn)
        ob, xst = _s5(u.reshape(bp, seq, S5_WIDTH), zero_s5, l, a_vec, bmat, cre, cim, dvec, gluw, glub, ng,
                      bp, tt, True, n_sub=4)
        xp, hx, route, counts = _outproj_sparse(oa, ob.reshape(tp, S5_WIDTH), xp, mod_p, l, g2, wout, wrt, br,
                                                tm_p, seq)
        pos, tail, tile_lo, tile_hi, tile_src, n_valid, n_rows = _route_plan(route, counts, tp)
        xsorted = _dispatch(pos, tail, hx, n_rows, tm_p)
        ys = _moe_sorted(tile_lo, tile_hi, tile_src, n_valid, xsorted, l, wg, wu, wd)
        xp = _combine(pos, ys, xp, mod_p, l, fg, tm_p, seq, last)
        outs_p["conv"].append(cv)
        outs_p["gdn"].append(sg)
        outs_p["s5"].append(xst)

        qkv, z, ba, u = _inproj(xs, mod_s, l, g1, w_in_packed, bs, 1)
        oa, gdn_s_new, cv = _gdn_step(qkv, ba, z, l, conv_w, alog, dtb, gn, conv_s, state_gdn, gdn_s_new)
        ob, xst = _s5(u, x0_s[l], l, a_vec, bmat, cre, cim, dvec, gluw, glub, ng, bs, 1, False)
        xs, hn, comb = _outproj(oa, ob, xs, mod_s, l, g2, wout, wrt, br, bs, 1)
        xs = _moe(hn, comb, xs, mod_s, l, wg, wu, wd, fg, bs, 1, last)
        outs_s["conv"].append(cv)
        outs_s["s5"].append(xst)

    st = lambda d, k: jnp.stack(d[k])
    re_p, im_p = _slab_to_state(st(outs_p, "s5"))
    re_s, im_s = _slab_to_state(st(outs_s, "s5"))
    conv_p = st(outs_p, "conv")[:, :, SUBLANES - (CONV_WIDTH - 1):, :]
    conv_s_new = st(outs_s, "conv").reshape(DEPTH, bs, CONV_WIDTH - 1, QKV_WIDTH)
    return (xp.reshape(bp, seq, D_MODEL), xs.reshape(bs, 1, D_MODEL),
            conv_p, st(outs_p, "gdn"), re_p, im_p, conv_s_new, gdn_s_new, re_s, im_s)
```

```python
import functools
import math

import jax
import jax.numpy as jnp
from jax import lax
from jax.experimental import pallas as pl
from jax.experimental.pallas import tpu as pltpu

F32 = jnp.float32
BF16 = jnp.bfloat16

D_MODEL = 1024
DEPTH = 2
GDN_HEAD_DIM = 128
GDN_WIDTH = 512
GDN_HEADS = 4
CONV_WIDTH = 4
S5_CH_PER_GROUP = 16
S5_WIDTH = 512
S5_GROUPS = 32
S5_STATE = 64
QKV_WIDTH = 3 * GDN_WIDTH
N_EXPERTS = 16
EXPERTS_PER_GROUP = 4
D_EXPERT = 256
NORM_EPS = 1e-6

LANES = 128
SUBLANES = 8
GDN_CHUNK = 128
GDN_CHUNKS_PER_ITER = 2
S5_SLABS = 4
S5_SLAB_STATE = 512
MOE_TILE = 512
PAIR_VALUE = (0.0, 1.0, 2.0, 4.0)
PAIR_CODES = ((0, 1), (0, 2), (1, 2), (0, 3), (1, 3), (2, 3))
N_CLASSES = (N_EXPERTS // EXPERTS_PER_GROUP) * len(PAIR_CODES)
CLASS_ROWS = 32
HX_WIDTH = D_MODEL + LANES
VMEM_LIMIT = 56 * 1024 * 1024


def _cparams(sem):
    return pltpu.CompilerParams(dimension_semantics=sem, vmem_limit_bytes=VMEM_LIMIT)


def _sigmoid(x):
    return 1.0 / (1.0 + jnp.exp(-x))


def _silu(x):
    return x * _sigmoid(x)


def _softplus(x):
    return jnp.maximum(x, 0.0) + jnp.log1p(jnp.exp(-jnp.abs(x)))


def _mm(a, b):
    return jnp.dot(a.astype(BF16), b.astype(BF16), preferred_element_type=F32)


def _ada_kernel(c_ref, w_ref, b_ref, o_ref):
    c = c_ref[...]
    o_ref[0] = _mm(_silu(c), w_ref[0]) + b_ref[0]


def _ada(c_all, w_ada, b_ada):
    rows = c_all.shape[0]
    n_out = w_ada.shape[-1]
    tn = 1536
    return pl.pallas_call(
        _ada_kernel,
        grid=(DEPTH, n_out // tn),
        in_specs=[
            pl.BlockSpec((rows, D_MODEL), lambda l, j: (0, 0)),
            pl.BlockSpec((1, D_MODEL, tn), lambda l, j: (l, 0, j)),
            pl.BlockSpec((1, 1, tn), lambda l, j: (l, 0, j)),
        ],
        out_specs=pl.BlockSpec((1, rows, tn), lambda l, j: (l, 0, j)),
        out_shape=jax.ShapeDtypeStruct((DEPTH, rows, n_out), F32),
        compiler_params=_cparams(("arbitrary", "arbitrary")),
        name="ada_mod",
    )(c_all, w_ada, b_ada.reshape(DEPTH, 1, n_out))


def _inproj_kernel(x_ref, sc_ref, sh_ref, g_ref, wqkv_ref, wz_ref, wba_ref, wu_ref,
                   qkv_ref, z_ref, ba_ref, u_ref):
    x = x_ref[...]
    ms = jnp.mean(x * x, axis=-1, keepdims=True)
    hn = x * lax.rsqrt(ms + NORM_EPS) * g_ref[...]
    hn = hn * (1.0 + sc_ref[...]) + sh_ref[...]
    hb = hn.astype(BF16)
    qkv_ref[...] = jnp.dot(hb, wqkv_ref[...], preferred_element_type=F32)
    z_ref[...] = jnp.dot(hb, wz_ref[...], preferred_element_type=F32)
    ba_ref[...] = jnp.dot(hb, wba_ref[...], preferred_element_type=F32)
    u_ref[...] = jnp.dot(hb, wu_ref[...], preferred_element_type=F32)


def _mod_spec(mod, l, j, tm, rows_per_seq, t):
    if mod.ndim == 4:
        tiles_per_seq = rows_per_seq // tm
        row0 = mod.shape[1] - t // rows_per_seq
        return pl.BlockSpec((None, None, 1, D_MODEL), lambda i, *_: (l, row0 + i // tiles_per_seq, 0, j))
    return pl.BlockSpec((None, tm, D_MODEL), lambda i, *_: (l, i, j))


def _layer_spec(arr, l):
    return pl.BlockSpec((None,) + arr.shape[1:], lambda *_: (l,) + (0,) * (arr.ndim - 1))


def _inproj(x, mod, l, g, w_in_packed, tm, rows_per_seq):
    t = x.shape[0]
    row = lambda n: pl.BlockSpec((tm, n), lambda i: (i, 0))
    wcol = lambda width, start: pl.BlockSpec((None, D_MODEL, width), lambda i: (l, 0, start // width))
    u_start = QKV_WIDTH + GDN_WIDTH
    return pl.pallas_call(
        _inproj_kernel,
        grid=(t // tm,),
        in_specs=[row(D_MODEL), _mod_spec(mod, l, 1, tm, rows_per_seq, t), _mod_spec(mod, l, 0, tm, rows_per_seq, t),
                  _layer_spec(g, l), wcol(QKV_WIDTH, 0), wcol(GDN_WIDTH, QKV_WIDTH),
                  wcol(LANES, u_start + S5_WIDTH), wcol(S5_WIDTH, u_start)],
        out_specs=[row(QKV_WIDTH), row(GDN_WIDTH), row(LANES), row(S5_WIDTH)],
        out_shape=[jax.ShapeDtypeStruct((t, QKV_WIDTH), F32), jax.ShapeDtypeStruct((t, GDN_WIDTH), F32),
                   jax.ShapeDtypeStruct((t, LANES), F32), jax.ShapeDtypeStruct((t, S5_WIDTH), F32)],
        compiler_params=_cparams(("arbitrary",)),
        name="inproj",
    )(x, mod, mod, g, w_in_packed, w_in_packed, w_in_packed, w_in_packed)


def _gdn_gates(ba, alog, dtb):
    beta = _sigmoid(ba)
    g = -jnp.exp(alog) * _softplus(ba + dtb)
    return beta, g


def _l2n(x):
    return x * lax.rsqrt(jnp.sum(x * x, axis=-1, keepdims=True) + NORM_EPS)


def _gated_norm(o, gn, z):
    on = o * lax.rsqrt(jnp.mean(o * o, axis=-1, keepdims=True) + NORM_EPS) * gn
    return on * _silu(z)


def _gdn_prompt_kernel(qkv_ref, ba_ref, z_ref, cw_ref, alog_ref, dtb_ref, gn_ref, conv0_ref, s0_ref,
                       o_ref, sout_ref, convout_ref, xp_scr, y_scr, g_scr, b_scr, s_scr, wq_scr, ak_scr, u0_scr,
                       egl_scr, *, tm):
    c_len = GDN_CHUNK
    hd = GDN_HEAD_DIM
    pair_rows = GDN_CHUNKS_PER_ITER * c_len
    n_pairs = tm // pair_rows
    i = pl.program_id(1)
    last = pl.num_programs(1) - 1

    @pl.when(i == 0)
    def _():
        xp_scr[0:SUBLANES, :] = conv0_ref[0]
        s_scr[...] = s0_ref[0]

    xp_scr[SUBLANES:SUBLANES + tm, :] = qkv_ref[...]
    cw = cw_ref[...]
    cw_rows = [cw[j:j + 1, :].reshape(1, 1, QKV_WIDTH) for j in range(CONV_WIDTH)]
    sub = lax.broadcasted_iota(jnp.int32, (1, SUBLANES, QKV_WIDTH), 1)

    def conv_rows(r0, n_rows):
        x3 = xp_scr[r0:r0 + n_rows + SUBLANES, :].reshape(n_rows // SUBLANES + 1, SUBLANES, QKV_WIDTH)

        def delayed(s):
            rot = pltpu.roll(x3, s, axis=1)
            return jnp.where(sub >= s, rot[1:], rot[:-1])

        y = delayed(3) * cw_rows[0]
        y = y + delayed(2) * cw_rows[1]
        y = y + delayed(1) * cw_rows[2]
        y = y + x3[1:] * cw_rows[3]
        y_scr[r0:r0 + n_rows, :] = _silu(y).reshape(n_rows, QKV_WIDTH)

    beta, g = _gdn_gates(ba_ref[...], alog_ref[...], dtb_ref[...])
    b_scr[...] = beta
    g_scr[...] = g

    r = lax.broadcasted_iota(jnp.int32, (c_len, c_len), 0)
    c = lax.broadcasted_iota(jnp.int32, (c_len, c_len), 1)
    ge = r >= c
    gt = r > c
    tri = jnp.where(ge, 1.0, 0.0).astype(BF16)
    eye = jnp.where(r == c, 1.0, 0.0).astype(F32)
    blk16 = (r // 16) == (c // 16)
    pair_masks = [((r // (2 * s)) == (c // (2 * s))) & ((r // s) != (c // s)) for s in (16, 32, 64)]
    gn = gn_ref[...]
    scale = hd ** -0.5

    def prep_stages(p):
        chains = []

        def load():
            for ck in range(GDN_CHUNKS_PER_ITER):
                ci = p * GDN_CHUNKS_PER_ITER + ck
                rows = slice(ci * c_len, (ci + 1) * c_len)
                gch = g_scr[rows, :]
                bch = b_scr[rows, :]
                g1 = gch.astype(BF16)
                r1 = gch - g1.astype(F32)
                g2 = r1.astype(BF16)
                g3 = (r1 - g2.astype(F32)).astype(BF16)
                gcum = (jnp.dot(tri, g1, preferred_element_type=F32)
                        + jnp.dot(tri, g2, preferred_element_type=F32)
                        + jnp.dot(tri, g3, preferred_element_type=F32))
                gcum_t = gcum.T
                glast = gcum[c_len - 1:c_len, :]
                egl_scr[ci] = jnp.broadcast_to(jnp.exp(glast), (SUBLANES, LANES))
                for h in range(GDN_HEADS):
                    lo = h * hd
                    q = _l2n(y_scr[rows, lo:lo + hd]) * scale
                    k = _l2n(y_scr[rows, GDN_WIDTH + lo:GDN_WIDTH + lo + hd])
                    v = y_scr[rows, 2 * GDN_WIDTH + lo:2 * GDN_WIDTH + lo + hd]
                    gl = GDN_HEADS + h
                    gcb = jnp.broadcast_to(gcum[:, gl:gl + 1], (c_len, c_len))
                    bcol = jnp.broadcast_to(bch[:, h:h + 1], (c_len, c_len))
                    egc = jnp.exp(gcb)
                    kdf = jnp.exp(glast[:, gl:gl + 1] - gcb)
                    decay = jnp.where(ge, jnp.exp(gcb - gcum_t[gl:gl + 1, :]), 0.0)
                    kb = k.astype(BF16)
                    chains.append(dict(
                        ci=ci, h=h, decay=decay, bcol=bcol, kb=kb,
                        kq=jnp.concatenate([kb, q.astype(BF16)], axis=0),
                        rhs=jnp.concatenate([((bcol * egc) * k).astype(BF16), (bcol * v).astype(BF16)], axis=1),
                        qg=(q * egc).astype(BF16),
                        kdt=(k * kdf).T.astype(BF16)))

        def gram():
            for ch in chains:
                ch["kkqk"] = lax.dot_general(ch["kq"], ch["kb"], (((1,), (1,)), ((), ())),
                                             preferred_element_type=F32)

        def neumann0():
            for ch in chains:
                ch["lmat"] = jnp.where(gt, ch["bcol"] * ch["kkqk"][:c_len] * ch["decay"], 0.0)
                ch["n1"] = jnp.where(blk16, -ch["lmat"], 0.0)
                ch["t"] = eye + ch["n1"]
            for ch in chains:
                ch["n2"] = _mm(ch["n1"], ch["n1"])

        def neumann1():
            for ch in chains:
                ch["n4"] = _mm(ch["n2"], ch["n2"])
                ch["t"] = ch["t"] + _mm(ch["t"], ch["n2"])

        def neumann2():
            for ch in chains:
                ch["n8"] = _mm(ch["n4"], ch["n4"])
                ch["t"] = ch["t"] + _mm(ch["t"], ch["n4"])

        def neumann3():
            for ch in chains:
                ch["t"] = ch["t"] + _mm(ch["t"], ch["n8"])

        def merge_a(pm):
            def run():
                for ch in chains:
                    ch["x"] = _mm(ch["t"], jnp.where(pm, ch["lmat"], 0.0))
            return run

        def merge_b():
            for ch in chains:
                ch["t"] = ch["t"] - _mm(ch["x"], ch["t"])

        def finish():
            for ch in chains:
                wu = jnp.dot(ch["t"].astype(BF16), ch["rhs"], preferred_element_type=F32)
                ci, h = ch["ci"], ch["h"]
                wq_scr[ci, h] = jnp.concatenate([wu[:, :hd].astype(BF16), ch["qg"]], axis=0)
                u0_scr[ci, h] = wu[:, hd:]
                ak_scr[ci, h] = jnp.concatenate([(ch["kkqk"][c_len:] * ch["decay"]).astype(BF16), ch["kdt"]],
                                                axis=0)

        stages = [load, gram, neumann0, neumann1, neumann2, neumann3]
        for pm in pair_masks:
            stages += [merge_a(pm), merge_b]
        return stages + [finish]

    def recur_stages(ci):
        rows = slice(ci * c_len, (ci + 1) * c_len)
        heads = range(GDN_HEADS)
        st = {}

        def first():
            st["ss"] = [s_scr[h] for h in heads]
            st["wsqs"] = [jnp.dot(wq_scr[ci, h], st["ss"][h].astype(BF16), preferred_element_type=F32)
                          for h in heads]

        def second():
            us = [(u0_scr[ci, h] - st["wsqs"][h][:c_len]).astype(BF16) for h in heads]
            st["auku"] = [jnp.dot(ak_scr[ci, h], us[h], preferred_element_type=F32) for h in heads]

        def third():
            egl = egl_scr[ci]
            for h in heads:
                lo = h * hd
                gl = GDN_HEADS + h
                o = st["wsqs"][h][c_len:] + st["auku"][h][:c_len]
                s_scr[h] = egl[0:1, gl:gl + 1] * st["ss"][h] + st["auku"][h][c_len:]
                o_ref[rows, lo:lo + hd] = _gated_norm(o, gn, z_ref[rows, lo:lo + hd]).astype(o_ref.dtype)

        return [first, second, third]

    def run_interleaved(main, early, side):
        n_main = len(main)
        done = 0
        for idx, stage in enumerate(main):
            stage()
            if idx == 0:
                for extra in early:
                    extra()
            want = (len(side) * (idx + 1)) // n_main
            while done < want:
                side[done]()
                done += 1

    conv_rows(0, pair_rows)
    for p in range(n_pairs):
        early = []
        side = []
        if p + 1 < n_pairs:
            early.append(functools.partial(conv_rows, (p + 1) * pair_rows, pair_rows))
        if p > 0:
            for ck in range(GDN_CHUNKS_PER_ITER):
                side += recur_stages((p - 1) * GDN_CHUNKS_PER_ITER + ck)
        run_interleaved(prep_stages(p), early, side)
    for ck in range(GDN_CHUNKS_PER_ITER):
        for stage in recur_stages((n_pairs - 1) * GDN_CHUNKS_PER_ITER + ck):
            stage()

    tail = xp_scr[tm:tm + SUBLANES, :]
    xp_scr[0:SUBLANES, :] = tail

    @pl.when(i == last)
    def _():
        convout_ref[0] = tail
        sout_ref[0] = s_scr[...]


def _gdn_prompt(qkv, ba, z, l, cw, alog, dtb, gn, conv0, s0, tm):
    n_seq = s0.shape[0]
    t = qkv.shape[0]
    nt = t // n_seq // tm
    row = lambda n: pl.BlockSpec((tm, n), lambda b, i: (b * nt + i, 0))
    lay = lambda a: _layer_spec(a, l)
    hd = GDN_HEAD_DIM
    return pl.pallas_call(
        functools.partial(_gdn_prompt_kernel, tm=tm),
        grid=(n_seq, nt),
        in_specs=[row(QKV_WIDTH), row(LANES), row(GDN_WIDTH), lay(cw), lay(alog), lay(dtb), lay(gn),
                  pl.BlockSpec((1, SUBLANES, QKV_WIDTH), lambda b, i: (b, 0, 0)),
                  pl.BlockSpec((1, GDN_HEADS, hd, hd), lambda b, i: (b, 0, 0, 0))],
        out_specs=[row(GDN_WIDTH),
                   pl.BlockSpec((1, GDN_HEADS, hd, hd), lambda b, i: (b, 0, 0, 0)),
                   pl.BlockSpec((1, SUBLANES, QKV_WIDTH), lambda b, i: (b, 0, 0))],
        out_shape=[jax.ShapeDtypeStruct((t, GDN_WIDTH), BF16),
                   jax.ShapeDtypeStruct((n_seq, GDN_HEADS, hd, hd), F32),
                   jax.ShapeDtypeStruct((n_seq, SUBLANES, QKV_WIDTH), F32)],
        scratch_shapes=[pltpu.VMEM((tm + SUBLANES, QKV_WIDTH), F32), pltpu.VMEM((tm, QKV_WIDTH), F32),
                        pltpu.VMEM((tm, LANES), F32), pltpu.VMEM((tm, LANES), F32),
                        pltpu.VMEM((GDN_HEADS, hd, hd), F32),
                        pltpu.VMEM((tm // GDN_CHUNK, GDN_HEADS, 2 * GDN_CHUNK, hd), BF16),
                        pltpu.VMEM((tm // GDN_CHUNK, GDN_HEADS, 2 * GDN_CHUNK, hd), BF16),
                        pltpu.VMEM((tm // GDN_CHUNK, GDN_HEADS, GDN_CHUNK, hd), F32),
                        pltpu.VMEM((tm // GDN_CHUNK, SUBLANES, LANES), F32)],
        compiler_params=_cparams(("arbitrary", "arbitrary")),
        name="gdn_prompt",
    )(qkv, ba, z, cw, alog, dtb, gn, conv0, s0)


def _gdn_step_kernel(qkv_ref, ba_ref, z_ref, cw_ref, alog_ref, dtb_ref, gn_ref, conv_ref, s_ref, *refs, first):
    nb = SUBLANES
    hd = GDN_HEAD_DIM
    if first:
        o_ref, sout_all_ref, convout_ref = refs
        sout_ref = sout_all_ref.at[0]
        for d in range(1, DEPTH):
            sout_all_ref[d] = jnp.zeros(sout_all_ref.shape[1:], F32)
    else:
        _, o_ref, sout_ref, convout_ref = refs
    x = qkv_ref[...]
    cb = conv_ref[...]
    cw = cw_ref[...]
    b0 = cb[:, 0:QKV_WIDTH]
    b1 = cb[:, QKV_WIDTH:2 * QKV_WIDTH]
    b2 = cb[:, 2 * QKV_WIDTH:3 * QKV_WIDTH]
    y = b0 * cw[0:1, :]
    y = y + b1 * cw[1:2, :]
    y = y + b2 * cw[2:3, :]
    y = y + x * cw[3:4, :]
    y = _silu(y)
    convout_ref[...] = jnp.concatenate([b1, b2, x], axis=1)

    beta, g = _gdn_gates(ba_ref[...], alog_ref[...], dtb_ref[...])
    a = jnp.exp(g)
    gn = gn_ref[...]
    zpad = jnp.zeros((hd - nb, hd), F32)
    for h in range(GDN_HEADS):
        lo = h * hd
        q = _l2n(y[:, lo:lo + hd]) * (hd ** -0.5)
        k = _l2n(y[:, GDN_WIDTH + lo:GDN_WIDTH + lo + hd])
        v = y[:, 2 * GDN_WIDTH + lo:2 * GDN_WIDTH + lo + hd]
        kt = jnp.concatenate([k, zpad], axis=0).T
        qt = jnp.concatenate([q, zpad], axis=0).T
        kq = jnp.sum(k * q, axis=-1, keepdims=True)
        bh = beta[:, h:h + 1]
        ah = a[:, GDN_HEADS + h:GDN_HEADS + h + 1]
        o_rows = []
        for n in range(nb):
            s = s_ref[n, h]
            kc = kt[:, n:n + 1]
            qc = qt[:, n:n + 1]
            rk = jnp.sum(s * kc, axis=0, keepdims=True)
            rq = jnp.sum(s * qc, axis=0, keepdims=True)
            an = ah[n:n + 1, :]
            un = bh[n:n + 1, :] * (v[n:n + 1, :] - an * rk)
            sout_ref[n, h] = an * s + kc * un
            o_rows.append(an * rq + kq[n:n + 1, :] * un)
        o = jnp.concatenate(o_rows, axis=0)
        o_ref[:, lo:lo + hd] = _gated_norm(o, gn, z_ref[:, lo:lo + hd])


def _gdn_step(qkv, ba, z, l, cw, alog, dtb, gn, conv, s, s_new):
    n_seq = qkv.shape[0]
    nb = SUBLANES
    hd = GDN_HEAD_DIM
    first = s_new is None
    row = lambda n: pl.BlockSpec((nb, n), lambda i: (i, 0))
    lay = lambda a: _layer_spec(a, l)
    if first:
        sspec = pl.BlockSpec((DEPTH, nb, GDN_HEADS, hd, hd), lambda i: (0, i, 0, 0, 0))
    else:
        sspec = pl.BlockSpec((None, nb, GDN_HEADS, hd, hd), lambda i: (l, i, 0, 0, 0))
    in_specs = [row(QKV_WIDTH), row(LANES), row(GDN_WIDTH), lay(cw), lay(alog), lay(dtb), lay(gn),
                pl.BlockSpec((None, nb, 3 * QKV_WIDTH), lambda i: (l, i, 0)),
                pl.BlockSpec((None, nb, GDN_HEADS, hd, hd), lambda i: (l, i, 0, 0, 0))]
    args = [qkv, ba, z, cw, alog, dtb, gn, conv, s]
    aliases = {}
    if not first:
        in_specs.append(pl.BlockSpec(memory_space=pl.ANY))
        args.append(s_new)
        aliases = {len(args) - 1: 1}
    return pl.pallas_call(
        functools.partial(_gdn_step_kernel, first=first),
        grid=(n_seq // nb,),
        in_specs=in_specs,
        out_specs=[row(GDN_WIDTH), sspec, row(3 * QKV_WIDTH)],
        out_shape=[jax.ShapeDtypeStruct((n_seq, GDN_WIDTH), F32),
                   jax.ShapeDtypeStruct((DEPTH, n_seq, GDN_HEADS, hd, hd), F32),
                   jax.ShapeDtypeStruct((n_seq, 3 * QKV_WIDTH), F32)],
        input_output_aliases=aliases,
        compiler_params=_cparams(("arbitrary",)),
        name="gdn_step",
    )(*args)


def _s5_param_kernel(lre_ref, lim_ref, ldt_ref, bre_ref, bim_ref, are_ref, aim_ref, bbre_ref, bbim_ref):
    lre = lre_ref[...]
    lim = lim_ref[...]
    dt = jnp.exp(ldt_ref[...])
    mag = jnp.exp(lre * dt)
    are = mag * jnp.cos(lim * dt)
    aim = mag * jnp.sin(lim * dt)
    are_ref[...] = are
    aim_ref[...] = aim
    nre = are - 1.0
    den = lre * lre + lim * lim
    cre = (nre * lre + aim * lim) / den
    cim = (aim * lre - nre * lim) / den
    cre = cre[:, None, :]
    cim = cim[:, None, :]
    bre = bre_ref[...]
    bim = bim_ref[...]
    bbre_ref[...] = cre * bre - cim * bim
    bbim_ref[...] = cre * bim + cim * bre


def _s5_params(lam_re, lam_im, log_dt, b_re, b_im):
    p = lam_re.shape[-1]
    lam_re = lam_re.reshape(-1, p)
    lam_im = lam_im.reshape(-1, p)
    g = lam_re.shape[0]
    bt_re = jnp.swapaxes(b_re.reshape((g,) + b_re.shape[2:]), 1, 2)
    bt_im = jnp.swapaxes(b_im.reshape((g,) + b_im.shape[2:]), 1, 2)
    cg = bt_re.shape[1]
    return pl.pallas_call(
        _s5_param_kernel,
        out_shape=[jax.ShapeDtypeStruct((g, p), F32), jax.ShapeDtypeStruct((g, p), F32),
                   jax.ShapeDtypeStruct((g, cg, p), F32), jax.ShapeDtypeStruct((g, cg, p), F32)],
        name="s5_params",
    )(lam_re, lam_im, log_dt.reshape(g, 1), bt_re, bt_im)


def _slab_blockdiag(m):
    g, cg, p = m.shape
    gl = S5_GROUPS // S5_SLABS
    m4 = m.reshape(g // gl, gl, cg, p)
    eye = jnp.eye(gl, dtype=m.dtype)
    return jnp.einsum('igcp,gh->igchp', m4, eye).reshape(g // S5_GROUPS, S5_SLABS, gl * cg, gl * p)


def _slab_vec(v):
    g, p = v.shape
    return v.reshape(g // S5_GROUPS, S5_SLABS, (S5_GROUPS // S5_SLABS) * p)


def _gelu_tanh(x):
    return 0.5 * x * (1.0 + jnp.tanh(math.sqrt(2.0 / math.pi) * (x + 0.044715 * (x * x * x))))


def _s5_kernel(u_ref, x0_ref, a_ref, bmat_ref, cre_ref, cim_ref, d_ref, gluw_ref, glub_ref, ng_ref,
               o_ref, xout_ref, utb_scr, xs_scr, x_scr, y_scr, ab_scr, *, nb, tt, interleave, n_sub):
    rows = nb * tt
    sub_tt = tt // n_sub
    sub_rows = nb * sub_tt
    ss = S5_SLAB_STATE
    i = pl.program_id(0)

    @pl.when(i == 0)
    def _():
        x_scr[...] = x0_ref[...]
        ab_scr[...] = jnp.broadcast_to(a_ref[...], ab_scr.shape)

    if interleave:
        for b in range(nb):
            for s in range(S5_SLABS):
                utb_scr[pl.ds(s * rows + b, tt, stride=nb), :] = u_ref[b, :, s * LANES:(s + 1) * LANES]
    else:
        for s in range(S5_SLABS):
            utb_scr[s * rows:(s + 1) * rows, :] = u_ref[:, s * LANES:(s + 1) * LANES]

    def u_rows(s, h):
        return utb_scr[s * rows + h * sub_rows:s * rows + (h + 1) * sub_rows, :]

    def in_proj_ops(h):
        def make(s):
            def run():
                xs_scr[h * sub_rows:(h + 1) * sub_rows, s * 2 * ss:(s + 1) * 2 * ss] = jnp.dot(
                    u_rows(s, h).astype(BF16), bmat_ref[s], preferred_element_type=F32)
            return run
        return [make(s) for s in range(S5_SLABS)]

    def scan_ops(h):
        def make(t):
            def run():
                rs = slice(t * nb, (t + 1) * nb)
                for s in range(S5_SLABS):
                    lo = s * 2 * ss
                    ar = ab_scr[:, lo:lo + ss]
                    ai = ab_scr[:, lo + ss:lo + 2 * ss]
                    xr = x_scr[:, lo:lo + ss]
                    xi = x_scr[:, lo + ss:lo + 2 * ss]
                    nr = (ar * xr - ai * xi) + xs_scr[rs, lo:lo + ss]
                    ni = (ar * xi + ai * xr) + xs_scr[rs, lo + ss:lo + 2 * ss]
                    x_scr[:, lo:lo + ss] = nr
                    x_scr[:, lo + ss:lo + 2 * ss] = ni
                    xs_scr[rs, lo:lo + ss] = nr
                    xs_scr[rs, lo + ss:lo + 2 * ss] = ni
            return run
        return [make(t) for t in range(h * sub_tt, (h + 1) * sub_tt)]

    def out_proj_ops(h):
        rs = slice(h * sub_rows, (h + 1) * sub_rows)
        ys = []

        def make(s):
            def run():
                lo = s * 2 * ss
                yr = jnp.dot(xs_scr[rs, lo:lo + ss].astype(BF16), cre_ref[s], preferred_element_type=F32)
                yi = jnp.dot(xs_scr[rs, lo + ss:lo + 2 * ss].astype(BF16), cim_ref[s], preferred_element_type=F32)
                ys.append((yr - yi) + d_ref[0:1, s * LANES:(s + 1) * LANES] * u_rows(s, h))
            return run

        def finish():
            y = _gelu_tanh(jnp.concatenate(ys, axis=1))
            y = y * _sigmoid(_mm(y, gluw_ref[...]) + glub_ref[...])
            y = y * lax.rsqrt(jnp.mean(y * y, axis=-1, keepdims=True) + NORM_EPS) * ng_ref[...]
            if interleave:
                for s in range(S5_SLABS):
                    y_scr[s * rows + h * sub_rows:s * rows + (h + 1) * sub_rows, :] = y[:, s * LANES:(s + 1) * LANES]
            else:
                o_ref[rs, :] = y

        return [make(s) for s in range(S5_SLABS)] + [finish]

    def run_interleaved(main, side):
        done = 0
        for idx, op in enumerate(main):
            op()
            want = (len(side) * (idx + 1)) // len(main)
            while done < want:
                side[done]()
                done += 1

    for op in in_proj_ops(0):
        op()
    for h in range(n_sub):
        side = in_proj_ops(h + 1) if h + 1 < n_sub else []
        if h > 0:
            side = side + out_proj_ops(h - 1)
        run_interleaved(scan_ops(h), side)
    for op in out_proj_ops(n_sub - 1):
        op()

    @pl.when(i == pl.num_programs(0) - 1)
    def _():
        xout_ref[...] = x_scr[...]

    if interleave:
        for b in range(nb):
            for s in range(S5_SLABS):
                o_ref[b, :, s * LANES:(s + 1) * LANES] = y_scr[pl.ds(s * rows + b, tt, stride=nb), :].astype(
                    o_ref.dtype)


def _s5(u, x0, l, a, bmat, cre, cim, d, gluw, glub, ng, nb, tt, interleave, n_sub=1):
    rows = nb * tt
    nstate = x0.shape[1]
    full = lambda arr: pl.BlockSpec(arr.shape, lambda i: (0,) * arr.ndim)
    lay = lambda arr: _layer_spec(arr, l)
    if interleave:
        steps = u.shape[1] // tt
        uspec = pl.BlockSpec((nb, tt, S5_WIDTH), lambda i: (0, i, 0))
        oshape = jax.ShapeDtypeStruct(u.shape, BF16)
    else:
        steps = 1
        uspec = pl.BlockSpec((nb, S5_WIDTH), lambda i: (0, 0))
        oshape = jax.ShapeDtypeStruct(u.shape, F32)
    return pl.pallas_call(
        functools.partial(_s5_kernel, nb=nb, tt=tt, interleave=interleave, n_sub=n_sub),
        grid=(steps,),
        in_specs=[uspec, full(x0), lay(a), lay(bmat), lay(cre), lay(cim), lay(d), lay(gluw), lay(glub),
                  lay(ng)],
        out_specs=[uspec, full(x0)],
        out_shape=[oshape, jax.ShapeDtypeStruct(x0.shape, F32)],
        scratch_shapes=[pltpu.VMEM((S5_SLABS * rows, LANES), F32), pltpu.VMEM((rows, nstate), F32),
                        pltpu.VMEM((nb, nstate), F32), pltpu.VMEM((S5_SLABS * rows, LANES), F32),
                        pltpu.VMEM((nb, nstate), F32)],
        compiler_params=_cparams(("arbitrary",)),
        name="s5_scan",
    )(u, x0, a, bmat, cre, cim, d, gluw, glub, ng)


def _router(logits_t, bias_col):
    scores = _sigmoid(logits_t)
    sel = scores + bias_col
    s = [sel[e:e + 1, :] for e in range(N_EXPERTS)]
    n_groups = N_EXPERTS // EXPERTS_PER_GROUP
    gs = []
    for gi in range(n_groups):
        m = s[gi * EXPERTS_PER_GROUP: (gi + 1) * EXPERTS_PER_GROUP]
        best = None
        for p in range(EXPERTS_PER_GROUP):
            for q in range(p + 1, EXPERTS_PER_GROUP):
                ps = m[p] + m[q]
                best = ps if best is None else jnp.maximum(best, ps)
        gs.append(best)
    gmax = functools.reduce(jnp.maximum, gs)
    taken = None
    in_best = []
    for gi in range(n_groups):
        hit = gs[gi] == gmax
        if taken is None:
            cur = hit
            taken = hit
        else:
            cur = jnp.logical_and(hit, jnp.logical_not(taken))
            taken = jnp.logical_or(taken, hit)
        in_best.append(cur)
    selm = []
    picked = []
    for e in range(N_EXPERTS):
        gi = e // EXPERTS_PER_GROUP
        cnt = jnp.zeros_like(s[e])
        for j in range(gi * EXPERTS_PER_GROUP, (gi + 1) * EXPERTS_PER_GROUP):
            if j == e:
                continue
            beats = (s[j] >= s[e]) if j < e else (s[j] > s[e])
            cnt = cnt + jnp.where(beats, 1.0, 0.0)
        sel_e = jnp.logical_and(in_best[gi], cnt < 1.5)
        selm.append(jnp.where(sel_e, 1.0, 0.0))
        picked.append(jnp.where(sel_e, scores[e:e + 1, :], 0.0))
    denom = functools.reduce(lambda x, y: x + y, picked)
    comb = [p / denom for p in picked]
    return comb, selm, in_best


def _router_dense(logits_t, bias_col):
    comb, _, _ = _router(logits_t, bias_col)
    return jnp.concatenate(comb, axis=0)


def _router_sparse(logits_t, bias_col):
    comb, selm, in_best = _router(logits_t, bias_col)
    cls = None
    wlo = None
    whi = None
    for gi in range(N_EXPERTS // EXPERTS_PER_GROUP):
        term = jnp.where(in_best[gi], float(len(PAIR_CODES) * gi) - 1.0, 0.0)
        cls = term if cls is None else cls + term
        seen = None
        for j in range(EXPERTS_PER_GROUP):
            e = gi * EXPERTS_PER_GROUP + j
            cls = cls + selm[e] * PAIR_VALUE[j]
            first = selm[e] if seen is None else selm[e] * (1.0 - seen)
            seen = selm[e] if seen is None else jnp.maximum(seen, selm[e])
            lo_term = first * comb[e]
            hi_term = (selm[e] - first) * comb[e]
            wlo = lo_term if wlo is None else wlo + lo_term
            whi = hi_term if whi is None else whi + hi_term
    return cls, wlo, whi


def _outproj_kernel(oa_ref, ob_ref, x_ref, gt_ref, sc_ref, sh_ref, g_ref, wout_ref, wrt_ref, br_ref,
                    xo_ref, hn_ref, comb_ref, *, tm):
    o = jnp.concatenate([oa_ref[...].astype(BF16), ob_ref[...].astype(BF16)], axis=1)
    mix = jnp.dot(o, wout_ref[...], preferred_element_type=F32)
    x = x_ref[...] + gt_ref[...] * mix
    xo_ref[...] = x
    ms = jnp.mean(x * x, axis=-1, keepdims=True)
    hn = x * lax.rsqrt(ms + NORM_EPS) * g_ref[...]
    hn = hn * (1.0 + sc_ref[...]) + sh_ref[...]
    hb = hn.astype(BF16)
    hn_ref[...] = hb
    logits_t = lax.dot_general(wrt_ref[...], hb, (((1,), (1,)), ((), ())), preferred_element_type=F32)
    comb_t = _router_dense(logits_t, br_ref[...])
    pad = jnp.zeros((LANES - N_EXPERTS, tm), F32)
    comb_ref[...] = jnp.concatenate([comb_t, pad], axis=0).T


def _outproj(oa, ob, x, mod, l, g, wout, wrt, br, tm, rows_per_seq):
    t = x.shape[0]
    full = lambda a: pl.BlockSpec(a.shape, lambda i: (0,) * a.ndim)
    lay = lambda a: _layer_spec(a, l)
    row = lambda n: pl.BlockSpec((tm, n), lambda i: (i, 0))
    ms = lambda j: _mod_spec(mod, l, j, tm, rows_per_seq, t)
    return pl.pallas_call(
        functools.partial(_outproj_kernel, tm=tm),
        grid=(t // tm,),
        in_specs=[row(GDN_WIDTH), row(S5_WIDTH), row(D_MODEL), ms(2), ms(4), ms(3), lay(g), lay(wout),
                  full(wrt), full(br)],
        out_specs=[row(D_MODEL), row(D_MODEL), row(LANES)],
        out_shape=[jax.ShapeDtypeStruct((t, D_MODEL), F32), jax.ShapeDtypeStruct((t, D_MODEL), BF16),
                   jax.ShapeDtypeStruct((t, LANES), F32)],
        compiler_params=_cparams(("arbitrary",)),
        name="outproj_router",
    )(oa, ob, x, mod, mod, mod, g, wout, wrt, br)


def _outproj_sparse_kernel(oa_ref, ob_ref, x_ref, gt_ref, sc_ref, sh_ref, g_ref, wout_ref, wrt_ref, br_ref,
                           xo_ref, hx_ref, route_ref, cnt_ref, run_scr, *, tm):
    i = pl.program_id(0)

    @pl.when(i == 0)
    def _():
        run_scr[...] = jnp.zeros_like(run_scr)

    o = jnp.concatenate([oa_ref[...].astype(BF16), ob_ref[...].astype(BF16)], axis=1)
    mix = jnp.dot(o, wout_ref[...], preferred_element_type=F32)
    x = x_ref[...] + gt_ref[...] * mix
    xo_ref[...] = x
    ms = jnp.mean(x * x, axis=-1, keepdims=True)
    hn = x * lax.rsqrt(ms + NORM_EPS) * g_ref[...]
    hn = hn * (1.0 + sc_ref[...]) + sh_ref[...]
    hb = hn.astype(BF16)
    logits_t = lax.dot_general(wrt_ref[...], hb, (((1,), (1,)), ((), ())), preferred_element_type=F32)
    cls, wlo, whi = _router_sparse(logits_t, br_ref[...])
    sub = lax.broadcasted_iota(jnp.int32, (CLASS_ROWS, tm), 0).astype(F32)
    onehot = sub == cls
    r = lax.broadcasted_iota(jnp.int32, (tm, tm), 0)
    c = lax.broadcasted_iota(jnp.int32, (tm, tm), 1)
    before = jnp.where(r < c, 1.0, 0.0).astype(BF16)
    prefix = jnp.dot(jnp.where(onehot, 1.0, 0.0).astype(BF16), before, preferred_element_type=F32)
    run = run_scr[...]
    rank = jnp.sum(jnp.where(onehot, prefix + run[:, 0:1], 0.0), axis=0, keepdims=True)
    run_scr[...] = run + jnp.sum(jnp.where(onehot, 1.0, 0.0), axis=1, keepdims=True)
    cnt_ref[...] = run_scr[...]
    route = jnp.concatenate([cls, rank, wlo, whi, jnp.zeros((SUBLANES - 4, tm), F32)], axis=0)
    route_ref[...] = route
    hx_ref[:, :D_MODEL] = hn
    hx_ref[:, D_MODEL:] = jnp.concatenate([route, jnp.zeros((LANES - SUBLANES, tm), F32)], axis=0).T


def _outproj_sparse(oa, ob, x, mod, l, g, wout, wrt, br, tm, rows_per_seq):
    t = x.shape[0]
    full = lambda a: pl.BlockSpec(a.shape, lambda i: (0,) * a.ndim)
    lay = lambda a: _layer_spec(a, l)
    row = lambda n: pl.BlockSpec((tm, n), lambda i: (i, 0))
    ms = lambda j: _mod_spec(mod, l, j, tm, rows_per_seq, t)
    return pl.pallas_call(
        functools.partial(_outproj_sparse_kernel, tm=tm),
        grid=(t // tm,),
        in_specs=[row(GDN_WIDTH), row(S5_WIDTH), row(D_MODEL), ms(2), ms(4), ms(3), lay(g), lay(wout),
                  full(wrt), full(br)],
        out_specs=[row(D_MODEL), row(HX_WIDTH), pl.BlockSpec((SUBLANES, tm), lambda i: (0, i)),
                   pl.BlockSpec((CLASS_ROWS, LANES), lambda i: (0, 0))],
        out_shape=[jax.ShapeDtypeStruct((t, D_MODEL), F32), jax.ShapeDtypeStruct((t, HX_WIDTH), F32),
                   jax.ShapeDtypeStruct((SUBLANES, t), F32), jax.ShapeDtypeStruct((CLASS_ROWS, LANES), F32)],
        scratch_shapes=[pltpu.VMEM((CLASS_ROWS, LANES), F32)],
        compiler_params=_cparams(("arbitrary",)),
        name="outproj_route",
    )(oa, ob, x, mod, mod, mod, g, wout, wrt, br)


def _dispatch_kernel(pos_ref, tail_ref, hx_ref, sorted_ref, stage, zero_scr, sems, zsem, *, tm, n_steps):
    i = pl.program_id(0)
    slot = i % 2

    def tail_copy(cidx):
        start = pl.multiple_of(jnp.maximum(tail_ref[cidx], 0), MOE_TILE)
        return pltpu.make_async_copy(zero_scr, sorted_ref.at[pl.ds(start, MOE_TILE), :], zsem)

    @pl.when(i == 0)
    def _():
        zero_scr[...] = jnp.zeros_like(zero_scr)
        for cidx in range(2 * N_CLASSES):
            @pl.when(tail_ref[cidx] >= 0)
            def _():
                tail_copy(cidx).start()
        for cidx in range(2 * N_CLASSES):
            @pl.when(tail_ref[cidx] >= 0)
            def _():
                tail_copy(cidx).wait()

    def row_copy(s, r, p):
        return pltpu.make_async_copy(stage.at[s, pl.ds(r, 1), :], sorted_ref.at[pl.ds(p, 1), :], sems.at[s])

    def wait_slot(s):
        pltpu.make_async_copy(stage.at[s], sorted_ref.at[pl.ds(0, tm), :], sems.at[s]).wait()

    @pl.when(i >= 2)
    def _():
        wait_slot(slot)

    stage[slot] = hx_ref[...]
    base = i * tm

    def issue(g, carry):
        r8 = pl.multiple_of(g * SUBLANES, SUBLANES)
        for k in range(SUBLANES):
            row_copy(slot, r8 + k, pos_ref[base + r8 + k]).start(priority=k % 2)
        return carry

    lax.fori_loop(0, tm // SUBLANES, issue, 0)

    @pl.when(i == n_steps - 1)
    def _():
        wait_slot(slot)
        if n_steps > 1:
            wait_slot(1 - slot)


def _dispatch(pos, tail, hx, n_rows, tm):
    t = hx.shape[0]
    n_steps = t // tm
    return pl.pallas_call(
        functools.partial(_dispatch_kernel, tm=tm, n_steps=n_steps),
        grid_spec=pltpu.PrefetchScalarGridSpec(
            num_scalar_prefetch=2,
            grid=(n_steps,),
            in_specs=[pl.BlockSpec((tm, HX_WIDTH), lambda i, pos, tail: (i, 0))],
            out_specs=pl.BlockSpec(memory_space=pl.ANY),
            scratch_shapes=[pltpu.VMEM((2, tm, HX_WIDTH), F32), pltpu.VMEM((MOE_TILE, HX_WIDTH), F32),
                            pltpu.SemaphoreType.DMA((2,)), pltpu.SemaphoreType.DMA(())]),
        out_shape=jax.ShapeDtypeStruct((n_rows, HX_WIDTH), F32),
        compiler_params=_cparams(("arbitrary",)),
        name="moe_dispatch",
    )(pos, tail, hx)


def _moe_sorted_kernel(lo_ref, hi_ref, src_ref, nv_ref, xs_ref, wg_lo_ref, wu_lo_ref, wd_lo_ref, wg_hi_ref,
                       wu_hi_ref, wd_hi_ref, o_ref):
    j = pl.program_id(0)

    @pl.when(j < nv_ref[0])
    def _():
        xs = xs_ref[...]
        x = xs[:, :D_MODEL].astype(BF16)
        wlo = xs[:, D_MODEL + 2:D_MODEL + 3]
        whi = xs[:, D_MODEL + 3:D_MODEL + 4]
        dot = functools.partial(jnp.dot, preferred_element_type=F32)
        a_lo = _silu(dot(x, wg_lo_ref[...])) * dot(x, wu_lo_ref[...]) * wlo
        a_hi = _silu(dot(x, wg_hi_ref[...])) * dot(x, wu_hi_ref[...]) * whi
        o_ref[...] = dot(a_lo.astype(BF16), wd_lo_ref[...]) + dot(a_hi.astype(BF16), wd_hi_ref[...])

    @pl.when(j >= nv_ref[0])
    def _():
        o_ref[...] = jnp.zeros_like(o_ref)


def _moe_sorted(tile_lo, tile_hi, tile_src, n_valid, xs, wg, wu, wd):
    n_rows = xs.shape[0]
    n_tiles = n_rows // MOE_TILE
    wspec = lambda w, which: pl.BlockSpec(
        (None,) + w.shape[1:], lambda j, lo, hi, src, nv: ((lo, hi)[which][j], 0, 0))
    return pl.pallas_call(
        _moe_sorted_kernel,
        grid_spec=pltpu.PrefetchScalarGridSpec(
            num_scalar_prefetch=4,
            grid=(n_tiles,),
            in_specs=[pl.BlockSpec((MOE_TILE, HX_WIDTH), lambda j, lo, hi, src, nv: (src[j], 0)),
                      wspec(wg, 0), wspec(wu, 0), wspec(wd, 0), wspec(wg, 1), wspec(wu, 1), wspec(wd, 1)],
            out_specs=pl.BlockSpec((MOE_TILE, D_MODEL), lambda j, lo, hi, src, nv: (j, 0))),
        out_shape=jax.ShapeDtypeStruct((n_rows, D_MODEL), F32),
        compiler_params=_cparams(("arbitrary",)),
        name="moe_sorted",
    )(tile_lo, tile_hi, tile_src, n_valid, xs, wg, wu, wd, wg, wu, wd)


def _combine_kernel(pos_ref, ys_ref, x_ref, gt_ref, fg_ref, o_ref, buf, sems, *, tm, n_steps, final_norm):
    i = pl.program_id(0)
    slot = i % 2

    def row_copy(s, r, p):
        return pltpu.make_async_copy(ys_ref.at[pl.ds(p, 1), :], buf.at[s, pl.ds(r, 1), :], sems.at[s])

    def issue(step, s):
        base = step * tm

        def body(g, carry):
            r8 = pl.multiple_of(g * SUBLANES, SUBLANES)
            for k in range(SUBLANES):
                row_copy(s, r8 + k, pos_ref[base + r8 + k]).start(priority=k % 2)
            return carry

        lax.fori_loop(0, tm // SUBLANES, body, 0)

    @pl.when(i == 0)
    def _():
        issue(0, 0)

    @pl.when(i + 1 < n_steps)
    def _():
        issue(i + 1, 1 - slot)

    pltpu.make_async_copy(ys_ref.at[pl.ds(0, tm), :], buf.at[slot], sems.at[slot]).wait()
    x = x_ref[...] + gt_ref[...] * buf[slot]
    if final_norm:
        x = x * lax.rsqrt(jnp.mean(x * x, axis=-1, keepdims=True) + NORM_EPS) * fg_ref[...]
    o_ref[...] = x


def _combine(pos, ys, x, mod, l, fg, tm, rows_per_seq, final_norm):
    t = x.shape[0]
    n_steps = t // tm
    return pl.pallas_call(
        functools.partial(_combine_kernel, tm=tm, n_steps=n_steps, final_norm=final_norm),
        grid_spec=pltpu.PrefetchScalarGridSpec(
            num_scalar_prefetch=1,
            grid=(n_steps,),
            in_specs=[pl.BlockSpec(memory_space=pl.ANY),
                      pl.BlockSpec((tm, D_MODEL), lambda i, pos: (i, 0)),
                      _mod_spec(mod, l, 5, tm, rows_per_seq, t),
                      pl.BlockSpec(fg.shape, lambda i, pos: (0, 0))],
            out_specs=pl.BlockSpec((tm, D_MODEL), lambda i, pos: (i, 0)),
            scratch_shapes=[pltpu.VMEM((2, tm, D_MODEL), F32), pltpu.SemaphoreType.DMA((2,))]),
        out_shape=jax.ShapeDtypeStruct((t, D_MODEL), F32),
        compiler_params=_cparams(("arbitrary",)),
        name="moe_combine",
    )(pos, ys, x, mod, fg)


def _route_plan(route, counts, n_tokens):
    cnt = counts[:N_CLASSES, 0].astype(jnp.int32)
    padded = ((cnt + MOE_TILE - 1) // MOE_TILE) * MOE_TILE
    ends = jnp.cumsum(padded)
    offsets = ends - padded
    cls = route[0].astype(jnp.int32)
    pos = route[1].astype(jnp.int32) + jnp.sum(
        jnp.where(cls[:, None] == jnp.arange(N_CLASSES, dtype=jnp.int32)[None, :], offsets[None, :], 0), axis=1)
    n_tiles = n_tokens // MOE_TILE + N_CLASSES
    n_valid = ends[-1] // MOE_TILE
    src = jnp.minimum(jnp.arange(n_tiles, dtype=jnp.int32), jnp.maximum(n_valid - 1, 0))
    tile_cls = jnp.minimum(jnp.sum((src[:, None] * MOE_TILE >= ends[None, :]).astype(jnp.int32), axis=1),
                           N_CLASSES - 1)
    group = tile_cls // len(PAIR_CODES)
    pair = jnp.asarray(PAIR_CODES, dtype=jnp.int32)[tile_cls % len(PAIR_CODES)]
    tile_lo = group * EXPERTS_PER_GROUP + pair[:, 0]
    tile_hi = group * EXPERTS_PER_GROUP + pair[:, 1]
    unused = (n_valid + jnp.arange(N_CLASSES, dtype=jnp.int32)) * MOE_TILE
    tail = jnp.concatenate([jnp.where(cnt > 0, ends - MOE_TILE, -1),
                            jnp.where(unused < n_tiles * MOE_TILE, unused, -1)]).astype(jnp.int32)
    return pos, tail, tile_lo, tile_hi, src, n_valid.reshape(1).astype(jnp.int32), n_tiles * MOE_TILE


def _moe_kernel(hn_ref, comb_ref, x_ref, gt_ref, wg_ref, wu_ref, wd_ref, fg_ref, o_ref, wgb_ref, wub_ref, wdb_ref,
                acc_scr, *, final_norm):
    e = pl.program_id(1)
    wg = wg_ref[...].astype(BF16)
    wu = wu_ref[...].astype(BF16)
    wd = wd_ref[...].astype(BF16)
    wgb_ref[...] = wg
    wub_ref[...] = wu
    wdb_ref[...] = wd

    @pl.when(e == 0)
    def _():
        acc_scr[...] = jnp.zeros_like(acc_scr)

    hg = jnp.dot(hn_ref[...], wg, preferred_element_type=F32)
    hu = jnp.dot(hn_ref[...], wu, preferred_element_type=F32)
    lane = lax.broadcasted_iota(jnp.int32, (1, LANES), 1)
    ce = jnp.sum(jnp.where(lane == e, comb_ref[...], 0.0), axis=-1, keepdims=True)
    act = _silu(hg) * hu * ce
    acc_scr[...] += jnp.dot(act.astype(BF16), wd, preferred_element_type=F32)

    @pl.when(e == pl.num_programs(1) - 1)
    def _():
        x = x_ref[...] + gt_ref[...] * acc_scr[...]
        if final_norm:
            x = x * lax.rsqrt(jnp.mean(x * x, axis=-1, keepdims=True) + NORM_EPS) * fg_ref[...]
        o_ref[...] = x


def _moe(hn, comb, x, mod, l, wg, wu, wd, fg, tm, rows_per_seq, final_norm):
    t = x.shape[0]
    row = lambda n: pl.BlockSpec((tm, n), lambda i, e: (i, 0))
    wspec = lambda w: pl.BlockSpec((None, None) + w.shape[2:], lambda i, e: (l, e, 0, 0))
    bspec = lambda w: pl.BlockSpec((None,) + w.shape[2:], lambda i, e: (e, 0, 0))
    return pl.pallas_call(
        functools.partial(_moe_kernel, final_norm=final_norm),
        grid=(t // tm, N_EXPERTS),
        in_specs=[row(D_MODEL), row(LANES), row(D_MODEL), _mod_spec(mod, l, 5, tm, rows_per_seq, t),
                  wspec(wg), wspec(wu), wspec(wd),
                  pl.BlockSpec(fg.shape, lambda i, e: (0, 0))],
        out_specs=[row(D_MODEL), bspec(wg), bspec(wu), bspec(wd)],
        out_shape=[jax.ShapeDtypeStruct((t, D_MODEL), F32)] + [jax.ShapeDtypeStruct(w.shape[1:], BF16)
                                                                for w in (wg, wu, wd)],
        scratch_shapes=[pltpu.VMEM((tm, D_MODEL), F32)],
        compiler_params=_cparams(("arbitrary", "arbitrary")),
        name="moe",
    )(hn, comb, x, mod, wg, wu, wd, fg)


def _pad_lanes(v, offset):
    return jnp.pad(v, ((0, 0), (offset, LANES - offset - v.shape[1])))[:, None, :]


def _state_to_slab(re, im):
    lead = re.shape[:2]
    r = re.reshape(lead + (S5_SLABS, S5_SLAB_STATE))
    i = im.reshape(lead + (S5_SLABS, S5_SLAB_STATE))
    return jnp.stack([r, i], axis=3).reshape(lead + (S5_SLABS * 2 * S5_SLAB_STATE,))


def _slab_to_state(x):
    lead = x.shape[:2]
    x4 = x.reshape(lead + (S5_SLABS, 2, S5_SLAB_STATE))
    re = x4[:, :, :, 0].reshape(lead + (S5_GROUPS, S5_STATE))
    im = x4[:, :, :, 1].reshape(lead + (S5_GROUPS, S5_STATE))
    return re, im


def kernel(x_prompt, x_sample, c_prompt, c_sample, state_conv, state_gdn, state_s5_re, state_s5_im, norm1_g, norm2_g, w_ada, b_ada, w_in, conv_w, a_log, dt_bias, gdn_norm_g, s5_lambda_re, s5_lambda_im, s5_log_dt, s5_b_re, s5_b_im, s5_c_re, s5_c_im, s5_d, s5_glu_w, s5_glu_b, s5_norm_g, w_out, w_router, b_router, w_gate, w_up, w_down, final_g):
    bp, seq, _ = x_prompt.shape
    bs = x_sample.shape[0]
    tp = bp * seq
    tm_p = 512
    tm_gdn = 1024 if seq % 1024 == 0 else tm_p
    tt = 128 if seq % 128 == 0 else seq

    mod_s = _ada(jnp.concatenate([c_sample, c_prompt], axis=0), w_ada, b_ada)
    mod_p = mod_s.reshape(DEPTH, bs + bp, 1, 6 * D_MODEL)

    nba = 2 * GDN_HEADS
    u_start = QKV_WIDTH + GDN_WIDTH
    w_in_packed = jnp.concatenate(
        [w_in[:, :, :u_start], w_in[:, :, u_start + nba:], w_in[:, :, u_start:u_start + nba],
         jnp.zeros((DEPTH, D_MODEL, LANES - nba), F32)], axis=2).astype(BF16)
    g1 = norm1_g[:, None, :]
    g2 = norm2_g[:, None, :]
    alog = _pad_lanes(a_log, GDN_HEADS)
    dtb = _pad_lanes(dt_bias, GDN_HEADS)
    gn = gdn_norm_g[:, None, :]
    a_re, a_im, bb_re, bb_im = _s5_params(s5_lambda_re, s5_lambda_im, s5_log_dt, s5_b_re, s5_b_im)
    a_vec = jnp.concatenate([_slab_vec(a_re), _slab_vec(a_im)], axis=2).reshape(DEPTH, 1, -1)
    bmat = jnp.concatenate([_slab_blockdiag(bb_re), _slab_blockdiag(bb_im)], axis=3).astype(BF16)
    flat = lambda m: m.reshape((DEPTH * S5_GROUPS,) + m.shape[2:])
    cre = jnp.swapaxes(_slab_blockdiag(flat(s5_c_re)), 2, 3).astype(BF16)
    cim = jnp.swapaxes(_slab_blockdiag(flat(s5_c_im)), 2, 3).astype(BF16)
    dvec = s5_d[:, None, :]
    gluw = s5_glu_w.astype(BF16)
    glub = s5_glu_b[:, None, :]
    ng = s5_norm_g[:, None, :]
    wout = w_out.astype(BF16)
    wrt = jnp.transpose(w_router).astype(BF16)
    br = b_router.reshape(N_EXPERTS, 1)
    fg = final_g.reshape(1, D_MODEL)
    conv_s = state_conv.reshape(DEPTH, bs, (CONV_WIDTH - 1) * QKV_WIDTH)

    xp = x_prompt.reshape(tp, D_MODEL)
    xs = x_sample.reshape(bs, D_MODEL)
    outs_p = {k: [] for k in ("conv", "gdn", "s5")}
    outs_s = {k: [] for k in ("conv", "s5")}
    gdn_s_new = None
    zero_conv = jnp.zeros((bp, SUBLANES, QKV_WIDTH), F32)
    zero_gdn = jnp.zeros((bp, GDN_HEADS, GDN_HEAD_DIM, GDN_HEAD_DIM), F32)
    zero_s5 = jnp.zeros((bp, S5_SLABS * 2 * S5_SLAB_STATE), F32)
    x0_s = _state_to_slab(state_s5_re, state_s5_im)

    for l in range(DEPTH):
        last = l == DEPTH - 1

        qkv, z, ba, u = _inproj(xs, mod_s, l, g1, w_in_packed, bs, 1)
        oa, gdn_s_new, cv = _gdn_step(qkv, ba, z, l, conv_w, alog, dtb, gn, conv_s, state_gdn, gdn_s_new)
        ob, xst = _s5(u, x0_s[l], l, a_vec, bmat, cre, cim, dvec, gluw, glub, ng, bs, 1, False)
        xs, hn, comb = _outproj(oa, ob, xs, mod_s, l, g2, wout, wrt, br, bs, 1)
        xs, wg_l, wu_l, wd_l = _moe(hn, comb, xs, mod_s, l, w_gate, w_up, w_down, fg, bs, 1, last)
        outs_s["conv"].append(cv)
        outs_s["s5"].append(xst)

        qkv, z, ba, u = _inproj(xp, mod_p, l, g1, w_in_packed, tm_p, seq)
        oa, sg, cv = _gdn_prompt(qkv, ba, z, l, conv_w, alog, dtb, gn, zero_conv, zero_gdn, tm_gdn)
        ob, xst = _s5(u.reshape(bp, seq, S5_WIDTH), zero_s5, l, a_vec, bmat, cre, cim, dvec, gluw, glub, ng,
                      bp, tt, True, n_sub=4)
        xp, hx, route, counts = _outproj_sparse(oa, ob.reshape(tp, S5_WIDTH), xp, mod_p, l, g2, wout, wrt, br,
                                                tm_p, seq)
        pos, tail, tile_lo, tile_hi, tile_src, n_valid, n_rows = _route_plan(route, counts, tp)
        xsorted = _dispatch(pos, tail, hx, n_rows, tm_p)
        ys = _moe_sorted(tile_lo, tile_hi, tile_src, n_valid, xsorted, wg_l, wu_l, wd_l)
        xp = _combine(pos, ys, xp, mod_p, l, fg, tm_p, seq, last)
        outs_p["conv"].append(cv)
        outs_p["gdn"].append(sg)
        outs_p["s5"].append(xst)

    st = lambda d, k: jnp.stack(d[k])
    re_p, im_p = _slab_to_state(st(outs_p, "s5"))
    re_s, im_s = _slab_to_state(st(outs_s, "s5"))
    conv_p = st(outs_p, "conv")[:, :, SUBLANES - (CONV_WIDTH - 1):, :]
    conv_s_new = st(outs_s, "conv").reshape(DEPTH, bs, CONV_WIDTH - 1, QKV_WIDTH)
    return (xp.reshape(bp, seq, D_MODEL), xs.reshape(bs, 1, D_MODEL),
            conv_p, st(outs_p, "gdn"), re_p, im_p, conv_s_new, gdn_s_new, re_s, im_s)
```

```python
import functools
import math

import jax
import jax.numpy as jnp
from jax import lax
from jax.experimental import pallas as pl
from jax.experimental.pallas import tpu as pltpu

F32 = jnp.float32
BF16 = jnp.bfloat16

D_MODEL = 1024
DEPTH = 2
GDN_HEAD_DIM = 128
GDN_WIDTH = 512
GDN_HEADS = 4
CONV_WIDTH = 4
S5_CH_PER_GROUP = 16
S5_WIDTH = 512
S5_GROUPS = 32
S5_STATE = 64
QKV_WIDTH = 3 * GDN_WIDTH
N_EXPERTS = 16
EXPERTS_PER_GROUP = 4
D_EXPERT = 256
NORM_EPS = 1e-6

LANES = 128
SUBLANES = 8
GDN_CHUNK = 128
GDN_CHUNKS_PER_ITER = 2
S5_SLABS = 4
S5_SLAB_STATE = 512
MOE_TILE = 512
PAIR_VALUE = (0.0, 1.0, 2.0, 4.0)
PAIR_CODES = ((0, 1), (0, 2), (1, 2), (0, 3), (1, 3), (2, 3))
N_CLASSES = (N_EXPERTS // EXPERTS_PER_GROUP) * len(PAIR_CODES)
CLASS_ROWS = 32
HX_WIDTH = D_MODEL + LANES
VMEM_LIMIT = 56 * 1024 * 1024


def _cparams(sem):
    return pltpu.CompilerParams(dimension_semantics=sem, vmem_limit_bytes=VMEM_LIMIT)


def _sigmoid(x):
    return 1.0 / (1.0 + jnp.exp(-x))


def _silu(x):
    return x * _sigmoid(x)


def _softplus(x):
    return jnp.maximum(x, 0.0) + jnp.log1p(jnp.exp(-jnp.abs(x)))


def _mm(a, b):
    return jnp.dot(a.astype(BF16), b.astype(BF16), preferred_element_type=F32)


def _ada_kernel(c_ref, w_ref, b_ref, o_ref):
    c = c_ref[...]
    o_ref[0] = _mm(_silu(c), w_ref[0]) + b_ref[0]


def _ada(c_all, w_ada, b_ada):
    rows = c_all.shape[0]
    n_out = w_ada.shape[-1]
    tn = 1536
    return pl.pallas_call(
        _ada_kernel,
        grid=(DEPTH, n_out // tn),
        in_specs=[
            pl.BlockSpec((rows, D_MODEL), lambda l, j: (0, 0)),
            pl.BlockSpec((1, D_MODEL, tn), lambda l, j: (l, 0, j)),
            pl.BlockSpec((1, 1, tn), lambda l, j: (l, 0, j)),
        ],
        out_specs=pl.BlockSpec((1, rows, tn), lambda l, j: (l, 0, j)),
        out_shape=jax.ShapeDtypeStruct((DEPTH, rows, n_out), F32),
        compiler_params=_cparams(("arbitrary", "arbitrary")),
        name="ada_mod",
    )(c_all, w_ada, b_ada.reshape(DEPTH, 1, n_out))


def _inproj_kernel(x_ref, sc_ref, sh_ref, g_ref, wqkv_ref, wz_ref, wba_ref, wu_ref,
                   qkv_ref, z_ref, ba_ref, u_ref):
    x = x_ref[...]
    ms = jnp.mean(x * x, axis=-1, keepdims=True)
    hn = x * lax.rsqrt(ms + NORM_EPS) * g_ref[...]
    hn = hn * (1.0 + sc_ref[...]) + sh_ref[...]
    hb = hn.astype(BF16)
    qkv_ref[...] = jnp.dot(hb, wqkv_ref[...], preferred_element_type=F32)
    z_ref[...] = jnp.dot(hb, wz_ref[...], preferred_element_type=F32)
    ba_ref[...] = jnp.dot(hb, wba_ref[...], preferred_element_type=F32)
    u_ref[...] = jnp.dot(hb, wu_ref[...], preferred_element_type=F32)


def _mod_spec(mod, l, j, tm, rows_per_seq, t):
    if mod.ndim == 4:
        tiles_per_seq = rows_per_seq // tm
        row0 = mod.shape[1] - t // rows_per_seq
        return pl.BlockSpec((None, None, 1, D_MODEL), lambda i, *_: (l, row0 + i // tiles_per_seq, 0, j))
    return pl.BlockSpec((None, tm, D_MODEL), lambda i, *_: (l, i, j))


def _layer_spec(arr, l):
    return pl.BlockSpec((None,) + arr.shape[1:], lambda *_: (l,) + (0,) * (arr.ndim - 1))


def _inproj(x, mod, l, g, w_in_packed, tm, rows_per_seq):
    t = x.shape[0]
    row = lambda n: pl.BlockSpec((tm, n), lambda i: (i, 0))
    wcol = lambda width, start: pl.BlockSpec((None, D_MODEL, width), lambda i: (l, 0, start // width))
    u_start = QKV_WIDTH + GDN_WIDTH
    return pl.pallas_call(
        _inproj_kernel,
        grid=(t // tm,),
        in_specs=[row(D_MODEL), _mod_spec(mod, l, 1, tm, rows_per_seq, t), _mod_spec(mod, l, 0, tm, rows_per_seq, t),
                  _layer_spec(g, l), wcol(QKV_WIDTH, 0), wcol(GDN_WIDTH, QKV_WIDTH),
                  wcol(LANES, u_start + S5_WIDTH), wcol(S5_WIDTH, u_start)],
        out_specs=[row(QKV_WIDTH), row(GDN_WIDTH), row(LANES), row(S5_WIDTH)],
        out_shape=[jax.ShapeDtypeStruct((t, QKV_WIDTH), F32), jax.ShapeDtypeStruct((t, GDN_WIDTH), F32),
                   jax.ShapeDtypeStruct((t, LANES), F32), jax.ShapeDtypeStruct((t, S5_WIDTH), F32)],
        compiler_params=_cparams(("arbitrary",)),
        name="inproj",
    )(x, mod, mod, g, w_in_packed, w_in_packed, w_in_packed, w_in_packed)


def _gdn_gates(ba, alog, dtb):
    beta = _sigmoid(ba)
    g = -jnp.exp(alog) * _softplus(ba + dtb)
    return beta, g


def _l2n(x):
    return x * lax.rsqrt(jnp.sum(x * x, axis=-1, keepdims=True) + NORM_EPS)


def _gated_norm(o, gn, z):
    on = o * lax.rsqrt(jnp.mean(o * o, axis=-1, keepdims=True) + NORM_EPS) * gn
    return on * _silu(z)


def _gdn_prompt_kernel(qkv_ref, ba_ref, z_ref, cw_ref, alog_ref, dtb_ref, gn_ref, conv0_ref, s0_ref,
                       o_ref, sout_ref, convout_ref, xp_scr, y_scr, g_scr, b_scr, s_scr, wq_scr, ak_scr, u0_scr,
                       egl_scr, *, tm):
    c_len = GDN_CHUNK
    hd = GDN_HEAD_DIM
    pair_rows = GDN_CHUNKS_PER_ITER * c_len
    n_pairs = tm // pair_rows
    i = pl.program_id(1)
    last = pl.num_programs(1) - 1

    @pl.when(i == 0)
    def _():
        xp_scr[0:SUBLANES, :] = conv0_ref[0]
        s_scr[...] = s0_ref[0]

    xp_scr[SUBLANES:SUBLANES + tm, :] = qkv_ref[...]
    cw = cw_ref[...]
    cw_rows = [cw[j:j + 1, :].reshape(1, 1, QKV_WIDTH) for j in range(CONV_WIDTH)]
    sub = lax.broadcasted_iota(jnp.int32, (1, SUBLANES, QKV_WIDTH), 1)

    def conv_rows(r0, n_rows):
        x3 = xp_scr[r0:r0 + n_rows + SUBLANES, :].reshape(n_rows // SUBLANES + 1, SUBLANES, QKV_WIDTH)

        def delayed(s):
            rot = pltpu.roll(x3, s, axis=1)
            return jnp.where(sub >= s, rot[1:], rot[:-1])

        y = delayed(3) * cw_rows[0]
        y = y + delayed(2) * cw_rows[1]
        y = y + delayed(1) * cw_rows[2]
        y = y + x3[1:] * cw_rows[3]
        y_scr[r0:r0 + n_rows, :] = _silu(y).reshape(n_rows, QKV_WIDTH)

    beta, g = _gdn_gates(ba_ref[...], alog_ref[...], dtb_ref[...])
    b_scr[...] = beta
    g_scr[...] = g

    r = lax.broadcasted_iota(jnp.int32, (c_len, c_len), 0)
    c = lax.broadcasted_iota(jnp.int32, (c_len, c_len), 1)
    ge = r >= c
    gt = r > c
    tri = jnp.where(ge, 1.0, 0.0).astype(BF16)
    eye = jnp.where(r == c, 1.0, 0.0).astype(F32)
    blk16 = (r // 16) == (c // 16)
    pair_masks = [((r // (2 * s)) == (c // (2 * s))) & ((r // s) != (c // s)) for s in (16, 32, 64)]
    gn = gn_ref[...]
    scale = hd ** -0.5

    def prep_stages(p):
        chains = []

        def load():
            for ck in range(GDN_CHUNKS_PER_ITER):
                ci = p * GDN_CHUNKS_PER_ITER + ck
                rows = slice(ci * c_len, (ci + 1) * c_len)
                gch = g_scr[rows, :]
                bch = b_scr[rows, :]
                g1 = gch.astype(BF16)
                r1 = gch - g1.astype(F32)
                g2 = r1.astype(BF16)
                g3 = (r1 - g2.astype(F32)).astype(BF16)
                gcum = (jnp.dot(tri, g1, preferred_element_type=F32)
                        + jnp.dot(tri, g2, preferred_element_type=F32)
                        + jnp.dot(tri, g3, preferred_element_type=F32))
                gcum_t = gcum.T
                glast = gcum[c_len - 1:c_len, :]
                egl_scr[ci] = jnp.broadcast_to(jnp.exp(glast), (SUBLANES, LANES))
                for h in range(GDN_HEADS):
                    lo = h * hd
                    q = _l2n(y_scr[rows, lo:lo + hd]) * scale
                    k = _l2n(y_scr[rows, GDN_WIDTH + lo:GDN_WIDTH + lo + hd])
                    v = y_scr[rows, 2 * GDN_WIDTH + lo:2 * GDN_WIDTH + lo + hd]
                    gl = GDN_HEADS + h
                    gcb = jnp.broadcast_to(gcum[:, gl:gl + 1], (c_len, c_len))
                    bcol = jnp.broadcast_to(bch[:, h:h + 1], (c_len, c_len))
                    egc = jnp.exp(gcb)
                    kdf = jnp.exp(glast[:, gl:gl + 1] - gcb)
                    decay = jnp.where(ge, jnp.exp(gcb - gcum_t[gl:gl + 1, :]), 0.0)
                    kb = k.astype(BF16)
                    chains.append(dict(
                        ci=ci, h=h, decay=decay, bcol=bcol, kb=kb,
                        kq=jnp.concatenate([kb, q.astype(BF16)], axis=0),
                        rhs=jnp.concatenate([((bcol * egc) * k).astype(BF16), (bcol * v).astype(BF16)], axis=1),
                        qg=(q * egc).astype(BF16),
                        kdt=(k * kdf).T.astype(BF16)))

        def gram():
            for ch in chains:
                ch["kkqk"] = lax.dot_general(ch["kq"], ch["kb"], (((1,), (1,)), ((), ())),
                                             preferred_element_type=F32)

        def neumann0():
            for ch in chains:
                ch["lmat"] = jnp.where(gt, ch["bcol"] * ch["kkqk"][:c_len] * ch["decay"], 0.0)
                ch["n1"] = jnp.where(blk16, -ch["lmat"], 0.0)
                ch["t"] = eye + ch["n1"]
            for ch in chains:
                ch["n2"] = _mm(ch["n1"], ch["n1"])

        def neumann1():
            for ch in chains:
                ch["n4"] = _mm(ch["n2"], ch["n2"])
                ch["t"] = ch["t"] + _mm(ch["t"], ch["n2"])

        def neumann2():
            for ch in chains:
                ch["n8"] = _mm(ch["n4"], ch["n4"])
                ch["t"] = ch["t"] + _mm(ch["t"], ch["n4"])

        def neumann3():
            for ch in chains:
                ch["t"] = ch["t"] + _mm(ch["t"], ch["n8"])

        def merge_a(pm):
            def run():
                for ch in chains:
                    ch["x"] = _mm(ch["t"], jnp.where(pm, ch["lmat"], 0.0))
            return run

        def merge_b():
            for ch in chains:
                ch["t"] = ch["t"] - _mm(ch["x"], ch["t"])

        def finish():
            for ch in chains:
                wu = jnp.dot(ch["t"].astype(BF16), ch["rhs"], preferred_element_type=F32)
                ci, h = ch["ci"], ch["h"]
                wq_scr[ci, h] = jnp.concatenate([wu[:, :hd].astype(BF16), ch["qg"]], axis=0)
                u0_scr[ci, h] = wu[:, hd:]
                ak_scr[ci, h] = jnp.concatenate([(ch["kkqk"][c_len:] * ch["decay"]).astype(BF16), ch["kdt"]],
                                                axis=0)

        stages = [load, gram, neumann0, neumann1, neumann2, neumann3]
        for pm in pair_masks:
            stages += [merge_a(pm), merge_b]
        return stages + [finish]

    def recur_stages(ci):
        rows = slice(ci * c_len, (ci + 1) * c_len)
        heads = range(GDN_HEADS)
        st = {}

        def first():
            st["ss"] = [s_scr[h] for h in heads]
            st["wsqs"] = [jnp.dot(wq_scr[ci, h], st["ss"][h].astype(BF16), preferred_element_type=F32)
                          for h in heads]

        def second():
            us = [(u0_scr[ci, h] - st["wsqs"][h][:c_len]).astype(BF16) for h in heads]
            st["auku"] = [jnp.dot(ak_scr[ci, h], us[h], preferred_element_type=F32) for h in heads]

        def third():
            egl = egl_scr[ci]
            for h in heads:
                lo = h * hd
                gl = GDN_HEADS + h
                o = st["wsqs"][h][c_len:] + st["auku"][h][:c_len]
                s_scr[h] = egl[0:1, gl:gl + 1] * st["ss"][h] + st["auku"][h][c_len:]
                o_ref[rows, lo:lo + hd] = _gated_norm(o, gn, z_ref[rows, lo:lo + hd]).astype(o_ref.dtype)

        return [first, second, third]

    def run_interleaved(main, early, side):
        n_main = len(main)
        done = 0
        for idx, stage in enumerate(main):
            stage()
            if idx == 0:
                for extra in early:
                    extra()
            want = (len(side) * (idx + 1)) // n_main
            while done < want:
                side[done]()
                done += 1

    conv_rows(0, pair_rows)
    for p in range(n_pairs):
        early = []
        side = []
        if p + 1 < n_pairs:
            early.append(functools.partial(conv_rows, (p + 1) * pair_rows, pair_rows))
        if p > 0:
            for ck in range(GDN_CHUNKS_PER_ITER):
                side += recur_stages((p - 1) * GDN_CHUNKS_PER_ITER + ck)
        run_interleaved(prep_stages(p), early, side)
    for ck in range(GDN_CHUNKS_PER_ITER):
        for stage in recur_stages((n_pairs - 1) * GDN_CHUNKS_PER_ITER + ck):
            stage()

    tail = xp_scr[tm:tm + SUBLANES, :]
    xp_scr[0:SUBLANES, :] = tail

    @pl.when(i == last)
    def _():
        convout_ref[0] = tail
        sout_ref[0] = s_scr[...]


def _gdn_prompt(qkv, ba, z, l, cw, alog, dtb, gn, conv0, s0, tm):
    n_seq = s0.shape[0]
    t = qkv.shape[0]
    nt = t // n_seq // tm
    row = lambda n: pl.BlockSpec((tm, n), lambda b, i: (b * nt + i, 0))
    lay = lambda a: _layer_spec(a, l)
    hd = GDN_HEAD_DIM
    return pl.pallas_call(
        functools.partial(_gdn_prompt_kernel, tm=tm),
        grid=(n_seq, nt),
        in_specs=[row(QKV_WIDTH), row(LANES), row(GDN_WIDTH), lay(cw), lay(alog), lay(dtb), lay(gn),
                  pl.BlockSpec((1, SUBLANES, QKV_WIDTH), lambda b, i: (b, 0, 0)),
                  pl.BlockSpec((1, GDN_HEADS, hd, hd), lambda b, i: (b, 0, 0, 0))],
        out_specs=[row(GDN_WIDTH),
                   pl.BlockSpec((1, GDN_HEADS, hd, hd), lambda b, i: (b, 0, 0, 0)),
                   pl.BlockSpec((1, SUBLANES, QKV_WIDTH), lambda b, i: (b, 0, 0))],
        out_shape=[jax.ShapeDtypeStruct((t, GDN_WIDTH), BF16),
                   jax.ShapeDtypeStruct((n_seq, GDN_HEADS, hd, hd), F32),
                   jax.ShapeDtypeStruct((n_seq, SUBLANES, QKV_WIDTH), F32)],
        scratch_shapes=[pltpu.VMEM((tm + SUBLANES, QKV_WIDTH), F32), pltpu.VMEM((tm, QKV_WIDTH), F32),
                        pltpu.VMEM((tm, LANES), F32), pltpu.VMEM((tm, LANES), F32),
                        pltpu.VMEM((GDN_HEADS, hd, hd), F32),
                        pltpu.VMEM((tm // GDN_CHUNK, GDN_HEADS, 2 * GDN_CHUNK, hd), BF16),
                        pltpu.VMEM((tm // GDN_CHUNK, GDN_HEADS, 2 * GDN_CHUNK, hd), BF16),
                        pltpu.VMEM((tm // GDN_CHUNK, GDN_HEADS, GDN_CHUNK, hd), F32),
                        pltpu.VMEM((tm // GDN_CHUNK, SUBLANES, LANES), F32)],
        compiler_params=_cparams(("arbitrary", "arbitrary")),
        name="gdn_prompt",
    )(qkv, ba, z, cw, alog, dtb, gn, conv0, s0)


def _gdn_step_kernel(qkv_ref, ba_ref, z_ref, cw_ref, alog_ref, dtb_ref, gn_ref, conv_ref, s_ref, *refs, first):
    nb = SUBLANES
    hd = GDN_HEAD_DIM
    if first:
        o_ref, sout_all_ref, convout_ref = refs
        sout_ref = sout_all_ref.at[0]
        for d in range(1, DEPTH):
            sout_all_ref[d] = jnp.zeros(sout_all_ref.shape[1:], F32)
    else:
        _, o_ref, sout_ref, convout_ref = refs
    x = qkv_ref[...]
    cb = conv_ref[...]
    cw = cw_ref[...]
    b0 = cb[:, 0:QKV_WIDTH]
    b1 = cb[:, QKV_WIDTH:2 * QKV_WIDTH]
    b2 = cb[:, 2 * QKV_WIDTH:3 * QKV_WIDTH]
    y = b0 * cw[0:1, :]
    y = y + b1 * cw[1:2, :]
    y = y + b2 * cw[2:3, :]
    y = y + x * cw[3:4, :]
    y = _silu(y)
    convout_ref[...] = jnp.concatenate([b1, b2, x], axis=1)

    beta, g = _gdn_gates(ba_ref[...], alog_ref[...], dtb_ref[...])
    a = jnp.exp(g)
    gn = gn_ref[...]
    zpad = jnp.zeros((hd - nb, hd), F32)
    for h in range(GDN_HEADS):
        lo = h * hd
        q = _l2n(y[:, lo:lo + hd]) * (hd ** -0.5)
        k = _l2n(y[:, GDN_WIDTH + lo:GDN_WIDTH + lo + hd])
        v = y[:, 2 * GDN_WIDTH + lo:2 * GDN_WIDTH + lo + hd]
        kt = jnp.concatenate([k, zpad], axis=0).T
        qt = jnp.concatenate([q, zpad], axis=0).T
        kq = jnp.sum(k * q, axis=-1, keepdims=True)
        bh = beta[:, h:h + 1]
        ah = a[:, GDN_HEADS + h:GDN_HEADS + h + 1]
        o_rows = []
        for n in range(nb):
            s = s_ref[n, h]
            kc = kt[:, n:n + 1]
            qc = qt[:, n:n + 1]
            rk = jnp.sum(s * kc, axis=0, keepdims=True)
            rq = jnp.sum(s * qc, axis=0, keepdims=True)
            an = ah[n:n + 1, :]
            un = bh[n:n + 1, :] * (v[n:n + 1, :] - an * rk)
            sout_ref[n, h] = an * s + kc * un
            o_rows.append(an * rq + kq[n:n + 1, :] * un)
        o = jnp.concatenate(o_rows, axis=0)
        o_ref[:, lo:lo + hd] = _gated_norm(o, gn, z_ref[:, lo:lo + hd])


def _gdn_step(qkv, ba, z, l, cw, alog, dtb, gn, conv, s, s_new):
    n_seq = qkv.shape[0]
    nb = SUBLANES
    hd = GDN_HEAD_DIM
    first = s_new is None
    row = lambda n: pl.BlockSpec((nb, n), lambda i: (i, 0))
    lay = lambda a: _layer_spec(a, l)
    if first:
        sspec = pl.BlockSpec((DEPTH, nb, GDN_HEADS, hd, hd), lambda i: (0, i, 0, 0, 0))
    else:
        sspec = pl.BlockSpec((None, nb, GDN_HEADS, hd, hd), lambda i: (l, i, 0, 0, 0))
    in_specs = [row(QKV_WIDTH), row(LANES), row(GDN_WIDTH), lay(cw), lay(alog), lay(dtb), lay(gn),
                pl.BlockSpec((None, nb, 3 * QKV_WIDTH), lambda i: (l, i, 0)),
                pl.BlockSpec((None, nb, GDN_HEADS, hd, hd), lambda i: (l, i, 0, 0, 0))]
    args = [qkv, ba, z, cw, alog, dtb, gn, conv, s]
    aliases = {}
    if not first:
        in_specs.append(pl.BlockSpec(memory_space=pl.ANY))
        args.append(s_new)
        aliases = {len(args) - 1: 1}
    return pl.pallas_call(
        functools.partial(_gdn_step_kernel, first=first),
        grid=(n_seq // nb,),
        in_specs=in_specs,
        out_specs=[row(GDN_WIDTH), sspec, row(3 * QKV_WIDTH)],
        out_shape=[jax.ShapeDtypeStruct((n_seq, GDN_WIDTH), F32),
                   jax.ShapeDtypeStruct((DEPTH, n_seq, GDN_HEADS, hd, hd), F32),
                   jax.ShapeDtypeStruct((n_seq, 3 * QKV_WIDTH), F32)],
        input_output_aliases=aliases,
        compiler_params=_cparams(("arbitrary",)),
        name="gdn_step",
    )(*args)


def _s5_param_kernel(lre_ref, lim_ref, ldt_ref, bre_ref, bim_ref, are_ref, aim_ref, bbre_ref, bbim_ref):
    lre = lre_ref[...]
    lim = lim_ref[...]
    dt = jnp.exp(ldt_ref[...])
    mag = jnp.exp(lre * dt)
    are = mag * jnp.cos(lim * dt)
    aim = mag * jnp.sin(lim * dt)
    are_ref[...] = are
    aim_ref[...] = aim
    nre = are - 1.0
    den = lre * lre + lim * lim
    cre = (nre * lre + aim * lim) / den
    cim = (aim * lre - nre * lim) / den
    cre = cre[:, None, :]
    cim = cim[:, None, :]
    bre = bre_ref[...]
    bim = bim_ref[...]
    bbre_ref[...] = cre * bre - cim * bim
    bbim_ref[...] = cre * bim + cim * bre


def _s5_params(lam_re, lam_im, log_dt, b_re, b_im):
    p = lam_re.shape[-1]
    lam_re = lam_re.reshape(-1, p)
    lam_im = lam_im.reshape(-1, p)
    g = lam_re.shape[0]
    bt_re = jnp.swapaxes(b_re.reshape((g,) + b_re.shape[2:]), 1, 2)
    bt_im = jnp.swapaxes(b_im.reshape((g,) + b_im.shape[2:]), 1, 2)
    cg = bt_re.shape[1]
    return pl.pallas_call(
        _s5_param_kernel,
        out_shape=[jax.ShapeDtypeStruct((g, p), F32), jax.ShapeDtypeStruct((g, p), F32),
                   jax.ShapeDtypeStruct((g, cg, p), F32), jax.ShapeDtypeStruct((g, cg, p), F32)],
        name="s5_params",
    )(lam_re, lam_im, log_dt.reshape(g, 1), bt_re, bt_im)


def _slab_blockdiag(m):
    g, cg, p = m.shape
    gl = S5_GROUPS // S5_SLABS
    m4 = m.reshape(g // gl, gl, cg, p)
    eye = jnp.eye(gl, dtype=m.dtype)
    return jnp.einsum('igcp,gh->igchp', m4, eye).reshape(g // S5_GROUPS, S5_SLABS, gl * cg, gl * p)


def _slab_vec(v):
    g, p = v.shape
    return v.reshape(g // S5_GROUPS, S5_SLABS, (S5_GROUPS // S5_SLABS) * p)


def _gelu_tanh(x):
    return 0.5 * x * (1.0 + jnp.tanh(math.sqrt(2.0 / math.pi) * (x + 0.044715 * (x * x * x))))


def _s5_kernel(u_ref, x0_ref, a_ref, bmat_ref, cre_ref, cim_ref, d_ref, gluw_ref, glub_ref, ng_ref,
               o_ref, xout_ref, utb_scr, xs_scr, x_scr, y_scr, ab_scr, *, nb, tt, interleave, n_sub):
    rows = nb * tt
    sub_tt = tt // n_sub
    sub_rows = nb * sub_tt
    ss = S5_SLAB_STATE
    i = pl.program_id(0)

    @pl.when(i == 0)
    def _():
        x_scr[...] = x0_ref[...]
        ab_scr[...] = jnp.broadcast_to(a_ref[...], ab_scr.shape)

    if interleave:
        for b in range(nb):
            for s in range(S5_SLABS):
                utb_scr[pl.ds(s * rows + b, tt, stride=nb), :] = u_ref[b, :, s * LANES:(s + 1) * LANES]
    else:
        for s in range(S5_SLABS):
            utb_scr[s * rows:(s + 1) * rows, :] = u_ref[:, s * LANES:(s + 1) * LANES]

    def u_rows(s, h):
        return utb_scr[s * rows + h * sub_rows:s * rows + (h + 1) * sub_rows, :]

    def in_proj_ops(h):
        def make(s):
            def run():
                xs_scr[h * sub_rows:(h + 1) * sub_rows, s * 2 * ss:(s + 1) * 2 * ss] = jnp.dot(
                    u_rows(s, h).astype(BF16), bmat_ref[s], preferred_element_type=F32)
            return run
        return [make(s) for s in range(S5_SLABS)]

    def scan_ops(h):
        def make(t):
            def run():
                rs = slice(t * nb, (t + 1) * nb)
                for s in range(S5_SLABS):
                    lo = s * 2 * ss
                    ar = ab_scr[:, lo:lo + ss]
                    ai = ab_scr[:, lo + ss:lo + 2 * ss]
                    xr = x_scr[:, lo:lo + ss]
                    xi = x_scr[:, lo + ss:lo + 2 * ss]
                    nr = (ar * xr - ai * xi) + xs_scr[rs, lo:lo + ss]
                    ni = (ar * xi + ai * xr) + xs_scr[rs, lo + ss:lo + 2 * ss]
                    x_scr[:, lo:lo + ss] = nr
                    x_scr[:, lo + ss:lo + 2 * ss] = ni
                    xs_scr[rs, lo:lo + ss] = nr
                    xs_scr[rs, lo + ss:lo + 2 * ss] = ni
            return run
        return [make(t) for t in range(h * sub_tt, (h + 1) * sub_tt)]

    def out_proj_ops(h):
        rs = slice(h * sub_rows, (h + 1) * sub_rows)
        ys = []

        def make(s):
            def run():
                lo = s * 2 * ss
                yr = jnp.dot(xs_scr[rs, lo:lo + ss].astype(BF16), cre_ref[s], preferred_element_type=F32)
                yi = jnp.dot(xs_scr[rs, lo + ss:lo + 2 * ss].astype(BF16), cim_ref[s], preferred_element_type=F32)
                ys.append((yr - yi) + d_ref[0:1, s * LANES:(s + 1) * LANES] * u_rows(s, h))
            return run

        def finish():
            y = _gelu_tanh(jnp.concatenate(ys, axis=1))
            y = y * _sigmoid(_mm(y, gluw_ref[...]) + glub_ref[...])
            y = y * lax.rsqrt(jnp.mean(y * y, axis=-1, keepdims=True) + NORM_EPS) * ng_ref[...]
            if interleave:
                for s in range(S5_SLABS):
                    y_scr[s * rows + h * sub_rows:s * rows + (h + 1) * sub_rows, :] = y[:, s * LANES:(s + 1) * LANES]
            else:
                o_ref[rs, :] = y

        return [make(s) for s in range(S5_SLABS)] + [finish]

    def run_interleaved(main, side):
        done = 0
        for idx, op in enumerate(main):
            op()
            want = (len(side) * (idx + 1)) // len(main)
            while done < want:
                side[done]()
                done += 1

    for op in in_proj_ops(0):
        op()
    for h in range(n_sub):
        side = in_proj_ops(h + 1) if h + 1 < n_sub else []
        if h > 0:
            side = side + out_proj_ops(h - 1)
        run_interleaved(scan_ops(h), side)
    for op in out_proj_ops(n_sub - 1):
        op()

    @pl.when(i == pl.num_programs(0) - 1)
    def _():
        xout_ref[...] = x_scr[...]

    if interleave:
        for b in range(nb):
            for s in range(S5_SLABS):
                o_ref[b, :, s * LANES:(s + 1) * LANES] = y_scr[pl.ds(s * rows + b, tt, stride=nb), :].astype(
                    o_ref.dtype)


def _s5(u, x0, l, a, bmat, cre, cim, d, gluw, glub, ng, nb, tt, interleave, n_sub=1):
    rows = nb * tt
    nstate = x0.shape[1]
    full = lambda arr: pl.BlockSpec(arr.shape, lambda i: (0,) * arr.ndim)
    lay = lambda arr: _layer_spec(arr, l)
    if interleave:
        steps = u.shape[1] // tt
        uspec = pl.BlockSpec((nb, tt, S5_WIDTH), lambda i: (0, i, 0))
        oshape = jax.ShapeDtypeStruct(u.shape, BF16)
    else:
        steps = 1
        uspec = pl.BlockSpec((nb, S5_WIDTH), lambda i: (0, 0))
        oshape = jax.ShapeDtypeStruct(u.shape, F32)
    return pl.pallas_call(
        functools.partial(_s5_kernel, nb=nb, tt=tt, interleave=interleave, n_sub=n_sub),
        grid=(steps,),
        in_specs=[uspec, full(x0), lay(a), lay(bmat), lay(cre), lay(cim), lay(d), lay(gluw), lay(glub),
                  lay(ng)],
        out_specs=[uspec, full(x0)],
        out_shape=[oshape, jax.ShapeDtypeStruct(x0.shape, F32)],
        scratch_shapes=[pltpu.VMEM((S5_SLABS * rows, LANES), F32), pltpu.VMEM((rows, nstate), F32),
                        pltpu.VMEM((nb, nstate), F32), pltpu.VMEM((S5_SLABS * rows, LANES), F32),
                        pltpu.VMEM((nb, nstate), F32)],
        compiler_params=_cparams(("arbitrary",)),
        name="s5_scan",
    )(u, x0, a, bmat, cre, cim, d, gluw, glub, ng)


def _router(logits_t, bias_col):
    scores = _sigmoid(logits_t)
    sel = scores + bias_col
    s = [sel[e:e + 1, :] for e in range(N_EXPERTS)]
    n_groups = N_EXPERTS // EXPERTS_PER_GROUP
    gs = []
    for gi in range(n_groups):
        m = s[gi * EXPERTS_PER_GROUP: (gi + 1) * EXPERTS_PER_GROUP]
        best = None
        for p in range(EXPERTS_PER_GROUP):
            for q in range(p + 1, EXPERTS_PER_GROUP):
                ps = m[p] + m[q]
                best = ps if best is None else jnp.maximum(best, ps)
        gs.append(best)
    gmax = functools.reduce(jnp.maximum, gs)
    taken = None
    in_best = []
    for gi in range(n_groups):
        hit = gs[gi] == gmax
        if taken is None:
            cur = hit
            taken = hit
        else:
            cur = jnp.logical_and(hit, jnp.logical_not(taken))
            taken = jnp.logical_or(taken, hit)
        in_best.append(cur)
    selm = []
    picked = []
    for e in range(N_EXPERTS):
        gi = e // EXPERTS_PER_GROUP
        cnt = jnp.zeros_like(s[e])
        for j in range(gi * EXPERTS_PER_GROUP, (gi + 1) * EXPERTS_PER_GROUP):
            if j == e:
                continue
            beats = (s[j] >= s[e]) if j < e else (s[j] > s[e])
            cnt = cnt + jnp.where(beats, 1.0, 0.0)
        sel_e = jnp.logical_and(in_best[gi], cnt < 1.5)
        selm.append(jnp.where(sel_e, 1.0, 0.0))
        picked.append(jnp.where(sel_e, scores[e:e + 1, :], 0.0))
    denom = functools.reduce(lambda x, y: x + y, picked)
    comb = [p / denom for p in picked]
    return comb, selm, in_best


def _router_dense(logits_t, bias_col):
    comb, _, _ = _router(logits_t, bias_col)
    return jnp.concatenate(comb, axis=0)


def _router_sparse(logits_t, bias_col):
    comb, selm, in_best = _router(logits_t, bias_col)
    cls = None
    wlo = None
    whi = None
    for gi in range(N_EXPERTS // EXPERTS_PER_GROUP):
        term = jnp.where(in_best[gi], float(len(PAIR_CODES) * gi) - 1.0, 0.0)
        cls = term if cls is None else cls + term
        seen = None
        for j in range(EXPERTS_PER_GROUP):
            e = gi * EXPERTS_PER_GROUP + j
            cls = cls + selm[e] * PAIR_VALUE[j]
            first = selm[e] if seen is None else selm[e] * (1.0 - seen)
            seen = selm[e] if seen is None else jnp.maximum(seen, selm[e])
            lo_term = first * comb[e]
            hi_term = (selm[e] - first) * comb[e]
            wlo = lo_term if wlo is None else wlo + lo_term
            whi = hi_term if whi is None else whi + hi_term
    return cls, wlo, whi


def _outproj_kernel(oa_ref, ob_ref, x_ref, gt_ref, sc_ref, sh_ref, g_ref, wout_ref, wrt_ref, br_ref,
                    xo_ref, hn_ref, comb_ref, *, tm):
    o = jnp.concatenate([oa_ref[...].astype(BF16), ob_ref[...].astype(BF16)], axis=1)
    mix = jnp.dot(o, wout_ref[...], preferred_element_type=F32)
    x = x_ref[...] + gt_ref[...] * mix
    xo_ref[...] = x
    ms = jnp.mean(x * x, axis=-1, keepdims=True)
    hn = x * lax.rsqrt(ms + NORM_EPS) * g_ref[...]
    hn = hn * (1.0 + sc_ref[...]) + sh_ref[...]
    hb = hn.astype(BF16)
    hn_ref[...] = hb
    logits_t = lax.dot_general(wrt_ref[...], hb, (((1,), (1,)), ((), ())), preferred_element_type=F32)
    comb_t = _router_dense(logits_t, br_ref[...])
    pad = jnp.zeros((LANES - N_EXPERTS, tm), F32)
    comb_ref[...] = jnp.concatenate([comb_t, pad], axis=0).T


def _outproj(oa, ob, x, mod, l, g, wout, wrt, br, tm, rows_per_seq):
    t = x.shape[0]
    full = lambda a: pl.BlockSpec(a.shape, lambda i: (0,) * a.ndim)
    lay = lambda a: _layer_spec(a, l)
    row = lambda n: pl.BlockSpec((tm, n), lambda i: (i, 0))
    ms = lambda j: _mod_spec(mod, l, j, tm, rows_per_seq, t)
    return pl.pallas_call(
        functools.partial(_outproj_kernel, tm=tm),
        grid=(t // tm,),
        in_specs=[row(GDN_WIDTH), row(S5_WIDTH), row(D_MODEL), ms(2), ms(4), ms(3), lay(g), lay(wout),
                  full(wrt), full(br)],
        out_specs=[row(D_MODEL), row(D_MODEL), row(LANES)],
        out_shape=[jax.ShapeDtypeStruct((t, D_MODEL), F32), jax.ShapeDtypeStruct((t, D_MODEL), BF16),
                   jax.ShapeDtypeStruct((t, LANES), F32)],
        compiler_params=_cparams(("arbitrary",)),
        name="outproj_router",
    )(oa, ob, x, mod, mod, mod, g, wout, wrt, br)


def _outproj_sparse_kernel(oa_ref, ob_ref, x_ref, gt_ref, sc_ref, sh_ref, g_ref, wout_ref, wrt_ref, br_ref,
                           xo_ref, hx_ref, route_ref, cnt_ref, run_scr, *, tm):
    i = pl.program_id(0)

    @pl.when(i == 0)
    def _():
        run_scr[...] = jnp.zeros_like(run_scr)

    o = jnp.concatenate([oa_ref[...].astype(BF16), ob_ref[...].astype(BF16)], axis=1)
    mix = jnp.dot(o, wout_ref[...], preferred_element_type=F32)
    x = x_ref[...] + gt_ref[...] * mix
    xo_ref[...] = x
    ms = jnp.mean(x * x, axis=-1, keepdims=True)
    hn = x * lax.rsqrt(ms + NORM_EPS) * g_ref[...]
    hn = hn * (1.0 + sc_ref[...]) + sh_ref[...]
    hb = hn.astype(BF16)
    logits_t = lax.dot_general(wrt_ref[...], hb, (((1,), (1,)), ((), ())), preferred_element_type=F32)
    cls, wlo, whi = _router_sparse(logits_t, br_ref[...])
    sub = lax.broadcasted_iota(jnp.int32, (CLASS_ROWS, tm), 0).astype(F32)
    onehot = sub == cls
    r = lax.broadcasted_iota(jnp.int32, (tm, tm), 0)
    c = lax.broadcasted_iota(jnp.int32, (tm, tm), 1)
    before = jnp.where(r < c, 1.0, 0.0).astype(BF16)
    prefix = jnp.dot(jnp.where(onehot, 1.0, 0.0).astype(BF16), before, preferred_element_type=F32)
    run = run_scr[...]
    rank = jnp.sum(jnp.where(onehot, prefix + run[:, 0:1], 0.0), axis=0, keepdims=True)
    run_scr[...] = run + jnp.sum(jnp.where(onehot, 1.0, 0.0), axis=1, keepdims=True)
    cnt_ref[...] = run_scr[...]
    route = jnp.concatenate([cls, rank, wlo, whi, jnp.zeros((SUBLANES - 4, tm), F32)], axis=0)
    route_ref[...] = route
    hx_ref[:, :D_MODEL] = hn
    hx_ref[:, D_MODEL:] = jnp.concatenate([route, jnp.zeros((LANES - SUBLANES, tm), F32)], axis=0).T


def _outproj_sparse(oa, ob, x, mod, l, g, wout, wrt, br, tm, rows_per_seq):
    t = x.shape[0]
    full = lambda a: pl.BlockSpec(a.shape, lambda i: (0,) * a.ndim)
    lay = lambda a: _layer_spec(a, l)
    row = lambda n: pl.BlockSpec((tm, n), lambda i: (i, 0))
    ms = lambda j: _mod_spec(mod, l, j, tm, rows_per_seq, t)
    return pl.pallas_call(
        functools.partial(_outproj_sparse_kernel, tm=tm),
        grid=(t // tm,),
        in_specs=[row(GDN_WIDTH), row(S5_WIDTH), row(D_MODEL), ms(2), ms(4), ms(3), lay(g), lay(wout),
                  full(wrt), full(br)],
        out_specs=[row(D_MODEL), row(HX_WIDTH), pl.BlockSpec((SUBLANES, tm), lambda i: (0, i)),
                   pl.BlockSpec((CLASS_ROWS, LANES), lambda i: (0, 0))],
        out_shape=[jax.ShapeDtypeStruct((t, D_MODEL), F32), jax.ShapeDtypeStruct((t, HX_WIDTH), F32),
                   jax.ShapeDtypeStruct((SUBLANES, t), F32), jax.ShapeDtypeStruct((CLASS_ROWS, LANES), F32)],
        scratch_shapes=[pltpu.VMEM((CLASS_ROWS, LANES), F32)],
        compiler_params=_cparams(("arbitrary",)),
        name="outproj_route",
    )(oa, ob, x, mod, mod, mod, g, wout, wrt, br)


def _dispatch_kernel(pos_ref, tail_ref, hx_ref, sorted_ref, stage, zero_scr, sems, zsem, *, tm, n_steps):
    i = pl.program_id(0)
    slot = i % 2

    def tail_copy(cidx):
        start = pl.multiple_of(jnp.maximum(tail_ref[cidx], 0), MOE_TILE)
        return pltpu.make_async_copy(zero_scr, sorted_ref.at[pl.ds(start, MOE_TILE), :], zsem)

    @pl.when(i == 0)
    def _():
        zero_scr[...] = jnp.zeros_like(zero_scr)
        for cidx in range(2 * N_CLASSES):
            @pl.when(tail_ref[cidx] >= 0)
            def _():
                tail_copy(cidx).start()
        for cidx in range(2 * N_CLASSES):
            @pl.when(tail_ref[cidx] >= 0)
            def _():
                tail_copy(cidx).wait()

    def row_copy(s, r, p):
        return pltpu.make_async_copy(stage.at[s, pl.ds(r, 1), :], sorted_ref.at[pl.ds(p, 1), :], sems.at[s])

    def wait_slot(s):
        pltpu.make_async_copy(stage.at[s], sorted_ref.at[pl.ds(0, tm), :], sems.at[s]).wait()

    @pl.when(i >= 2)
    def _():
        wait_slot(slot)

    stage[slot] = hx_ref[...]
    base = i * tm

    def issue(g, carry):
        r8 = pl.multiple_of(g * SUBLANES, SUBLANES)
        for k in range(SUBLANES):
            row_copy(slot, r8 + k, pos_ref[base + r8 + k]).start(priority=k % 2)
        return carry

    lax.fori_loop(0, tm // SUBLANES, issue, 0)

    @pl.when(i == n_steps - 1)
    def _():
        wait_slot(slot)
        if n_steps > 1:
            wait_slot(1 - slot)


def _dispatch(pos, tail, hx, n_rows, tm):
    t = hx.shape[0]
    n_steps = t // tm
    return pl.pallas_call(
        functools.partial(_dispatch_kernel, tm=tm, n_steps=n_steps),
        grid_spec=pltpu.PrefetchScalarGridSpec(
            num_scalar_prefetch=2,
            grid=(n_steps,),
            in_specs=[pl.BlockSpec((tm, HX_WIDTH), lambda i, pos, tail: (i, 0))],
            out_specs=pl.BlockSpec(memory_space=pl.ANY),
            scratch_shapes=[pltpu.VMEM((2, tm, HX_WIDTH), F32), pltpu.VMEM((MOE_TILE, HX_WIDTH), F32),
                            pltpu.SemaphoreType.DMA((2,)), pltpu.SemaphoreType.DMA(())]),
        out_shape=jax.ShapeDtypeStruct((n_rows, HX_WIDTH), F32),
        compiler_params=_cparams(("arbitrary",)),
        name="moe_dispatch",
    )(pos, tail, hx)


def _moe_sorted_kernel(lo_ref, hi_ref, src_ref, nv_ref, xs_ref, wg_lo_ref, wu_lo_ref, wd_lo_ref, wg_hi_ref,
                       wu_hi_ref, wd_hi_ref, o_ref):
    j = pl.program_id(0)

    @pl.when(j < nv_ref[0])
    def _():
        xs = xs_ref[...]
        x = xs[:, :D_MODEL].astype(BF16)
        wlo = xs[:, D_MODEL + 2:D_MODEL + 3]
        whi = xs[:, D_MODEL + 3:D_MODEL + 4]
        dot = functools.partial(jnp.dot, preferred_element_type=F32)
        a_lo = _silu(dot(x, wg_lo_ref[...])) * dot(x, wu_lo_ref[...]) * wlo
        a_hi = _silu(dot(x, wg_hi_ref[...])) * dot(x, wu_hi_ref[...]) * whi
        o_ref[...] = dot(a_lo.astype(BF16), wd_lo_ref[...]) + dot(a_hi.astype(BF16), wd_hi_ref[...])

    @pl.when(j >= nv_ref[0])
    def _():
        o_ref[...] = jnp.zeros_like(o_ref)


def _moe_sorted(tile_lo, tile_hi, tile_src, n_valid, xs, wg, wu, wd):
    n_rows = xs.shape[0]
    n_tiles = n_rows // MOE_TILE
    wspec = lambda w, which: pl.BlockSpec(
        (None,) + w.shape[1:], lambda j, lo, hi, src, nv: ((lo, hi)[which][j], 0, 0))
    return pl.pallas_call(
        _moe_sorted_kernel,
        grid_spec=pltpu.PrefetchScalarGridSpec(
            num_scalar_prefetch=4,
            grid=(n_tiles,),
            in_specs=[pl.BlockSpec((MOE_TILE, HX_WIDTH), lambda j, lo, hi, src, nv: (src[j], 0)),
                      wspec(wg, 0), wspec(wu, 0), wspec(wd, 0), wspec(wg, 1), wspec(wu, 1), wspec(wd, 1)],
            out_specs=pl.BlockSpec((MOE_TILE, D_MODEL), lambda j, lo, hi, src, nv: (j, 0))),
        out_shape=jax.ShapeDtypeStruct((n_rows, D_MODEL), F32),
        compiler_params=_cparams(("arbitrary",)),
        name="moe_sorted",
    )(tile_lo, tile_hi, tile_src, n_valid, xs, wg, wu, wd, wg, wu, wd)


def _combine_kernel(pos_ref, ys_ref, x_ref, gt_ref, fg_ref, o_ref, buf, sems, *, tm, n_steps, final_norm):
    i = pl.program_id(0)
    slot = i % 2

    def row_copy(s, r, p):
        return pltpu.make_async_copy(ys_ref.at[pl.ds(p, 1), :], buf.at[s, pl.ds(r, 1), :], sems.at[s])

    def issue(step, s):
        base = step * tm

        def body(g, carry):
            r8 = pl.multiple_of(g * SUBLANES, SUBLANES)
            for k in range(SUBLANES):
                row_copy(s, r8 + k, pos_ref[base + r8 + k]).start(priority=k % 2)
            return carry

        lax.fori_loop(0, tm // SUBLANES, body, 0)

    @pl.when(i == 0)
    def _():
        issue(0, 0)

    @pl.when(i + 1 < n_steps)
    def _():
        issue(i + 1, 1 - slot)

    pltpu.make_async_copy(ys_ref.at[pl.ds(0, tm), :], buf.at[slot], sems.at[slot]).wait()
    x = x_ref[...] + gt_ref[...] * buf[slot]
    if final_norm:
        x = x * lax.rsqrt(jnp.mean(x * x, axis=-1, keepdims=True) + NORM_EPS) * fg_ref[...]
    o_ref[...] = x


def _combine(pos, ys, x, mod, l, fg, tm, rows_per_seq, final_norm):
    t = x.shape[0]
    n_steps = t // tm
    return pl.pallas_call(
        functools.partial(_combine_kernel, tm=tm, n_steps=n_steps, final_norm=final_norm),
        grid_spec=pltpu.PrefetchScalarGridSpec(
            num_scalar_prefetch=1,
            grid=(n_steps,),
            in_specs=[pl.BlockSpec(memory_space=pl.ANY),
                      pl.BlockSpec((tm, D_MODEL), lambda i, pos: (i, 0)),
                      _mod_spec(mod, l, 5, tm, rows_per_seq, t),
                      pl.BlockSpec(fg.shape, lambda i, pos: (0, 0))],
            out_specs=pl.BlockSpec((tm, D_MODEL), lambda i, pos: (i, 0)),
            scratch_shapes=[pltpu.VMEM((2, tm, D_MODEL), F32), pltpu.SemaphoreType.DMA((2,))]),
        out_shape=jax.ShapeDtypeStruct((t, D_MODEL), F32),
        compiler_params=_cparams(("arbitrary",)),
        name="moe_combine",
    )(pos, ys, x, mod, fg)


def _route_plan(route, counts, n_tokens):
    cnt = counts[:N_CLASSES, 0].astype(jnp.int32)
    padded = ((cnt + MOE_TILE - 1) // MOE_TILE) * MOE_TILE
    ends = jnp.cumsum(padded)
    offsets = ends - padded
    cls = route[0].astype(jnp.int32)
    pos = route[1].astype(jnp.int32) + jnp.sum(
        jnp.where(cls[:, None] == jnp.arange(N_CLASSES, dtype=jnp.int32)[None, :], offsets[None, :], 0), axis=1)
    n_tiles = n_tokens // MOE_TILE + N_CLASSES
    n_valid = ends[-1] // MOE_TILE
    src = jnp.minimum(jnp.arange(n_tiles, dtype=jnp.int32), jnp.maximum(n_valid - 1, 0))
    tile_cls = jnp.minimum(jnp.sum((src[:, None] * MOE_TILE >= ends[None, :]).astype(jnp.int32), axis=1),
                           N_CLASSES - 1)
    group = tile_cls // len(PAIR_CODES)
    pair = jnp.asarray(PAIR_CODES, dtype=jnp.int32)[tile_cls % len(PAIR_CODES)]
    tile_lo = group * EXPERTS_PER_GROUP + pair[:, 0]
    tile_hi = group * EXPERTS_PER_GROUP + pair[:, 1]
    unused = (n_valid + jnp.arange(N_CLASSES, dtype=jnp.int32)) * MOE_TILE
    tail = jnp.concatenate([jnp.where(cnt > 0, ends - MOE_TILE, -1),
                            jnp.where(unused < n_tiles * MOE_TILE, unused, -1)]).astype(jnp.int32)
    return pos, tail, tile_lo, tile_hi, src, n_valid.reshape(1).astype(jnp.int32), n_tiles * MOE_TILE


def _moe_kernel(hn_ref, comb_ref, x_ref, gt_ref, wg_ref, wu_ref, wd_ref, fg_ref, o_ref, wgb_ref, wub_ref, wdb_ref,
                acc_scr, *, final_norm):
    e = pl.program_id(1)
    wg = wg_ref[...].astype(BF16)
    wu = wu_ref[...].astype(BF16)
    wd = wd_ref[...].astype(BF16)
    wgb_ref[...] = wg
    wub_ref[...] = wu
    wdb_ref[...] = wd

    @pl.when(e == 0)
    def _():
        acc_scr[...] = jnp.zeros_like(acc_scr)

    hg = jnp.dot(hn_ref[...], wg, preferred_element_type=F32)
    hu = jnp.dot(hn_ref[...], wu, preferred_element_type=F32)
    lane = lax.broadcasted_iota(jnp.int32, (1, LANES), 1)
    ce = jnp.sum(jnp.where(lane == e, comb_ref[...], 0.0), axis=-1, keepdims=True)
    act = _silu(hg) * hu * ce
    acc_scr[...] += jnp.dot(act.astype(BF16), wd, preferred_element_type=F32)

    @pl.when(e == pl.num_programs(1) - 1)
    def _():
        x = x_ref[...] + gt_ref[...] * acc_scr[...]
        if final_norm:
            x = x * lax.rsqrt(jnp.mean(x * x, axis=-1, keepdims=True) + NORM_EPS) * fg_ref[...]
        o_ref[...] = x


def _moe(hn, comb, x, mod, l, wg, wu, wd, fg, tm, rows_per_seq, final_norm):
    t = x.shape[0]
    row = lambda n: pl.BlockSpec((tm, n), lambda i, e: (i, 0))
    wspec = lambda w: pl.BlockSpec((None, None) + w.shape[2:], lambda i, e: (l, e, 0, 0))
    bspec = lambda w: pl.BlockSpec((None,) + w.shape[2:], lambda i, e: (e, 0, 0))
    return pl.pallas_call(
        functools.partial(_moe_kernel, final_norm=final_norm),
        grid=(t // tm, N_EXPERTS),
        in_specs=[row(D_MODEL), row(LANES), row(D_MODEL), _mod_spec(mod, l, 5, tm, rows_per_seq, t),
                  wspec(wg), wspec(wu), wspec(wd),
                  pl.BlockSpec(fg.shape, lambda i, e: (0, 0))],
        out_specs=[row(D_MODEL), bspec(wg), bspec(wu), bspec(wd)],
        out_shape=[jax.ShapeDtypeStruct((t, D_MODEL), F32)] + [jax.ShapeDtypeStruct(w.shape[1:], BF16)
                                                                for w in (wg, wu, wd)],
        scratch_shapes=[pltpu.VMEM((tm, D_MODEL), F32)],
        compiler_params=_cparams(("arbitrary", "arbitrary")),
        name="moe",
    )(hn, comb, x, mod, wg, wu, wd, fg)


def _pad_lanes(v, offset):
    return jnp.pad(v, ((0, 0), (offset, LANES - offset - v.shape[1])))[:, None, :]


def _state_to_slab(re, im):
    lead = re.shape[:2]
    r = re.reshape(lead + (S5_SLABS, S5_SLAB_STATE))
    i = im.reshape(lead + (S5_SLABS, S5_SLAB_STATE))
    return jnp.stack([r, i], axis=3).reshape(lead + (S5_SLABS * 2 * S5_SLAB_STATE,))


def _slab_to_state(x):
    lead = x.shape[:2]
    x4 = x.reshape(lead + (S5_SLABS, 2, S5_SLAB_STATE))
    re = x4[:, :, :, 0].reshape(lead + (S5_GROUPS, S5_STATE))
    im = x4[:, :, :, 1].reshape(lead + (S5_GROUPS, S5_STATE))
    return re, im


def kernel(x_prompt, x_sample, c_prompt, c_sample, state_conv, state_gdn, state_s5_re, state_s5_im, norm1_g, norm2_g, w_ada, b_ada, w_in, conv_w, a_log, dt_bias, gdn_norm_g, s5_lambda_re, s5_lambda_im, s5_log_dt, s5_b_re, s5_b_im, s5_c_re, s5_c_im, s5_d, s5_glu_w, s5_glu_b, s5_norm_g, w_out, w_router, b_router, w_gate, w_up, w_down, final_g):
    bp, seq, _ = x_prompt.shape
    bs = x_sample.shape[0]
    tp = bp * seq
    tm_p = 512
    tm_gdn = 1024 if seq % 1024 == 0 else tm_p
    tm_dma = tm_gdn
    tt = 128 if seq % 128 == 0 else seq

    mod_s = _ada(jnp.concatenate([c_sample, c_prompt], axis=0), w_ada, b_ada)
    mod_p = mod_s.reshape(DEPTH, bs + bp, 1, 6 * D_MODEL)

    nba = 2 * GDN_HEADS
    u_start = QKV_WIDTH + GDN_WIDTH
    w_in_packed = jnp.concatenate(
        [w_in[:, :, :u_start], w_in[:, :, u_start + nba:], w_in[:, :, u_start:u_start + nba],
         jnp.zeros((DEPTH, D_MODEL, LANES - nba), F32)], axis=2).astype(BF16)
    g1 = norm1_g[:, None, :]
    g2 = norm2_g[:, None, :]
    alog = _pad_lanes(a_log, GDN_HEADS)
    dtb = _pad_lanes(dt_bias, GDN_HEADS)
    gn = gdn_norm_g[:, None, :]
    a_re, a_im, bb_re, bb_im = _s5_params(s5_lambda_re, s5_lambda_im, s5_log_dt, s5_b_re, s5_b_im)
    a_vec = jnp.concatenate([_slab_vec(a_re), _slab_vec(a_im)], axis=2).reshape(DEPTH, 1, -1)
    bmat = jnp.concatenate([_slab_blockdiag(bb_re), _slab_blockdiag(bb_im)], axis=3).astype(BF16)
    flat = lambda m: m.reshape((DEPTH * S5_GROUPS,) + m.shape[2:])
    cre = jnp.swapaxes(_slab_blockdiag(flat(s5_c_re)), 2, 3).astype(BF16)
    cim = jnp.swapaxes(_slab_blockdiag(flat(s5_c_im)), 2, 3).astype(BF16)
    dvec = s5_d[:, None, :]
    gluw = s5_glu_w.astype(BF16)
    glub = s5_glu_b[:, None, :]
    ng = s5_norm_g[:, None, :]
    wout = w_out.astype(BF16)
    wrt = jnp.transpose(w_router).astype(BF16)
    br = b_router.reshape(N_EXPERTS, 1)
    fg = final_g.reshape(1, D_MODEL)
    conv_s = state_conv.reshape(DEPTH, bs, (CONV_WIDTH - 1) * QKV_WIDTH)

    xp = x_prompt.reshape(tp, D_MODEL)
    xs = x_sample.reshape(bs, D_MODEL)
    outs_p = {k: [] for k in ("conv", "gdn", "s5")}
    outs_s = {k: [] for k in ("conv", "s5")}
    gdn_s_new = None
    zero_conv = jnp.zeros((bp, SUBLANES, QKV_WIDTH), F32)
    zero_gdn = jnp.zeros((bp, GDN_HEADS, GDN_HEAD_DIM, GDN_HEAD_DIM), F32)
    zero_s5 = jnp.zeros((bp, S5_SLABS * 2 * S5_SLAB_STATE), F32)
    x0_s = _state_to_slab(state_s5_re, state_s5_im)

    for l in range(DEPTH):
        last = l == DEPTH - 1

        qkv, z, ba, u = _inproj(xs, mod_s, l, g1, w_in_packed, bs, 1)
        oa, gdn_s_new, cv = _gdn_step(qkv, ba, z, l, conv_w, alog, dtb, gn, conv_s, state_gdn, gdn_s_new)
        ob, xst = _s5(u, x0_s[l], l, a_vec, bmat, cre, cim, dvec, gluw, glub, ng, bs, 1, False)
        xs, hn, comb = _outproj(oa, ob, xs, mod_s, l, g2, wout, wrt, br, bs, 1)
        xs, wg_l, wu_l, wd_l = _moe(hn, comb, xs, mod_s, l, w_gate, w_up, w_down, fg, bs, 1, last)
        outs_s["conv"].append(cv)
        outs_s["s5"].append(xst)

        qkv, z, ba, u = _inproj(xp, mod_p, l, g1, w_in_packed, tm_p, seq)
        oa, sg, cv = _gdn_prompt(qkv, ba, z, l, conv_w, alog, dtb, gn, zero_conv, zero_gdn, tm_gdn)
        ob, xst = _s5(u.reshape(bp, seq, S5_WIDTH), zero_s5, l, a_vec, bmat, cre, cim, dvec, gluw, glub, ng,
                      bp, tt, True, n_sub=4)
        xp, hx, route, counts = _outproj_sparse(oa, ob.reshape(tp, S5_WIDTH), xp, mod_p, l, g2, wout, wrt, br,
                                                tm_p, seq)
        pos, tail, tile_lo, tile_hi, tile_src, n_valid, n_rows = _route_plan(route, counts, tp)
        xsorted = _dispatch(pos, tail, hx, n_rows, tm_dma)
        ys = _moe_sorted(tile_lo, tile_hi, tile_src, n_valid, xsorted, wg_l, wu_l, wd_l)
        xp = _combine(pos, ys, xp, mod_p, l, fg, tm_dma, seq, last)
        outs_p["conv"].append(cv)
        outs_p["gdn"].append(sg)
        outs_p["s5"].append(xst)

    st = lambda d, k: jnp.stack(d[k])
    re_p, im_p = _slab_to_state(st(outs_p, "s5"))
    re_s, im_s = _slab_to_state(st(outs_s, "s5"))
    conv_p = st(outs_p, "conv")[:, :, SUBLANES - (CONV_WIDTH - 1):, :]
    conv_s_new = st(outs_s, "conv").reshape(DEPTH, bs, CONV_WIDTH - 1, QKV_WIDTH)
    return (xp.reshape(bp, seq, D_MODEL), xs.reshape(bs, 1, D_MODEL),
            conv_p, st(outs_p, "gdn"), re_p, im_p, conv_s_new, gdn_s_new, re_s, im_s)
```

```python
import functools
import math

import jax
import jax.numpy as jnp
from jax import lax
from jax.experimental import pallas as pl
from jax.experimental.pallas import tpu as pltpu

F32 = jnp.float32
BF16 = jnp.bfloat16

D_MODEL = 1024
DEPTH = 2
GDN_HEAD_DIM = 128
GDN_WIDTH = 512
GDN_HEADS = 4
CONV_WIDTH = 4
S5_CH_PER_GROUP = 16
S5_WIDTH = 512
S5_GROUPS = 32
S5_STATE = 64
QKV_WIDTH = 3 * GDN_WIDTH
N_EXPERTS = 16
EXPERTS_PER_GROUP = 4
D_EXPERT = 256
NORM_EPS = 1e-6

LANES = 128
SUBLANES = 8
GDN_CHUNK = 128
GDN_CHUNKS_PER_ITER = 2
S5_SLABS = 4
S5_SLAB_STATE = 512
MOE_TILE = 512
PAIR_VALUE = (0.0, 1.0, 2.0, 4.0)
PAIR_CODES = ((0, 1), (0, 2), (1, 2), (0, 3), (1, 3), (2, 3))
N_CLASSES = (N_EXPERTS // EXPERTS_PER_GROUP) * len(PAIR_CODES)
CLASS_ROWS = 32
HX_WIDTH = D_MODEL + LANES
VMEM_LIMIT = 56 * 1024 * 1024


def _cparams(sem):
    return pltpu.CompilerParams(dimension_semantics=sem, vmem_limit_bytes=VMEM_LIMIT)


def _sigmoid(x):
    return 1.0 / (1.0 + jnp.exp(-x))


def _silu(x):
    return x * _sigmoid(x)


def _softplus(x):
    return jnp.maximum(x, 0.0) + jnp.log1p(jnp.exp(-jnp.abs(x)))


def _mm(a, b):
    return jnp.dot(a.astype(BF16), b.astype(BF16), preferred_element_type=F32)


def _ada_kernel(c_ref, w_ref, b_ref, o_ref):
    c = c_ref[...]
    o_ref[0] = _mm(_silu(c), w_ref[0]) + b_ref[0]


def _ada(c_all, w_ada, b_ada):
    rows = c_all.shape[0]
    n_out = w_ada.shape[-1]
    tn = 1536
    return pl.pallas_call(
        _ada_kernel,
        grid=(DEPTH, n_out // tn),
        in_specs=[
            pl.BlockSpec((rows, D_MODEL), lambda l, j: (0, 0)),
            pl.BlockSpec((1, D_MODEL, tn), lambda l, j: (l, 0, j)),
            pl.BlockSpec((1, 1, tn), lambda l, j: (l, 0, j)),
        ],
        out_specs=pl.BlockSpec((1, rows, tn), lambda l, j: (l, 0, j)),
        out_shape=jax.ShapeDtypeStruct((DEPTH, rows, n_out), F32),
        compiler_params=_cparams(("arbitrary", "arbitrary")),
        name="ada_mod",
    )(c_all, w_ada, b_ada.reshape(DEPTH, 1, n_out))


def _inproj_kernel(x_ref, sc_ref, sh_ref, g_ref, wqkv_ref, wz_ref, wba_ref, wu_ref,
                   qkv_ref, z_ref, ba_ref, u_ref):
    x = x_ref[...]
    ms = jnp.mean(x * x, axis=-1, keepdims=True)
    hn = x * lax.rsqrt(ms + NORM_EPS) * g_ref[...]
    hn = hn * (1.0 + sc_ref[...]) + sh_ref[...]
    hb = hn.astype(BF16)
    qkv_ref[...] = jnp.dot(hb, wqkv_ref[...], preferred_element_type=F32)
    z_ref[...] = jnp.dot(hb, wz_ref[...], preferred_element_type=F32)
    ba_ref[...] = jnp.dot(hb, wba_ref[...], preferred_element_type=F32)
    u_ref[...] = jnp.dot(hb, wu_ref[...], preferred_element_type=F32)


def _mod_spec(mod, l, j, tm, rows_per_seq, t):
    if mod.ndim == 4:
        tiles_per_seq = rows_per_seq // tm
        row0 = mod.shape[1] - t // rows_per_seq
        return pl.BlockSpec((None, None, 1, D_MODEL), lambda i, *_: (l, row0 + i // tiles_per_seq, 0, j))
    return pl.BlockSpec((None, tm, D_MODEL), lambda i, *_: (l, i, j))


def _layer_spec(arr, l):
    return pl.BlockSpec((None,) + arr.shape[1:], lambda *_: (l,) + (0,) * (arr.ndim - 1))


def _inproj(x, mod, l, g, w_in_packed, tm, rows_per_seq):
    t = x.shape[0]
    row = lambda n: pl.BlockSpec((tm, n), lambda i: (i, 0))
    wcol = lambda width, start: pl.BlockSpec((None, D_MODEL, width), lambda i: (l, 0, start // width))
    u_start = QKV_WIDTH + GDN_WIDTH
    return pl.pallas_call(
        _inproj_kernel,
        grid=(t // tm,),
        in_specs=[row(D_MODEL), _mod_spec(mod, l, 1, tm, rows_per_seq, t), _mod_spec(mod, l, 0, tm, rows_per_seq, t),
                  _layer_spec(g, l), wcol(QKV_WIDTH, 0), wcol(GDN_WIDTH, QKV_WIDTH),
                  wcol(LANES, u_start + S5_WIDTH), wcol(S5_WIDTH, u_start)],
        out_specs=[row(QKV_WIDTH), row(GDN_WIDTH), row(LANES), row(S5_WIDTH)],
        out_shape=[jax.ShapeDtypeStruct((t, QKV_WIDTH), F32), jax.ShapeDtypeStruct((t, GDN_WIDTH), F32),
                   jax.ShapeDtypeStruct((t, LANES), F32), jax.ShapeDtypeStruct((t, S5_WIDTH), F32)],
        compiler_params=_cparams(("arbitrary",)),
        name="inproj",
    )(x, mod, mod, g, w_in_packed, w_in_packed, w_in_packed, w_in_packed)


def _gdn_gates(ba, alog, dtb):
    beta = _sigmoid(ba)
    g = -jnp.exp(alog) * _softplus(ba + dtb)
    return beta, g


def _l2n(x):
    return x * lax.rsqrt(jnp.sum(x * x, axis=-1, keepdims=True) + NORM_EPS)


def _gated_norm(o, gn, z):
    on = o * lax.rsqrt(jnp.mean(o * o, axis=-1, keepdims=True) + NORM_EPS) * gn
    return on * _silu(z)


def _gdn_prompt_kernel(qkv_ref, ba_ref, z_ref, cw_ref, alog_ref, dtb_ref, gn_ref, conv0_ref, s0_ref,
                       o_ref, sout_ref, convout_ref, xp_scr, y_scr, g_scr, b_scr, s_scr, wq_scr, ak_scr, u0_scr,
                       egl_scr, *, tm):
    c_len = GDN_CHUNK
    hd = GDN_HEAD_DIM
    pair_rows = GDN_CHUNKS_PER_ITER * c_len
    n_pairs = tm // pair_rows
    i = pl.program_id(1)
    last = pl.num_programs(1) - 1

    @pl.when(i == 0)
    def _():
        xp_scr[0:SUBLANES, :] = conv0_ref[0]
        s_scr[...] = s0_ref[0]

    xp_scr[SUBLANES:SUBLANES + tm, :] = qkv_ref[...]
    cw = cw_ref[...]
    cw_rows = [cw[j:j + 1, :].reshape(1, 1, QKV_WIDTH) for j in range(CONV_WIDTH)]
    sub = lax.broadcasted_iota(jnp.int32, (1, SUBLANES, QKV_WIDTH), 1)

    def conv_rows(r0, n_rows):
        x3 = xp_scr[r0:r0 + n_rows + SUBLANES, :].reshape(n_rows // SUBLANES + 1, SUBLANES, QKV_WIDTH)

        def delayed(s):
            rot = pltpu.roll(x3, s, axis=1)
            return jnp.where(sub >= s, rot[1:], rot[:-1])

        y = delayed(3) * cw_rows[0]
        y = y + delayed(2) * cw_rows[1]
        y = y + delayed(1) * cw_rows[2]
        y = y + x3[1:] * cw_rows[3]
        y_scr[r0:r0 + n_rows, :] = _silu(y).reshape(n_rows, QKV_WIDTH)

    beta, g = _gdn_gates(ba_ref[...], alog_ref[...], dtb_ref[...])
    b_scr[...] = beta
    g_scr[...] = g

    r = lax.broadcasted_iota(jnp.int32, (c_len, c_len), 0)
    c = lax.broadcasted_iota(jnp.int32, (c_len, c_len), 1)
    ge = r >= c
    gt = r > c
    tri = jnp.where(ge, 1.0, 0.0).astype(BF16)
    eye = jnp.where(r == c, 1.0, 0.0).astype(F32)
    blk16 = (r // 16) == (c // 16)
    pair_masks = [((r // (2 * s)) == (c // (2 * s))) & ((r // s) != (c // s)) for s in (16, 32, 64)]
    gn = gn_ref[...]
    scale = hd ** -0.5

    def prep_stages(p):
        chains = []

        def load():
            for ck in range(GDN_CHUNKS_PER_ITER):
                ci = p * GDN_CHUNKS_PER_ITER + ck
                rows = slice(ci * c_len, (ci + 1) * c_len)
                gch = g_scr[rows, :]
                bch = b_scr[rows, :]
                g1 = gch.astype(BF16)
                r1 = gch - g1.astype(F32)
                g2 = r1.astype(BF16)
                g3 = (r1 - g2.astype(F32)).astype(BF16)
                gcum = (jnp.dot(tri, g1, preferred_element_type=F32)
                        + jnp.dot(tri, g2, preferred_element_type=F32)
                        + jnp.dot(tri, g3, preferred_element_type=F32))
                gcum_t = gcum.T
                glast = gcum[c_len - 1:c_len, :]
                egl_scr[ci] = jnp.broadcast_to(jnp.exp(glast), (SUBLANES, LANES))
                for h in range(GDN_HEADS):
                    lo = h * hd
                    q = _l2n(y_scr[rows, lo:lo + hd]) * scale
                    k = _l2n(y_scr[rows, GDN_WIDTH + lo:GDN_WIDTH + lo + hd])
                    v = y_scr[rows, 2 * GDN_WIDTH + lo:2 * GDN_WIDTH + lo + hd]
                    gl = GDN_HEADS + h
                    gcb = jnp.broadcast_to(gcum[:, gl:gl + 1], (c_len, c_len))
                    bcol = jnp.broadcast_to(bch[:, h:h + 1], (c_len, c_len))
                    egc = jnp.exp(gcb)
                    kdf = jnp.exp(glast[:, gl:gl + 1] - gcb)
                    decay = jnp.where(ge, jnp.exp(gcb - gcum_t[gl:gl + 1, :]), 0.0)
                    kb = k.astype(BF16)
                    chains.append(dict(
                        ci=ci, h=h, decay=decay, bcol=bcol, kb=kb,
                        kq=jnp.concatenate([kb, q.astype(BF16)], axis=0),
                        rhs=jnp.concatenate([((bcol * egc) * k).astype(BF16), (bcol * v).astype(BF16)], axis=1),
                        qg=(q * egc).astype(BF16),
                        kdt=(k * kdf).T.astype(BF16)))

        def gram():
            for ch in chains:
                ch["kkqk"] = lax.dot_general(ch["kq"], ch["kb"], (((1,), (1,)), ((), ())),
                                             preferred_element_type=F32)

        def neumann0():
            for ch in chains:
                ch["lmat"] = jnp.where(gt, ch["bcol"] * ch["kkqk"][:c_len] * ch["decay"], 0.0)
                ch["n1"] = jnp.where(blk16, -ch["lmat"], 0.0)
                ch["t"] = eye + ch["n1"]
            for ch in chains:
                ch["n2"] = _mm(ch["n1"], ch["n1"])

        def neumann1():
            for ch in chains:
                ch["n4"] = _mm(ch["n2"], ch["n2"])
                ch["t"] = ch["t"] + _mm(ch["t"], ch["n2"])

        def neumann2():
            for ch in chains:
                ch["n8"] = _mm(ch["n4"], ch["n4"])
                ch["t"] = ch["t"] + _mm(ch["t"], ch["n4"])

        def neumann3():
            for ch in chains:
                ch["t"] = ch["t"] + _mm(ch["t"], ch["n8"])

        def merge_a(pm):
            def run():
                for ch in chains:
                    ch["x"] = _mm(ch["t"], jnp.where(pm, ch["lmat"], 0.0))
            return run

        def merge_b():
            for ch in chains:
                ch["t"] = ch["t"] - _mm(ch["x"], ch["t"])

        def finish():
            for ch in chains:
                wu = jnp.dot(ch["t"].astype(BF16), ch["rhs"], preferred_element_type=F32)
                ci, h = ch["ci"], ch["h"]
                wq_scr[ci, h] = jnp.concatenate([wu[:, :hd].astype(BF16), ch["qg"]], axis=0)
                u0_scr[ci, h] = wu[:, hd:]
                ak_scr[ci, h] = jnp.concatenate([(ch["kkqk"][c_len:] * ch["decay"]).astype(BF16), ch["kdt"]],
                                                axis=0)

        stages = [load, gram, neumann0, neumann1, neumann2, neumann3]
        for pm in pair_masks:
            stages += [merge_a(pm), merge_b]
        return stages + [finish]

    def recur_stages(ci):
        rows = slice(ci * c_len, (ci + 1) * c_len)
        heads = range(GDN_HEADS)
        st = {}

        def first():
            st["ss"] = [s_scr[h] for h in heads]
            st["wsqs"] = [jnp.dot(wq_scr[ci, h], st["ss"][h].astype(BF16), preferred_element_type=F32)
                          for h in heads]

        def second():
            us = [(u0_scr[ci, h] - st["wsqs"][h][:c_len]).astype(BF16) for h in heads]
            st["auku"] = [jnp.dot(ak_scr[ci, h], us[h], preferred_element_type=F32) for h in heads]

        def third():
            egl = egl_scr[ci]
            for h in heads:
                lo = h * hd
                gl = GDN_HEADS + h
                o = st["wsqs"][h][c_len:] + st["auku"][h][:c_len]
                s_scr[h] = egl[0:1, gl:gl + 1] * st["ss"][h] + st["auku"][h][c_len:]
                o_ref[rows, lo:lo + hd] = _gated_norm(o, gn, z_ref[rows, lo:lo + hd]).astype(o_ref.dtype)

        return [first, second, third]

    def run_interleaved(main, early, side):
        n_main = len(main)
        done = 0
        for idx, stage in enumerate(main):
            stage()
            if idx == 0:
                for extra in early:
                    extra()
            want = (len(side) * (idx + 1)) // n_main
            while done < want:
                side[done]()
                done += 1

    conv_rows(0, pair_rows)
    for p in range(n_pairs):
        early = []
        side = []
        if p + 1 < n_pairs:
            early.append(functools.partial(conv_rows, (p + 1) * pair_rows, pair_rows))
        if p > 0:
            for ck in range(GDN_CHUNKS_PER_ITER):
                side += recur_stages((p - 1) * GDN_CHUNKS_PER_ITER + ck)
        run_interleaved(prep_stages(p), early, side)
    for ck in range(GDN_CHUNKS_PER_ITER):
        for stage in recur_stages((n_pairs - 1) * GDN_CHUNKS_PER_ITER + ck):
            stage()

    tail = xp_scr[tm:tm + SUBLANES, :]
    xp_scr[0:SUBLANES, :] = tail

    @pl.when(i == last)
    def _():
        convout_ref[0] = tail
        sout_ref[0] = s_scr[...]


def _gdn_prompt(qkv, ba, z, l, cw, alog, dtb, gn, conv0, s0, tm):
    n_seq = s0.shape[0]
    t = qkv.shape[0]
    nt = t // n_seq // tm
    row = lambda n: pl.BlockSpec((tm, n), lambda b, i: (b * nt + i, 0))
    lay = lambda a: _layer_spec(a, l)
    hd = GDN_HEAD_DIM
    return pl.pallas_call(
        functools.partial(_gdn_prompt_kernel, tm=tm),
        grid=(n_seq, nt),
        in_specs=[row(QKV_WIDTH), row(LANES), row(GDN_WIDTH), lay(cw), lay(alog), lay(dtb), lay(gn),
                  pl.BlockSpec((1, SUBLANES, QKV_WIDTH), lambda b, i: (b, 0, 0)),
                  pl.BlockSpec((1, GDN_HEADS, hd, hd), lambda b, i: (b, 0, 0, 0))],
        out_specs=[row(GDN_WIDTH),
                   pl.BlockSpec((1, GDN_HEADS, hd, hd), lambda b, i: (b, 0, 0, 0)),
                   pl.BlockSpec((1, SUBLANES, QKV_WIDTH), lambda b, i: (b, 0, 0))],
        out_shape=[jax.ShapeDtypeStruct((t, GDN_WIDTH), BF16),
                   jax.ShapeDtypeStruct((n_seq, GDN_HEADS, hd, hd), F32),
                   jax.ShapeDtypeStruct((n_seq, SUBLANES, QKV_WIDTH), F32)],
        scratch_shapes=[pltpu.VMEM((tm + SUBLANES, QKV_WIDTH), F32), pltpu.VMEM((tm, QKV_WIDTH), F32),
                        pltpu.VMEM((tm, LANES), F32), pltpu.VMEM((tm, LANES), F32),
                        pltpu.VMEM((GDN_HEADS, hd, hd), F32),
                        pltpu.VMEM((tm // GDN_CHUNK, GDN_HEADS, 2 * GDN_CHUNK, hd), BF16),
                        pltpu.VMEM((tm // GDN_CHUNK, GDN_HEADS, 2 * GDN_CHUNK, hd), BF16),
                        pltpu.VMEM((tm // GDN_CHUNK, GDN_HEADS, GDN_CHUNK, hd), F32),
                        pltpu.VMEM((tm // GDN_CHUNK, SUBLANES, LANES), F32)],
        compiler_params=_cparams(("arbitrary", "arbitrary")),
        name="gdn_prompt",
    )(qkv, ba, z, cw, alog, dtb, gn, conv0, s0)


def _gdn_step_kernel(qkv_ref, ba_ref, z_ref, cw_ref, alog_ref, dtb_ref, gn_ref, conv_ref, s_ref, *refs, first):
    nb = SUBLANES
    hd = GDN_HEAD_DIM
    if first:
        o_ref, sout_all_ref, convout_ref = refs
        sout_ref = sout_all_ref.at[0]
        for d in range(1, DEPTH):
            sout_all_ref[d] = jnp.zeros(sout_all_ref.shape[1:], F32)
    else:
        _, o_ref, sout_ref, convout_ref = refs
    x = qkv_ref[...]
    cb = conv_ref[...]
    cw = cw_ref[...]
    b0 = cb[:, 0:QKV_WIDTH]
    b1 = cb[:, QKV_WIDTH:2 * QKV_WIDTH]
    b2 = cb[:, 2 * QKV_WIDTH:3 * QKV_WIDTH]
    y = b0 * cw[0:1, :]
    y = y + b1 * cw[1:2, :]
    y = y + b2 * cw[2:3, :]
    y = y + x * cw[3:4, :]
    y = _silu(y)
    convout_ref[...] = jnp.concatenate([b1, b2, x], axis=1)

    beta, g = _gdn_gates(ba_ref[...], alog_ref[...], dtb_ref[...])
    a = jnp.exp(g)
    gn = gn_ref[...]
    zpad = jnp.zeros((hd - nb, hd), F32)
    for h in range(GDN_HEADS):
        lo = h * hd
        q = _l2n(y[:, lo:lo + hd]) * (hd ** -0.5)
        k = _l2n(y[:, GDN_WIDTH + lo:GDN_WIDTH + lo + hd])
        v = y[:, 2 * GDN_WIDTH + lo:2 * GDN_WIDTH + lo + hd]
        kt = jnp.concatenate([k, zpad], axis=0).T
        qt = jnp.concatenate([q, zpad], axis=0).T
        kq = jnp.sum(k * q, axis=-1, keepdims=True)
        bh = beta[:, h:h + 1]
        ah = a[:, GDN_HEADS + h:GDN_HEADS + h + 1]
        o_rows = []
        for n in range(nb):
            s = s_ref[n, h]
            kc = kt[:, n:n + 1]
            qc = qt[:, n:n + 1]
            rk = jnp.sum(s * kc, axis=0, keepdims=True)
            rq = jnp.sum(s * qc, axis=0, keepdims=True)
            an = ah[n:n + 1, :]
            un = bh[n:n + 1, :] * (v[n:n + 1, :] - an * rk)
            sout_ref[n, h] = an * s + kc * un
            o_rows.append(an * rq + kq[n:n + 1, :] * un)
        o = jnp.concatenate(o_rows, axis=0)
        o_ref[:, lo:lo + hd] = _gated_norm(o, gn, z_ref[:, lo:lo + hd])


def _gdn_step(qkv, ba, z, l, cw, alog, dtb, gn, conv, s, s_new):
    n_seq = qkv.shape[0]
    nb = SUBLANES
    hd = GDN_HEAD_DIM
    first = s_new is None
    row = lambda n: pl.BlockSpec((nb, n), lambda i: (i, 0))
    lay = lambda a: _layer_spec(a, l)
    if first:
        sspec = pl.BlockSpec((DEPTH, nb, GDN_HEADS, hd, hd), lambda i: (0, i, 0, 0, 0))
    else:
        sspec = pl.BlockSpec((None, nb, GDN_HEADS, hd, hd), lambda i: (l, i, 0, 0, 0))
    in_specs = [row(QKV_WIDTH), row(LANES), row(GDN_WIDTH), lay(cw), lay(alog), lay(dtb), lay(gn),
                pl.BlockSpec((None, nb, 3 * QKV_WIDTH), lambda i: (l, i, 0)),
                pl.BlockSpec((None, nb, GDN_HEADS, hd, hd), lambda i: (l, i, 0, 0, 0))]
    args = [qkv, ba, z, cw, alog, dtb, gn, conv, s]
    aliases = {}
    if not first:
        in_specs.append(pl.BlockSpec(memory_space=pl.ANY))
        args.append(s_new)
        aliases = {len(args) - 1: 1}
    return pl.pallas_call(
        functools.partial(_gdn_step_kernel, first=first),
        grid=(n_seq // nb,),
        in_specs=in_specs,
        out_specs=[row(GDN_WIDTH), sspec, row(3 * QKV_WIDTH)],
        out_shape=[jax.ShapeDtypeStruct((n_seq, GDN_WIDTH), F32),
                   jax.ShapeDtypeStruct((DEPTH, n_seq, GDN_HEADS, hd, hd), F32),
                   jax.ShapeDtypeStruct((n_seq, 3 * QKV_WIDTH), F32)],
        input_output_aliases=aliases,
        compiler_params=_cparams(("arbitrary",)),
        name="gdn_step",
    )(*args)


def _s5_param_kernel(lre_ref, lim_ref, ldt_ref, bre_ref, bim_ref, are_ref, aim_ref, bbre_ref, bbim_ref):
    lre = lre_ref[...]
    lim = lim_ref[...]
    dt = jnp.exp(ldt_ref[...])
    mag = jnp.exp(lre * dt)
    are = mag * jnp.cos(lim * dt)
    aim = mag * jnp.sin(lim * dt)
    are_ref[...] = are
    aim_ref[...] = aim
    nre = are - 1.0
    den = lre * lre + lim * lim
    cre = (nre * lre + aim * lim) / den
    cim = (aim * lre - nre * lim) / den
    cre = cre[:, None, :]
    cim = cim[:, None, :]
    bre = bre_ref[...]
    bim = bim_ref[...]
    bbre_ref[...] = cre * bre - cim * bim
    bbim_ref[...] = cre * bim + cim * bre


def _s5_params(lam_re, lam_im, log_dt, b_re, b_im):
    p = lam_re.shape[-1]
    lam_re = lam_re.reshape(-1, p)
    lam_im = lam_im.reshape(-1, p)
    g = lam_re.shape[0]
    bt_re = jnp.swapaxes(b_re.reshape((g,) + b_re.shape[2:]), 1, 2)
    bt_im = jnp.swapaxes(b_im.reshape((g,) + b_im.shape[2:]), 1, 2)
    cg = bt_re.shape[1]
    return pl.pallas_call(
        _s5_param_kernel,
        out_shape=[jax.ShapeDtypeStruct((g, p), F32), jax.ShapeDtypeStruct((g, p), F32),
                   jax.ShapeDtypeStruct((g, cg, p), F32), jax.ShapeDtypeStruct((g, cg, p), F32)],
        name="s5_params",
    )(lam_re, lam_im, log_dt.reshape(g, 1), bt_re, bt_im)


def _slab_blockdiag(m):
    g, cg, p = m.shape
    gl = S5_GROUPS // S5_SLABS
    m4 = m.reshape(g // gl, gl, cg, p)
    eye = jnp.eye(gl, dtype=m.dtype)
    return jnp.einsum('igcp,gh->igchp', m4, eye).reshape(g // S5_GROUPS, S5_SLABS, gl * cg, gl * p)


def _slab_vec(v):
    g, p = v.shape
    return v.reshape(g // S5_GROUPS, S5_SLABS, (S5_GROUPS // S5_SLABS) * p)


def _gelu_tanh(x):
    return 0.5 * x * (1.0 + jnp.tanh(math.sqrt(2.0 / math.pi) * (x + 0.044715 * (x * x * x))))


def _s5_kernel(u_ref, x0_ref, a_ref, bmat_ref, cre_ref, cim_ref, d_ref, gluw_ref, glub_ref, ng_ref,
               o_ref, xout_ref, utb_scr, xs_scr, x_scr, y_scr, ab_scr, *, nb, tt, interleave, n_sub):
    rows = nb * tt
    sub_tt = tt // n_sub
    sub_rows = nb * sub_tt
    ss = S5_SLAB_STATE
    i = pl.program_id(0)

    @pl.when(i == 0)
    def _():
        x_scr[...] = x0_ref[...]
        ab_scr[...] = jnp.broadcast_to(a_ref[...], ab_scr.shape)

    if interleave:
        for b in range(nb):
            for s in range(S5_SLABS):
                utb_scr[pl.ds(s * rows + b, tt, stride=nb), :] = u_ref[b, :, s * LANES:(s + 1) * LANES]
    else:
        for s in range(S5_SLABS):
            utb_scr[s * rows:(s + 1) * rows, :] = u_ref[:, s * LANES:(s + 1) * LANES]

    def u_rows(s, h):
        return utb_scr[s * rows + h * sub_rows:s * rows + (h + 1) * sub_rows, :]

    def in_proj_ops(h):
        def make(s):
            def run():
                xs_scr[h * sub_rows:(h + 1) * sub_rows, s * 2 * ss:(s + 1) * 2 * ss] = jnp.dot(
                    u_rows(s, h).astype(BF16), bmat_ref[s], preferred_element_type=F32)
            return run
        return [make(s) for s in range(S5_SLABS)]

    def scan_ops(h):
        def make(t):
            def run():
                rs = slice(t * nb, (t + 1) * nb)
                for s in range(S5_SLABS):
                    lo = s * 2 * ss
                    ar = ab_scr[:, lo:lo + ss]
                    ai = ab_scr[:, lo + ss:lo + 2 * ss]
                    xr = x_scr[:, lo:lo + ss]
                    xi = x_scr[:, lo + ss:lo + 2 * ss]
                    nr = (ar * xr - ai * xi) + xs_scr[rs, lo:lo + ss]
                    ni = (ar * xi + ai * xr) + xs_scr[rs, lo + ss:lo + 2 * ss]
                    x_scr[:, lo:lo + ss] = nr
                    x_scr[:, lo + ss:lo + 2 * ss] = ni
                    xs_scr[rs, lo:lo + ss] = nr
                    xs_scr[rs, lo + ss:lo + 2 * ss] = ni
            return run
        return [make(t) for t in range(h * sub_tt, (h + 1) * sub_tt)]

    def out_proj_ops(h):
        rs = slice(h * sub_rows, (h + 1) * sub_rows)
        ys = []

        def make(s):
            def run():
                lo = s * 2 * ss
                yr = jnp.dot(xs_scr[rs, lo:lo + ss].astype(BF16), cre_ref[s], preferred_element_type=F32)
                yi = jnp.dot(xs_scr[rs, lo + ss:lo + 2 * ss].astype(BF16), cim_ref[s], preferred_element_type=F32)
                ys.append((yr - yi) + d_ref[0:1, s * LANES:(s + 1) * LANES] * u_rows(s, h))
            return run

        def finish():
            y = _gelu_tanh(jnp.concatenate(ys, axis=1))
            y = y * _sigmoid(_mm(y, gluw_ref[...]) + glub_ref[...])
            y = y * lax.rsqrt(jnp.mean(y * y, axis=-1, keepdims=True) + NORM_EPS) * ng_ref[...]
            if interleave:
                for s in range(S5_SLABS):
                    y_scr[s * rows + h * sub_rows:s * rows + (h + 1) * sub_rows, :] = y[:, s * LANES:(s + 1) * LANES]
            else:
                o_ref[rs, :] = y

        return [make(s) for s in range(S5_SLABS)] + [finish]

    def run_interleaved(main, side):
        done = 0
        for idx, op in enumerate(main):
            op()
            want = (len(side) * (idx + 1)) // len(main)
            while done < want:
                side[done]()
                done += 1

    for op in in_proj_ops(0):
        op()
    for h in range(n_sub):
        side = in_proj_ops(h + 1) if h + 1 < n_sub else []
        if h > 0:
            side = side + out_proj_ops(h - 1)
        run_interleaved(scan_ops(h), side)
    for op in out_proj_ops(n_sub - 1):
        op()

    @pl.when(i == pl.num_programs(0) - 1)
    def _():
        xout_ref[...] = x_scr[...]

    if interleave:
        for b in range(nb):
            for s in range(S5_SLABS):
                o_ref[b, :, s * LANES:(s + 1) * LANES] = y_scr[pl.ds(s * rows + b, tt, stride=nb), :].astype(
                    o_ref.dtype)


def _s5(u, x0, l, a, bmat, cre, cim, d, gluw, glub, ng, nb, tt, interleave, n_sub=1):
    rows = nb * tt
    nstate = x0.shape[1]
    full = lambda arr: pl.BlockSpec(arr.shape, lambda i: (0,) * arr.ndim)
    lay = lambda arr: _layer_spec(arr, l)
    if interleave:
        steps = u.shape[1] // tt
        uspec = pl.BlockSpec((nb, tt, S5_WIDTH), lambda i: (0, i, 0))
        oshape = jax.ShapeDtypeStruct(u.shape, BF16)
    else:
        steps = 1
        uspec = pl.BlockSpec((nb, S5_WIDTH), lambda i: (0, 0))
        oshape = jax.ShapeDtypeStruct(u.shape, F32)
    return pl.pallas_call(
        functools.partial(_s5_kernel, nb=nb, tt=tt, interleave=interleave, n_sub=n_sub),
        grid=(steps,),
        in_specs=[uspec, full(x0), lay(a), lay(bmat), lay(cre), lay(cim), lay(d), lay(gluw), lay(glub),
                  lay(ng)],
        out_specs=[uspec, full(x0)],
        out_shape=[oshape, jax.ShapeDtypeStruct(x0.shape, F32)],
        scratch_shapes=[pltpu.VMEM((S5_SLABS * rows, LANES), F32), pltpu.VMEM((rows, nstate), F32),
                        pltpu.VMEM((nb, nstate), F32), pltpu.VMEM((S5_SLABS * rows, LANES), F32),
                        pltpu.VMEM((nb, nstate), F32)],
        compiler_params=_cparams(("arbitrary",)),
        name="s5_scan",
    )(u, x0, a, bmat, cre, cim, d, gluw, glub, ng)


def _router(logits_t, bias_col):
    scores = _sigmoid(logits_t)
    sel = scores + bias_col
    s = [sel[e:e + 1, :] for e in range(N_EXPERTS)]
    n_groups = N_EXPERTS // EXPERTS_PER_GROUP
    gs = []
    for gi in range(n_groups):
        m = s[gi * EXPERTS_PER_GROUP: (gi + 1) * EXPERTS_PER_GROUP]
        best = None
        for p in range(EXPERTS_PER_GROUP):
            for q in range(p + 1, EXPERTS_PER_GROUP):
                ps = m[p] + m[q]
                best = ps if best is None else jnp.maximum(best, ps)
        gs.append(best)
    gmax = functools.reduce(jnp.maximum, gs)
    taken = None
    in_best = []
    for gi in range(n_groups):
        hit = gs[gi] == gmax
        if taken is None:
            cur = hit
            taken = hit
        else:
            cur = jnp.logical_and(hit, jnp.logical_not(taken))
            taken = jnp.logical_or(taken, hit)
        in_best.append(cur)
    selm = []
    picked = []
    for e in range(N_EXPERTS):
        gi = e // EXPERTS_PER_GROUP
        cnt = jnp.zeros_like(s[e])
        for j in range(gi * EXPERTS_PER_GROUP, (gi + 1) * EXPERTS_PER_GROUP):
            if j == e:
                continue
            beats = (s[j] >= s[e]) if j < e else (s[j] > s[e])
            cnt = cnt + jnp.where(beats, 1.0, 0.0)
        sel_e = jnp.logical_and(in_best[gi], cnt < 1.5)
        selm.append(jnp.where(sel_e, 1.0, 0.0))
        picked.append(jnp.where(sel_e, scores[e:e + 1, :], 0.0))
    denom = functools.reduce(lambda x, y: x + y, picked)
    comb = [p / denom for p in picked]
    return comb, selm, in_best


def _router_dense(logits_t, bias_col):
    comb, _, _ = _router(logits_t, bias_col)
    return jnp.concatenate(comb, axis=0)


def _router_sparse(logits_t, bias_col):
    comb, selm, in_best = _router(logits_t, bias_col)
    cls = None
    wlo = None
    whi = None
    for gi in range(N_EXPERTS // EXPERTS_PER_GROUP):
        term = jnp.where(in_best[gi], float(len(PAIR_CODES) * gi) - 1.0, 0.0)
        cls = term if cls is None else cls + term
        seen = None
        for j in range(EXPERTS_PER_GROUP):
            e = gi * EXPERTS_PER_GROUP + j
            cls = cls + selm[e] * PAIR_VALUE[j]
            first = selm[e] if seen is None else selm[e] * (1.0 - seen)
            seen = selm[e] if seen is None else jnp.maximum(seen, selm[e])
            lo_term = first * comb[e]
            hi_term = (selm[e] - first) * comb[e]
            wlo = lo_term if wlo is None else wlo + lo_term
            whi = hi_term if whi is None else whi + hi_term
    return cls, wlo, whi


def _outproj_kernel(oa_ref, ob_ref, x_ref, gt_ref, sc_ref, sh_ref, g_ref, wout_ref, wrt_ref, br_ref,
                    xo_ref, hn_ref, comb_ref, *, tm):
    o = jnp.concatenate([oa_ref[...].astype(BF16), ob_ref[...].astype(BF16)], axis=1)
    mix = jnp.dot(o, wout_ref[...], preferred_element_type=F32)
    x = x_ref[...] + gt_ref[...] * mix
    xo_ref[...] = x
    ms = jnp.mean(x * x, axis=-1, keepdims=True)
    hn = x * lax.rsqrt(ms + NORM_EPS) * g_ref[...]
    hn = hn * (1.0 + sc_ref[...]) + sh_ref[...]
    hb = hn.astype(BF16)
    hn_ref[...] = hb
    logits_t = lax.dot_general(wrt_ref[...], hb, (((1,), (1,)), ((), ())), preferred_element_type=F32)
    comb_t = _router_dense(logits_t, br_ref[...])
    pad = jnp.zeros((LANES - N_EXPERTS, tm), F32)
    comb_ref[...] = jnp.concatenate([comb_t, pad], axis=0).T


def _outproj(oa, ob, x, mod, l, g, wout, wrt, br, tm, rows_per_seq):
    t = x.shape[0]
    full = lambda a: pl.BlockSpec(a.shape, lambda i: (0,) * a.ndim)
    lay = lambda a: _layer_spec(a, l)
    row = lambda n: pl.BlockSpec((tm, n), lambda i: (i, 0))
    ms = lambda j: _mod_spec(mod, l, j, tm, rows_per_seq, t)
    return pl.pallas_call(
        functools.partial(_outproj_kernel, tm=tm),
        grid=(t // tm,),
        in_specs=[row(GDN_WIDTH), row(S5_WIDTH), row(D_MODEL), ms(2), ms(4), ms(3), lay(g), lay(wout),
                  full(wrt), full(br)],
        out_specs=[row(D_MODEL), row(D_MODEL), row(LANES)],
        out_shape=[jax.ShapeDtypeStruct((t, D_MODEL), F32), jax.ShapeDtypeStruct((t, D_MODEL), BF16),
                   jax.ShapeDtypeStruct((t, LANES), F32)],
        compiler_params=_cparams(("arbitrary",)),
        name="outproj_router",
    )(oa, ob, x, mod, mod, mod, g, wout, wrt, br)


def _outproj_sparse_kernel(oa_ref, ob_ref, x_ref, gt_ref, sc_ref, sh_ref, g_ref, wout_ref, wrt_ref, br_ref,
                           xo_ref, hx_ref, route_ref, cnt_ref, run_scr, *, tm):
    i = pl.program_id(0)

    @pl.when(i == 0)
    def _():
        run_scr[...] = jnp.zeros_like(run_scr)

    o = jnp.concatenate([oa_ref[...].astype(BF16), ob_ref[...].astype(BF16)], axis=1)
    mix = jnp.dot(o, wout_ref[...], preferred_element_type=F32)
    x = x_ref[...] + gt_ref[...] * mix
    xo_ref[...] = x
    ms = jnp.mean(x * x, axis=-1, keepdims=True)
    hn = x * lax.rsqrt(ms + NORM_EPS) * g_ref[...]
    hn = hn * (1.0 + sc_ref[...]) + sh_ref[...]
    hb = hn.astype(BF16)
    logits_t = lax.dot_general(wrt_ref[...], hb, (((1,), (1,)), ((), ())), preferred_element_type=F32)
    cls, wlo, whi = _router_sparse(logits_t, br_ref[...])
    sub = lax.broadcasted_iota(jnp.int32, (CLASS_ROWS, tm), 0).astype(F32)
    onehot = sub == cls
    r = lax.broadcasted_iota(jnp.int32, (tm, tm), 0)
    c = lax.broadcasted_iota(jnp.int32, (tm, tm), 1)
    before = jnp.where(r < c, 1.0, 0.0).astype(BF16)
    prefix = jnp.dot(jnp.where(onehot, 1.0, 0.0).astype(BF16), before, preferred_element_type=F32)
    run = run_scr[...]
    rank = jnp.sum(jnp.where(onehot, prefix + run[:, 0:1], 0.0), axis=0, keepdims=True)
    run_scr[...] = run + jnp.sum(jnp.where(onehot, 1.0, 0.0), axis=1, keepdims=True)
    cnt_ref[...] = run_scr[...]
    route = jnp.concatenate([cls, rank, wlo, whi, jnp.zeros((SUBLANES - 4, tm), F32)], axis=0)
    route_ref[...] = route
    hx_ref[:, :D_MODEL] = hn
    hx_ref[:, D_MODEL:] = jnp.concatenate([route, jnp.zeros((LANES - SUBLANES, tm), F32)], axis=0).T


def _outproj_sparse(oa, ob, x, mod, l, g, wout, wrt, br, tm, rows_per_seq):
    t = x.shape[0]
    full = lambda a: pl.BlockSpec(a.shape, lambda i: (0,) * a.ndim)
    lay = lambda a: _layer_spec(a, l)
    row = lambda n: pl.BlockSpec((tm, n), lambda i: (i, 0))
    ms = lambda j: _mod_spec(mod, l, j, tm, rows_per_seq, t)
    return pl.pallas_call(
        functools.partial(_outproj_sparse_kernel, tm=tm),
        grid=(t // tm,),
        in_specs=[row(GDN_WIDTH), row(S5_WIDTH), row(D_MODEL), ms(2), ms(4), ms(3), lay(g), lay(wout),
                  full(wrt), full(br)],
        out_specs=[row(D_MODEL), row(HX_WIDTH), pl.BlockSpec((SUBLANES, tm), lambda i: (0, i)),
                   pl.BlockSpec((CLASS_ROWS, LANES), lambda i: (0, 0))],
        out_shape=[jax.ShapeDtypeStruct((t, D_MODEL), F32), jax.ShapeDtypeStruct((t, HX_WIDTH), F32),
                   jax.ShapeDtypeStruct((SUBLANES, t), F32), jax.ShapeDtypeStruct((CLASS_ROWS, LANES), F32)],
        scratch_shapes=[pltpu.VMEM((CLASS_ROWS, LANES), F32)],
        compiler_params=_cparams(("arbitrary",)),
        name="outproj_route",
    )(oa, ob, x, mod, mod, mod, g, wout, wrt, br)


def _dispatch_kernel(pos_ref, tail_ref, hx_ref, sorted_ref, stage, zero_scr, sems, zsem, *, tm, n_steps):
    i = pl.program_id(0)
    slot = i % 2

    def tail_copy(cidx):
        start = pl.multiple_of(jnp.maximum(tail_ref[cidx], 0), MOE_TILE)
        return pltpu.make_async_copy(zero_scr, sorted_ref.at[pl.ds(start, MOE_TILE), :], zsem)

    @pl.when(i == 0)
    def _():
        zero_scr[...] = jnp.zeros_like(zero_scr)
        for cidx in range(2 * N_CLASSES):
            @pl.when(tail_ref[cidx] >= 0)
            def _():
                tail_copy(cidx).start()
        for cidx in range(2 * N_CLASSES):
            @pl.when(tail_ref[cidx] >= 0)
            def _():
                tail_copy(cidx).wait()

    def row_copy(s, r, p):
        return pltpu.make_async_copy(stage.at[s, pl.ds(r, 1), :], sorted_ref.at[pl.ds(p, 1), :], sems.at[s])

    def wait_slot(s):
        pltpu.make_async_copy(stage.at[s], sorted_ref.at[pl.ds(0, tm), :], sems.at[s]).wait()

    @pl.when(i >= 2)
    def _():
        wait_slot(slot)

    stage[slot] = hx_ref[...]
    base = i * tm

    def issue(g, carry):
        r8 = pl.multiple_of(g * SUBLANES, SUBLANES)
        for k in range(SUBLANES):
            row_copy(slot, r8 + k, pos_ref[base + r8 + k]).start()
        return carry

    lax.fori_loop(0, tm // SUBLANES, issue, 0)

    @pl.when(i == n_steps - 1)
    def _():
        wait_slot(slot)
        if n_steps > 1:
            wait_slot(1 - slot)


def _dispatch(pos, tail, hx, n_rows, tm):
    t = hx.shape[0]
    n_steps = t // tm
    return pl.pallas_call(
        functools.partial(_dispatch_kernel, tm=tm, n_steps=n_steps),
        grid_spec=pltpu.PrefetchScalarGridSpec(
            num_scalar_prefetch=2,
            grid=(n_steps,),
            in_specs=[pl.BlockSpec((tm, HX_WIDTH), lambda i, pos, tail: (i, 0))],
            out_specs=pl.BlockSpec(memory_space=pl.ANY),
            scratch_shapes=[pltpu.VMEM((2, tm, HX_WIDTH), F32), pltpu.VMEM((MOE_TILE, HX_WIDTH), F32),
                            pltpu.SemaphoreType.DMA((2,)), pltpu.SemaphoreType.DMA(())]),
        out_shape=jax.ShapeDtypeStruct((n_rows, HX_WIDTH), F32),
        compiler_params=_cparams(("arbitrary",)),
        name="moe_dispatch",
    )(pos, tail, hx)


def _moe_sorted_kernel(lo_ref, hi_ref, src_ref, nv_ref, xs_ref, wg_lo_ref, wu_lo_ref, wd_lo_ref, wg_hi_ref,
                       wu_hi_ref, wd_hi_ref, o_ref):
    j = pl.program_id(0)

    @pl.when(j < nv_ref[0])
    def _():
        xs = xs_ref[...]
        x = xs[:, :D_MODEL].astype(BF16)
        wlo = xs[:, D_MODEL + 2:D_MODEL + 3]
        whi = xs[:, D_MODEL + 3:D_MODEL + 4]
        dot = functools.partial(jnp.dot, preferred_element_type=F32)
        a_lo = _silu(dot(x, wg_lo_ref[...])) * dot(x, wu_lo_ref[...]) * wlo
        a_hi = _silu(dot(x, wg_hi_ref[...])) * dot(x, wu_hi_ref[...]) * whi
        o_ref[...] = dot(a_lo.astype(BF16), wd_lo_ref[...]) + dot(a_hi.astype(BF16), wd_hi_ref[...])

    @pl.when(j >= nv_ref[0])
    def _():
        o_ref[...] = jnp.zeros_like(o_ref)


def _moe_sorted(tile_lo, tile_hi, tile_src, n_valid, xs, wg, wu, wd):
    n_rows = xs.shape[0]
    n_tiles = n_rows // MOE_TILE
    wspec = lambda w, which: pl.BlockSpec(
        (None,) + w.shape[1:], lambda j, lo, hi, src, nv: ((lo, hi)[which][j], 0, 0))
    return pl.pallas_call(
        _moe_sorted_kernel,
        grid_spec=pltpu.PrefetchScalarGridSpec(
            num_scalar_prefetch=4,
            grid=(n_tiles,),
            in_specs=[pl.BlockSpec((MOE_TILE, HX_WIDTH), lambda j, lo, hi, src, nv: (src[j], 0)),
                      wspec(wg, 0), wspec(wu, 0), wspec(wd, 0), wspec(wg, 1), wspec(wu, 1), wspec(wd, 1)],
            out_specs=pl.BlockSpec((MOE_TILE, D_MODEL), lambda j, lo, hi, src, nv: (j, 0))),
        out_shape=jax.ShapeDtypeStruct((n_rows, D_MODEL), F32),
        compiler_params=_cparams(("arbitrary",)),
        name="moe_sorted",
    )(tile_lo, tile_hi, tile_src, n_valid, xs, wg, wu, wd, wg, wu, wd)


def _combine_kernel(pos_ref, ys_ref, x_ref, gt_ref, fg_ref, o_ref, buf, sems, *, tm, n_steps, final_norm):
    i = pl.program_id(0)
    slot = i % 2

    def row_copy(s, r, p):
        return pltpu.make_async_copy(ys_ref.at[pl.ds(p, 1), :], buf.at[s, pl.ds(r, 1), :], sems.at[s])

    def issue(step, s):
        base = step * tm

        def body(g, carry):
            r8 = pl.multiple_of(g * SUBLANES, SUBLANES)
            for k in range(SUBLANES):
                row_copy(s, r8 + k, pos_ref[base + r8 + k]).start()
            return carry

        lax.fori_loop(0, tm // SUBLANES, body, 0)

    @pl.when(i == 0)
    def _():
        issue(0, 0)

    @pl.when(i + 1 < n_steps)
    def _():
        issue(i + 1, 1 - slot)

    pltpu.make_async_copy(ys_ref.at[pl.ds(0, tm), :], buf.at[slot], sems.at[slot]).wait()
    x = x_ref[...] + gt_ref[...] * buf[slot]
    if final_norm:
        x = x * lax.rsqrt(jnp.mean(x * x, axis=-1, keepdims=True) + NORM_EPS) * fg_ref[...]
    o_ref[...] = x


def _combine(pos, ys, x, mod, l, fg, tm, rows_per_seq, final_norm):
    t = x.shape[0]
    n_steps = t // tm
    return pl.pallas_call(
        functools.partial(_combine_kernel, tm=tm, n_steps=n_steps, final_norm=final_norm),
        grid_spec=pltpu.PrefetchScalarGridSpec(
            num_scalar_prefetch=1,
            grid=(n_steps,),
            in_specs=[pl.BlockSpec(memory_space=pl.ANY),
                      pl.BlockSpec((tm, D_MODEL), lambda i, pos: (i, 0)),
                      _mod_spec(mod, l, 5, tm, rows_per_seq, t),
                      pl.BlockSpec(fg.shape, lambda i, pos: (0, 0))],
            out_specs=pl.BlockSpec((tm, D_MODEL), lambda i, pos: (i, 0)),
            scratch_shapes=[pltpu.VMEM((2, tm, D_MODEL), F32), pltpu.SemaphoreType.DMA((2,))]),
        out_shape=jax.ShapeDtypeStruct((t, D_MODEL), F32),
        compiler_params=_cparams(("arbitrary",)),
        name="moe_combine",
    )(pos, ys, x, mod, fg)


def _route_plan(route, counts, n_tokens):
    cnt = counts[:N_CLASSES, 0].astype(jnp.int32)
    padded = ((cnt + MOE_TILE - 1) // MOE_TILE) * MOE_TILE
    ends = jnp.cumsum(padded)
    offsets = ends - padded
    cls = route[0].astype(jnp.int32)
    pos = route[1].astype(jnp.int32) + jnp.sum(
        jnp.where(cls[:, None] == jnp.arange(N_CLASSES, dtype=jnp.int32)[None, :], offsets[None, :], 0), axis=1)
    n_tiles = n_tokens // MOE_TILE + N_CLASSES
    n_valid = ends[-1] // MOE_TILE
    src = jnp.minimum(jnp.arange(n_tiles, dtype=jnp.int32), jnp.maximum(n_valid - 1, 0))
    tile_cls = jnp.minimum(jnp.sum((src[:, None] * MOE_TILE >= ends[None, :]).astype(jnp.int32), axis=1),
                           N_CLASSES - 1)
    group = tile_cls // len(PAIR_CODES)
    pair = jnp.asarray(PAIR_CODES, dtype=jnp.int32)[tile_cls % len(PAIR_CODES)]
    tile_lo = group * EXPERTS_PER_GROUP + pair[:, 0]
    tile_hi = group * EXPERTS_PER_GROUP + pair[:, 1]
    unused = (n_valid + jnp.arange(N_CLASSES, dtype=jnp.int32)) * MOE_TILE
    tail = jnp.concatenate([jnp.where(cnt > 0, ends - MOE_TILE, -1),
                            jnp.where(unused < n_tiles * MOE_TILE, unused, -1)]).astype(jnp.int32)
    return pos, tail, tile_lo, tile_hi, src, n_valid.reshape(1).astype(jnp.int32), n_tiles * MOE_TILE


def _moe_kernel(hn_ref, comb_ref, x_ref, gt_ref, wg_ref, wu_ref, wd_ref, fg_ref, o_ref, wgb_ref, wub_ref, wdb_ref,
                acc_scr, *, final_norm):
    e = pl.program_id(1)
    wg = wg_ref[...].astype(BF16)
    wu = wu_ref[...].astype(BF16)
    wd = wd_ref[...].astype(BF16)
    wgb_ref[...] = wg
    wub_ref[...] = wu
    wdb_ref[...] = wd

    @pl.when(e == 0)
    def _():
        acc_scr[...] = jnp.zeros_like(acc_scr)

    hg = jnp.dot(hn_ref[...], wg, preferred_element_type=F32)
    hu = jnp.dot(hn_ref[...], wu, preferred_element_type=F32)
    lane = lax.broadcasted_iota(jnp.int32, (1, LANES), 1)
    ce = jnp.sum(jnp.where(lane == e, comb_ref[...], 0.0), axis=-1, keepdims=True)
    act = _silu(hg) * hu * ce
    acc_scr[...] += jnp.dot(act.astype(BF16), wd, preferred_element_type=F32)

    @pl.when(e == pl.num_programs(1) - 1)
    def _():
        x = x_ref[...] + gt_ref[...] * acc_scr[...]
        if final_norm:
            x = x * lax.rsqrt(jnp.mean(x * x, axis=-1, keepdims=True) + NORM_EPS) * fg_ref[...]
        o_ref[...] = x


def _moe(hn, comb, x, mod, l, wg, wu, wd, fg, tm, rows_per_seq, final_norm):
    t = x.shape[0]
    row = lambda n: pl.BlockSpec((tm, n), lambda i, e: (i, 0))
    wspec = lambda w: pl.BlockSpec((None, None) + w.shape[2:], lambda i, e: (l, e, 0, 0))
    bspec = lambda w: pl.BlockSpec((None,) + w.shape[2:], lambda i, e: (e, 0, 0))
    return pl.pallas_call(
        functools.partial(_moe_kernel, final_norm=final_norm),
        grid=(t // tm, N_EXPERTS),
        in_specs=[row(D_MODEL), row(LANES), row(D_MODEL), _mod_spec(mod, l, 5, tm, rows_per_seq, t),
                  wspec(wg), wspec(wu), wspec(wd),
                  pl.BlockSpec(fg.shape, lambda i, e: (0, 0))],
        out_specs=[row(D_MODEL), bspec(wg), bspec(wu), bspec(wd)],
        out_shape=[jax.ShapeDtypeStruct((t, D_MODEL), F32)] + [jax.ShapeDtypeStruct(w.shape[1:], BF16)
                                                                for w in (wg, wu, wd)],
        scratch_shapes=[pltpu.VMEM((tm, D_MODEL), F32)],
        compiler_params=_cparams(("arbitrary", "arbitrary")),
        name="moe",
    )(hn, comb, x, mod, wg, wu, wd, fg)


def _pad_lanes(v, offset):
    return jnp.pad(v, ((0, 0), (offset, LANES - offset - v.shape[1])))[:, None, :]


def _state_to_slab(re, im):
    lead = re.shape[:2]
    r = re.reshape(lead + (S5_SLABS, S5_SLAB_STATE))
    i = im.reshape(lead + (S5_SLABS, S5_SLAB_STATE))
    return jnp.stack([r, i], axis=3).reshape(lead + (S5_SLABS * 2 * S5_SLAB_STATE,))


def _slab_to_state(x):
    lead = x.shape[:2]
    x4 = x.reshape(lead + (S5_SLABS, 2, S5_SLAB_STATE))
    re = x4[:, :, :, 0].reshape(lead + (S5_GROUPS, S5_STATE))
    im = x4[:, :, :, 1].reshape(lead + (S5_GROUPS, S5_STATE))
    return re, im


def kernel(x_prompt, x_sample, c_prompt, c_sample, state_conv, state_gdn, state_s5_re, state_s5_im, norm1_g, norm2_g, w_ada, b_ada, w_in, conv_w, a_log, dt_bias, gdn_norm_g, s5_lambda_re, s5_lambda_im, s5_log_dt, s5_b_re, s5_b_im, s5_c_re, s5_c_im, s5_d, s5_glu_w, s5_glu_b, s5_norm_g, w_out, w_router, b_router, w_gate, w_up, w_down, final_g):
    bp, seq, _ = x_prompt.shape
    bs = x_sample.shape[0]
    tp = bp * seq
    tm_p = 512
    tm_gdn = 1024 if seq % 1024 == 0 else tm_p
    tm_dma = tm_gdn
    tt = 128 if seq % 128 == 0 else seq

    mod_s = _ada(jnp.concatenate([c_sample, c_prompt], axis=0), w_ada, b_ada)
    mod_p = mod_s.reshape(DEPTH, bs + bp, 1, 6 * D_MODEL)

    nba = 2 * GDN_HEADS
    u_start = QKV_WIDTH + GDN_WIDTH
    w_in_packed = jnp.concatenate(
        [w_in[:, :, :u_start], w_in[:, :, u_start + nba:], w_in[:, :, u_start:u_start + nba],
         jnp.zeros((DEPTH, D_MODEL, LANES - nba), F32)], axis=2).astype(BF16)
    g1 = norm1_g[:, None, :]
    g2 = norm2_g[:, None, :]
    alog = _pad_lanes(a_log, GDN_HEADS)
    dtb = _pad_lanes(dt_bias, GDN_HEADS)
    gn = gdn_norm_g[:, None, :]
    a_re, a_im, bb_re, bb_im = _s5_params(s5_lambda_re, s5_lambda_im, s5_log_dt, s5_b_re, s5_b_im)
    a_vec = jnp.concatenate([_slab_vec(a_re), _slab_vec(a_im)], axis=2).reshape(DEPTH, 1, -1)
    bmat = jnp.concatenate([_slab_blockdiag(bb_re), _slab_blockdiag(bb_im)], axis=3).astype(BF16)
    flat = lambda m: m.reshape((DEPTH * S5_GROUPS,) + m.shape[2:])
    cre = jnp.swapaxes(_slab_blockdiag(flat(s5_c_re)), 2, 3).astype(BF16)
    cim = jnp.swapaxes(_slab_blockdiag(flat(s5_c_im)), 2, 3).astype(BF16)
    dvec = s5_d[:, None, :]
    gluw = s5_glu_w.astype(BF16)
    glub = s5_glu_b[:, None, :]
    ng = s5_norm_g[:, None, :]
    wout = w_out.astype(BF16)
    wrt = jnp.transpose(w_router).astype(BF16)
    br = b_router.reshape(N_EXPERTS, 1)
    fg = final_g.reshape(1, D_MODEL)
    conv_s = state_conv.reshape(DEPTH, bs, (CONV_WIDTH - 1) * QKV_WIDTH)

    xp = x_prompt.reshape(tp, D_MODEL)
    xs = x_sample.reshape(bs, D_MODEL)
    outs_p = {k: [] for k in ("conv", "gdn", "s5")}
    outs_s = {k: [] for k in ("conv", "s5")}
    gdn_s_new = None
    zero_conv = jnp.zeros((bp, SUBLANES, QKV_WIDTH), F32)
    zero_gdn = jnp.zeros((bp, GDN_HEADS, GDN_HEAD_DIM, GDN_HEAD_DIM), F32)
    zero_s5 = jnp.zeros((bp, S5_SLABS * 2 * S5_SLAB_STATE), F32)
    x0_s = _state_to_slab(state_s5_re, state_s5_im)

    for l in range(DEPTH):
        last = l == DEPTH - 1

        qkv, z, ba, u = _inproj(xs, mod_s, l, g1, w_in_packed, bs, 1)
        oa, gdn_s_new, cv = _gdn_step(qkv, ba, z, l, conv_w, alog, dtb, gn, conv_s, state_gdn, gdn_s_new)
        ob, xst = _s5(u, x0_s[l], l, a_vec, bmat, cre, cim, dvec, gluw, glub, ng, bs, 1, False)
        xs, hn, comb = _outproj(oa, ob, xs, mod_s, l, g2, wout, wrt, br, bs, 1)
        xs, wg_l, wu_l, wd_l = _moe(hn, comb, xs, mod_s, l, w_gate, w_up, w_down, fg, bs, 1, last)
        outs_s["conv"].append(cv)
        outs_s["s5"].append(xst)

        qkv, z, ba, u = _inproj(xp, mod_p, l, g1, w_in_packed, tm_p, seq)
        oa, sg, cv = _gdn_prompt(qkv, ba, z, l, conv_w, alog, dtb, gn, zero_conv, zero_gdn, tm_gdn)
        ob, xst = _s5(u.reshape(bp, seq, S5_WIDTH), zero_s5, l, a_vec, bmat, cre, cim, dvec, gluw, glub, ng,
                      bp, tt, True, n_sub=4)
        xp, hx, route, counts = _outproj_sparse(oa, ob.reshape(tp, S5_WIDTH), xp, mod_p, l, g2, wout, wrt, br,
                                                tm_p, seq)
        pos, tail, tile_lo, tile_hi, tile_src, n_valid, n_rows = _route_plan(route, counts, tp)
        xsorted = _dispatch(pos, tail, hx, n_rows, tm_dma)
        ys = _moe_sorted(tile_lo, tile_hi, tile_src, n_valid, xsorted, wg_l, wu_l, wd_l)
        xp = _combine(pos, ys, xp, mod_p, l, fg, tm_dma, seq, last)
        outs_p["conv"].append(cv)
        outs_p["gdn"].append(sg)
        outs_p["s5"].append(xst)

    st = lambda d, k: jnp.stack(d[k])
    re_p, im_p = _slab_to_state(st(outs_p, "s5"))
    re_s, im_s = _slab_to_state(st(outs_s, "s5"))
    conv_p = st(outs_p, "conv")[:, :, SUBLANES - (CONV_WIDTH - 1):, :]
    conv_s_new = st(outs_s, "conv").reshape(DEPTH, bs, CONV_WIDTH - 1, QKV_WIDTH)
    return (xp.reshape(bp, seq, D_MODEL), xs.reshape(bs, 1, D_MODEL),
            conv_p, st(outs_p, "gdn"), re_p, im_p, conv_s_new, gdn_s_new, re_s, im_s)
```
